```python
import math
import jax, jax.numpy as jnp
from jax import lax
import numpy as np

D_MODEL = 2048
BATCH = 4
SEQ = 2048
DEPTH = 2
DEC_BATCH = 128
DEC_SEQ = 4
PAST_LEN = 16384
PAGE_SIZE = 128

N_MIXERS = 2
N_A_LAYERS = (DEPTH + 1) // 2
N_B_LAYERS = DEPTH // 2
A_HEADS = 8
A_DK = D_MODEL // (2 * A_HEADS)
A_DV = D_MODEL // A_HEADS
A_CHUNK = 64
B_EXPAND = 2
B_DINNER = B_EXPAND * D_MODEL
B_HEADDIM = 64
B_HEADS = B_DINNER // B_HEADDIM
B_GROUPS = 8
B_DSTATE = 128
B_DCONV = 4
B_CONV_DIM = B_DINNER + 2 * B_GROUPS * B_DSTATE
B_CHUNK = 128
D_FF = 5632
FFN_CONV = 3
EPS = 1e-6

kernel_name = 'hybrid_mlstm_mamba2_convffn_step'

F32 = jnp.float32


def _rmsnorm(x, w):
    xf = x.astype(F32)
    y = xf * lax.rsqrt(jnp.mean(xf * xf, axis=-1, keepdims=True) + EPS)
    return (y * w.astype(F32)).astype(x.dtype)


def _chunk_len(L, c):
    c = min(c, L)
    while L % c:
        c -= 1
    return c


def _causal_dwconv(x, buf, w, b):
    K = w.shape[0]
    L = x.shape[1]
    xp = jnp.concatenate([buf.astype(x.dtype), x], axis=1)
    y = xp[:, 0:L] * w[0]
    for j in range(1, K):
        y = y + xp[:, j:j + L] * w[j]
    return y + b, xp[:, L:]


def _mlstm_chunk_scan(q, k, v, ig, lf, C0, n0, m0):
    Bn, H, L, _ = q.shape
    DV = v.shape[-1]
    c = _chunk_len(L, A_CHUNK)
    nc = L // c

    def to_chunks(a):
        return jnp.moveaxis(a.reshape(a.shape[:2] + (nc, c) + a.shape[3:]), 2, 0)

    causal = jnp.tril(jnp.ones((c, c), dtype=bool))

    def step(carry, inp):
        C, n, m = carry
        qc, kc, vc, ic, fc = inp
        b = jnp.cumsum(fc, axis=-1)
        logD = jnp.where(causal, b[..., :, None] - b[..., None, :] + ic[..., None, :], -jnp.inf)
        inter = b + m[..., None]
        m_t = jnp.maximum(inter, jnp.max(logD, axis=-1))
        Dm = jnp.exp(logD - m_t[..., None])
        w_inter = jnp.exp(inter - m_t)
        s = jnp.einsum('bhtd,bhsd->bhts', qc, kc) * Dm
        num = w_inter[..., None] * jnp.einsum('bhtd,bhde->bhte', qc, C) + jnp.einsum('bhts,bhse->bhte', s, vc)
        den = w_inter * jnp.einsum('bhtd,bhd->bht', qc, n) + jnp.sum(s, axis=-1)
        h = num / jnp.maximum(jnp.abs(den), jnp.exp(-m_t))[..., None]
        m_new = m_t[..., -1]
        w_state = jnp.exp(b[..., -1:] - b + ic - m_new[..., None])
        decay = jnp.exp(inter[..., -1] - m_new)
        C_new = decay[..., None, None] * C + jnp.einsum('bhs,bhsd,bhse->bhde', w_state, kc, vc)
        n_new = decay[..., None] * n + jnp.einsum('bhs,bhsd->bhd', w_state, kc)
        return (C_new, n_new, m_new), h

    (C, n, m), hs = lax.scan(step, (C0, n0, m0), tuple(map(to_chunks, (q, k, v, ig, lf))))
    h = jnp.moveaxis(hs, 0, 2).reshape(Bn, H, L, DV)
    return h, C, n, m


def _mlstm_mixer(h, C0, n0, m0, w_in, b_if, w_out):
    Bn, L, _ = h.shape
    HK, HV = A_HEADS * A_DK, A_HEADS * A_DV
    proj = h @ w_in
    q, k, v, o, gates = jnp.split(proj, [HK, 2 * HK, 2 * HK + HV, 2 * HK + 2 * HV], axis=-1)

    def heads(a, d):
        return a.reshape(Bn, L, A_HEADS, d).transpose(0, 2, 1, 3).astype(F32)

    q = heads(q, A_DK)
    k = heads(k, A_DK) * (A_DK ** -0.5)
    v = heads(v, A_DV)
    gates = (gates.astype(F32) + b_if.astype(F32)).transpose(0, 2, 1)
    ig = gates[:, :A_HEADS]
    lf = jax.nn.log_sigmoid(gates[:, A_HEADS:])
    ht, C, n, m = _mlstm_chunk_scan(q, k, v, ig, lf, C0.astype(F32), n0.astype(F32), m0.astype(F32))
    ht = ht.transpose(0, 2, 1, 3).reshape(Bn, L, HV)
    out = (jax.nn.sigmoid(o.astype(F32)) * ht).astype(h.dtype) @ w_out
    return out, C, n, m


def _ssd_chunk_scan(x, dt, A, Bm, Cm, h0):
    Bn, L, H, P = x.shape
    G, N = Bm.shape[2], Bm.shape[3]
    R = H // G
    c = _chunk_len(L, B_CHUNK)
    nc = L // c
    xs = jnp.moveaxis(x.reshape(Bn, nc, c, G, R, P), 1, 0)
    dts = jnp.moveaxis(dt.reshape(Bn, nc, c, G, R), 1, 0)
    Bs = jnp.moveaxis(Bm.reshape(Bn, nc, c, G, N), 1, 0)
    Cs = jnp.moveaxis(Cm.reshape(Bn, nc, c, G, N), 1, 0)
    A_gr = A.reshape(G, R)
    causal = jnp.tril(jnp.ones((c, c), dtype=bool))[None, :, :, None, None]

    def step(hs, inp):
        xc, dtc, Bc, Cc = inp
        acum = jnp.cumsum(dtc * A_gr, axis=1)
        seg = acum[:, :, None] - acum[:, None, :]
        decay = jnp.exp(jnp.where(causal, seg, -jnp.inf))
        cb = jnp.einsum('btgn,bsgn->btsg', Cc, Bc)
        wts = cb[..., None] * decay * dtc[:, None]
        y = jnp.einsum('btsgr,bsgrp->btgrp', wts, xc)
        y = y + jnp.einsum('btgn,bgrpn->btgrp', Cc, hs) * jnp.exp(acum)[..., None]
        to_end = jnp.exp(acum[:, -1:] - acum) * dtc
        hs = jnp.exp(acum[:, -1])[..., None, None] * hs + jnp.einsum('bsgr,bsgrp,bsgn->bgrpn', to_end, xc, Bc)
        return hs, y

    hT, ys = lax.scan(step, h0.reshape(Bn, G, R, P, N), (xs, dts, Bs, Cs))
    y = jnp.moveaxis(ys, 0, 1).reshape(Bn, L, H, P)
    return y, hT.reshape(Bn, H, P, N)


def _gated_group_rmsnorm(y, z, w):
    g = y * jax.nn.silu(z)
    g = g.reshape(y.shape[:-1] + (B_GROUPS, B_DINNER // B_GROUPS))
    g = g * lax.rsqrt(jnp.mean(g * g, axis=-1, keepdims=True) + EPS)
    return g.reshape(y.shape) * w


def _mamba_mixer(h, ssm0, conv0, w_in, conv_w, conv_b, dt_bias, A_log, D_skip, norm_w, w_out):
    Bn, L, _ = h.shape
    proj = h @ w_in
    z, xBC, dt = jnp.split(proj, [B_DINNER, B_DINNER + B_CONV_DIM], axis=-1)
    xBC, conv_new = _causal_dwconv(xBC, conv0, conv_w, conv_b)
    xBC = jax.nn.silu(xBC.astype(F32))
    xs, Bm, Cm = jnp.split(xBC, [B_DINNER, B_DINNER + B_GROUPS * B_DSTATE], axis=-1)
    xs = xs.reshape(Bn, L, B_HEADS, B_HEADDIM)
    Bm = Bm.reshape(Bn, L, B_GROUPS, B_DSTATE)
    Cm = Cm.reshape(Bn, L, B_GROUPS, B_DSTATE)
    dt = jax.nn.softplus(dt.astype(F32) + dt_bias.astype(F32))
    A = -jnp.exp(A_log.astype(F32))
    y, ssm_new = _ssd_chunk_scan(xs, dt, A, Bm, Cm, ssm0.astype(F32))
    y = y + D_skip.astype(F32)[:, None] * xs
    y = _gated_group_rmsnorm(y.reshape(Bn, L, B_DINNER), z.astype(F32), norm_w.astype(F32))
    return y.astype(h.dtype) @ w_out, ssm_new, conv_new


def _conv_ffn(h, buf, w_up, conv_w, conv_b, w_down):
    u = h @ w_up
    u, buf_new = _causal_dwconv(u, buf, conv_w, conv_b)
    g, val = jnp.split(u, 2, axis=-1)
    return (jax.nn.silu(g) * val) @ w_down, buf_new


def _forward(x, mC, mn, mm, ssm, sconv, fconv, p):
    nC, nn_, nm, nssm, nsconv, nfconv = [], [], [], [], [], []
    for i in range(DEPTH):
        j = i // N_MIXERS
        hn = _rmsnorm(x, p['norm_mix'][i])
        if i % N_MIXERS == 0:
            out, C, n, m = _mlstm_mixer(hn, mC[j], mn[j], mm[j], p['a_w_in'][j], p['a_b_if'][j], p['a_w_out'][j])
            nC.append(C)
            nn_.append(n)
            nm.append(m)
        else:
            out, s_new, c_new = _mamba_mixer(hn, ssm[j], sconv[j], p['b_w_in'][j], p['b_conv_w'][j], p['b_conv_b'][j],
                                             p['b_dt_bias'][j], p['b_A_log'][j], p['b_D'][j], p['b_norm_w'][j], p['b_w_out'][j])
            nssm.append(s_new)
            nsconv.append(c_new)
        x = x + out.astype(x.dtype)
        hn = _rmsnorm(x, p['norm_ffn'][i])
        out, fb = _conv_ffn(hn, fconv[i], p['f_w_up'][i], p['f_conv_w'][i], p['f_conv_b'][i], p['f_w_down'][i])
        nfconv.append(fb)
        x = x + out.astype(x.dtype)
    y = _rmsnorm(x, p['norm_final'])
    return (y, jnp.stack(nC), jnp.stack(nn_), jnp.stack(nm), jnp.stack(nssm), jnp.stack(nsconv), jnp.stack(nfconv))


def setup_inputs(seed: int = 0) -> dict:
    key = jax.random.key(seed)
    ks = iter(jax.random.split(key, 48))

    def nrm(shape, scale):
        return scale * jax.random.normal(next(ks), shape, F32)

    a_cols = 2 * A_HEADS * A_DK + 2 * A_HEADS * A_DV + 2 * A_HEADS
    b_cols = B_DINNER + B_CONV_DIM + B_HEADS
    i_bias = nrm((N_A_LAYERS, A_HEADS), 0.1)
    f_bias = jnp.linspace(3.0, 6.0, A_HEADS, dtype=F32)[None] + nrm((N_A_LAYERS, A_HEADS), 0.1)
    dt0 = jnp.exp(jax.random.uniform(next(ks), (N_B_LAYERS, B_HEADS), F32, math.log(1e-3), math.log(1e-1)))
    b_dt_bias = dt0 + jnp.log(-jnp.expm1(-dt0))
    b_A_log = jnp.log(jax.random.uniform(next(ks), (N_B_LAYERS, B_HEADS), F32, 1.0, 16.0))
    return {
        'x_prompt': nrm((BATCH, SEQ, D_MODEL), 1.0),
        'x_sample': nrm((DEC_BATCH, DEC_SEQ, D_MODEL), 1.0),
        'state_mlstm_C': nrm((N_A_LAYERS, DEC_BATCH, A_HEADS, A_DK, A_DV), 0.3),
        'state_mlstm_n': nrm((N_A_LAYERS, DEC_BATCH, A_HEADS, A_DK), 0.3),
        'state_mlstm_m': nrm((N_A_LAYERS, DEC_BATCH, A_HEADS), 1.0),
        'state_ssm': nrm((N_B_LAYERS, DEC_BATCH, B_HEADS, B_HEADDIM, B_DSTATE), 0.5),
        'state_ssm_conv': nrm((N_B_LAYERS, DEC_BATCH, B_DCONV - 1, B_CONV_DIM), 1.0),
        'state_ffn_conv': nrm((DEPTH, DEC_BATCH, FFN_CONV - 1, 2 * D_FF), 1.0),
        'norm_mix': 1.0 + nrm((DEPTH, D_MODEL), 0.05),
        'norm_ffn': 1.0 + nrm((DEPTH, D_MODEL), 0.05),
        'norm_final': 1.0 + nrm((D_MODEL,), 0.05),
        'a_w_in': nrm((N_A_LAYERS, D_MODEL, a_cols), D_MODEL ** -0.5),
        'a_b_if': jnp.concatenate([i_bias, f_bias], axis=-1),
        'a_w_out': nrm((N_A_LAYERS, A_HEADS * A_DV, D_MODEL), (A_HEADS * A_DV) ** -0.5),
        'b_w_in': nrm((N_B_LAYERS, D_MODEL, b_cols), D_MODEL ** -0.5),
        'b_conv_w': nrm((N_B_LAYERS, B_DCONV, B_CONV_DIM), B_DCONV ** -0.5),
        'b_conv_b': nrm((N_B_LAYERS, B_CONV_DIM), 0.02),
        'b_dt_bias': b_dt_bias,
        'b_A_log': b_A_log,
        'b_D': 1.0 + nrm((N_B_LAYERS, B_HEADS), 0.1),
        'b_norm_w': 1.0 + nrm((N_B_LAYERS, B_DINNER), 0.05),
        'b_w_out': nrm((N_B_LAYERS, B_DINNER, D_MODEL), B_DINNER ** -0.5),
        'f_w_up': nrm((DEPTH, D_MODEL, 2 * D_FF), D_MODEL ** -0.5),
        'f_conv_w': nrm((DEPTH, FFN_CONV, 2 * D_FF), FFN_CONV ** -0.5),
        'f_conv_b': nrm((DEPTH, 2 * D_FF), 0.02),
        'f_w_down': nrm((DEPTH, D_FF, D_MODEL), D_FF ** -0.5),
    }


def reference(x_prompt, x_sample, state_mlstm_C, state_mlstm_n, state_mlstm_m, state_ssm, state_ssm_conv,
              state_ffn_conv, norm_mix, norm_ffn, norm_final, a_w_in, a_b_if, a_w_out, b_w_in, b_conv_w,
              b_conv_b, b_dt_bias, b_A_log, b_D, b_norm_w, b_w_out, f_w_up, f_conv_w, f_conv_b, f_w_down):
    p = {'norm_mix': norm_mix, 'norm_ffn': norm_ffn, 'norm_final': norm_final,
         'a_w_in': a_w_in, 'a_b_if': a_b_if, 'a_w_out': a_w_out,
         'b_w_in': b_w_in, 'b_conv_w': b_conv_w, 'b_conv_b': b_conv_b, 'b_dt_bias': b_dt_bias,
         'b_A_log': b_A_log, 'b_D': b_D, 'b_norm_w': b_norm_w, 'b_w_out': b_w_out,
         'f_w_up': f_w_up, 'f_conv_w': f_conv_w, 'f_conv_b': f_conv_b, 'f_w_down': f_w_down}
    Bp = x_prompt.shape[0]
    z_C = jnp.zeros((N_A_LAYERS, Bp, A_HEADS, A_DK, A_DV), F32)
    z_n = jnp.zeros((N_A_LAYERS, Bp, A_HEADS, A_DK), F32)
    z_m = jnp.zeros((N_A_LAYERS, Bp, A_HEADS), F32)
    z_ssm = jnp.zeros((N_B_LAYERS, Bp, B_HEADS, B_HEADDIM, B_DSTATE), F32)
    z_sconv = jnp.zeros((N_B_LAYERS, Bp, B_DCONV - 1, B_CONV_DIM), x_prompt.dtype)
    z_fconv = jnp.zeros((DEPTH, Bp, FFN_CONV - 1, 2 * D_FF), x_prompt.dtype)
    y_prompt, pC, pn, pm, pssm, psconv, pfconv = _forward(x_prompt, z_C, z_n, z_m, z_ssm, z_sconv, z_fconv, p)
    y_sample, sC, sn, sm, sssm, ssconv, sfconv = _forward(x_sample, state_mlstm_C, state_mlstm_n, state_mlstm_m,
                                                          state_ssm, state_ssm_conv, state_ffn_conv, p)
    return (y_prompt, y_sample, pC, sC, pn, sn, pm, sm, pssm, sssm, psconv, ssconv, pfconv, sfconv)
```

```python
import functools

import jax
import jax.numpy as jnp
from jax import lax
from jax.experimental import pallas as pl
from jax.experimental.pallas import tpu as pltpu

F32 = jnp.float32
BF16 = jnp.bfloat16
SDS = jax.ShapeDtypeStruct

D_MODEL = 2048
A_HEADS = 8
A_DK = 128
A_DV = 256
B_DINNER = 4096
B_HEADDIM = 64
B_HEADS = 64
B_GROUPS = 8
B_DSTATE = 128
B_DCONV = 4
B_CONV_DIM = 6144
D_FF = 5632
FFN_CONV = 3
EPS = 1e-6
NEG_INF = float("-inf")

LANES = 128
SUBLANES = 8
VMEM_LIMIT_BYTES = 56 * 1024 * 1024

HEADS_PER_GROUP = B_HEADS // B_GROUPS
GROUP_CH = HEADS_PER_GROUP * B_HEADDIM


def _params(sem):
    return pltpu.CompilerParams(dimension_semantics=sem, vmem_limit_bytes=VMEM_LIMIT_BYTES)


def _sigmoid(x):
    return 1.0 / (1.0 + jnp.exp(-x))


def _silu(x):
    return x * _sigmoid(x)


def _softplus(x):
    return jnp.maximum(x, 0.0) + jnp.log1p(jnp.exp(-jnp.abs(x)))


def _log_sigmoid(x):
    return -_softplus(-x)


def _rms(x, w):
    ms = jnp.mean(x * x, axis=-1, keepdims=True)
    return x * lax.rsqrt(ms + EPS) * w


def _cumsum_rows(x):
    n = x.shape[0]
    row = lax.broadcasted_iota(jnp.int32, x.shape, 0)
    k = 1
    while k < n:
        x = x + jnp.where(row >= k, pltpu.roll(x, k, 0), 0.0)
        k *= 2
    return x


def _cumsum_lanes(x):
    n = x.shape[1]
    col = lax.broadcasted_iota(jnp.int32, x.shape, 1)
    k = 1
    while k < n:
        x = x + jnp.where(col >= k, pltpu.roll(x, k, 1), 0.0)
        k *= 2
    return x


def _dot(a, b):
    return jnp.dot(a, b, preferred_element_type=F32)


def _dot_nt(a, b):
    return lax.dot_general(a, b, (((1,), (1,)), ((), ())), preferred_element_type=F32)


def _dot_tn(a, b):
    return lax.dot_general(a, b, (((0,), (0,)), ((), ())), preferred_element_type=F32)


def _conv_taps(u, shifted, w_ref, b_ref, ntaps):
    y = u * w_ref[ntaps - 1:ntaps, :]
    for k in range(1, ntaps):
        y = y + shifted(k) * w_ref[ntaps - 1 - k:ntaps - k, :]
    return y + b_ref[...]


def _shift_with_prev(u, prev8, k):
    ext = jnp.concatenate([prev8, u], axis=0)
    return pltpu.roll(ext, k, 0)[SUBLANES:]


def _pad_rows(a, rows):
    if a.shape[0] == rows:
        return a
    return jnp.concatenate([a, jnp.zeros((rows - a.shape[0], a.shape[1]), a.dtype)], axis=0)


def _norm_matmul_kernel(x_ref, nw_ref, w_ref, ws_ref, o_ref, os_ref, hn_ref):
    @pl.when(pl.program_id(1) == 0)
    def _():
        hn = _rms(x_ref[...], nw_ref[...]).astype(BF16)
        hn_ref[...] = hn
        os_ref[...] = _dot(hn, ws_ref[...])

    o_ref[...] = _dot(hn_ref[...], w_ref[...])


def _norm_matmul(x, nw, w, ws, *, tm, tn):
    T, D = x.shape
    N = w.shape[1]
    return pl.pallas_call(
        _norm_matmul_kernel,
        grid=(T // tm, N // tn),
        in_specs=[pl.BlockSpec((tm, D), lambda i, j: (i, 0)),
                  pl.BlockSpec((1, D), lambda i, j: (0, 0)),
                  pl.BlockSpec((D, tn), lambda i, j: (0, j)),
                  pl.BlockSpec((D, LANES), lambda i, j: (0, 0))],
        out_specs=[pl.BlockSpec((tm, tn), lambda i, j: (i, j)),
                   pl.BlockSpec((tm, LANES), lambda i, j: (i, 0))],
        out_shape=[SDS((T, N), F32), SDS((T, LANES), F32)],
        scratch_shapes=[pltpu.VMEM((tm, D), BF16)],
        compiler_params=_params(("arbitrary", "arbitrary")),
        name="norm_matmul",
    )(x, nw, w, ws)


def _matmul_res_kernel(a_ref, w_ref, r_ref, o_ref):
    o_ref[...] = r_ref[...] + _dot(a_ref[...], w_ref[...])


def _matmul_res(a, w, res, *, tm, tn):
    T, K = a.shape
    N = w.shape[1]
    return pl.pallas_call(
        _matmul_res_kernel,
        grid=(T // tm, N // tn),
        in_specs=[pl.BlockSpec((tm, K), lambda i, j: (i, 0)),
                  pl.BlockSpec((K, tn), lambda i, j: (0, j)),
                  pl.BlockSpec((tm, tn), lambda i, j: (i, j))],
        out_specs=pl.BlockSpec((tm, tn), lambda i, j: (i, j)),
        out_shape=SDS((T, N), F32),
        compiler_params=_params(("arbitrary", "arbitrary")),
        name="matmul_res",
    )(a, w, res)


def _rmsnorm_kernel(x_ref, w_ref, o_ref):
    o_ref[...] = _rms(x_ref[...], w_ref[...])


def _rmsnorm(x, w, *, tm):
    T, D = x.shape
    return pl.pallas_call(
        _rmsnorm_kernel,
        grid=(T // tm,),
        in_specs=[pl.BlockSpec((tm, D), lambda i: (i, 0)), pl.BlockSpec((1, D), lambda i: (0, 0))],
        out_specs=pl.BlockSpec((tm, D), lambda i: (i, 0)),
        out_shape=SDS((T, D), F32),
        compiler_params=_params(("arbitrary",)),
        name="rmsnorm",
    )(x, w)


def _ffn_kernel(*refs, tm, tiles_per_seq, seq_len, sample):
    if sample:
        (x_ref, nw_ref, wug_ref, wuv_ref, cwg_ref, cwv_ref, cbg_ref, cbv_ref, wd_ref, ig_ref, iv_ref,
         out_ref, tg_ref, tv_ref, hn_s, acc_s) = refs
    else:
        (x_ref, nw_ref, wug_ref, wuv_ref, cwg_ref, cwv_ref, cbg_ref, cbv_ref, wd_ref,
         out_ref, tg_ref, tv_ref, hn_s, acc_s, cg_s, cv_s) = refs
    i = pl.program_id(0)
    j = pl.program_id(1)

    @pl.when(j == 0)
    def _():
        hn_s[...] = _rms(x_ref[...], nw_ref[...]).astype(BF16)
        acc_s[...] = jnp.zeros_like(acc_s)

    hn = hn_s[...]
    ug = _dot(hn, wug_ref[...])
    uv = _dot(hn, wuv_ref[...])

    if sample:
        tg_ref[...] = ug
        tv_ref[...] = uv
        t_in = lax.broadcasted_iota(jnp.int32, (tm, 1), 0) % seq_len

        def shifter(u, inj_ref):
            return lambda k: jnp.where(t_in >= k, pltpu.roll(u, k, 0), 0.0) + inj_ref[k - 1]

        yg = _conv_taps(ug, shifter(ug, ig_ref), cwg_ref, cbg_ref, FFN_CONV)
        yv = _conv_taps(uv, shifter(uv, iv_ref), cwv_ref, cbv_ref, FFN_CONV)
    else:
        tail_g = ug[tm - SUBLANES:, :]
        tail_v = uv[tm - SUBLANES:, :]
        tg_ref[0] = tail_g
        tv_ref[0] = tail_v
        @pl.when(i % tiles_per_seq == 0)
        def _():
            cg_s[j] = jnp.zeros((SUBLANES, ug.shape[1]), F32)
            cv_s[j] = jnp.zeros((SUBLANES, uv.shape[1]), F32)

        prev_g = cg_s[j]
        prev_v = cv_s[j]
        yg = _conv_taps(ug, lambda k: _shift_with_prev(ug, prev_g, k), cwg_ref, cbg_ref, FFN_CONV)
        yv = _conv_taps(uv, lambda k: _shift_with_prev(uv, prev_v, k), cwv_ref, cbv_ref, FFN_CONV)
        cg_s[j] = tail_g
        cv_s[j] = tail_v

    act = (_silu(yg) * yv).astype(BF16)
    acc_s[...] += _dot(act, wd_ref[...])

    @pl.when(j == pl.num_programs(1) - 1)
    def _():
        out_ref[...] = x_ref[...] + acc_s[...]


def _ffn(x, nw, w_up, conv_w, conv_b, w_down, *, tm, tf, seq_len, inj=None):
    T, D = x.shape
    nj = D_FF // tf
    ni = T // tm
    sample = inj is not None
    tiles_per_seq = max(seq_len // tm, 1)
    in_specs = [pl.BlockSpec((tm, D), lambda i, j: (i, 0)),
                pl.BlockSpec((1, D), lambda i, j: (0, 0)),
                pl.BlockSpec((D, tf), lambda i, j: (0, j)),
                pl.BlockSpec((D, tf), lambda i, j: (0, nj + j)),
                pl.BlockSpec((FFN_CONV, tf), lambda i, j: (0, j)),
                pl.BlockSpec((FFN_CONV, tf), lambda i, j: (0, nj + j)),
                pl.BlockSpec((1, tf), lambda i, j: (0, j)),
                pl.BlockSpec((1, tf), lambda i, j: (0, nj + j)),
                pl.BlockSpec((tf, D), lambda i, j: (j, 0))]
    args = [x, nw, w_up, w_up, conv_w, conv_w, conv_b, conv_b, w_down]
    scratch = [pltpu.VMEM((tm, D), BF16), pltpu.VMEM((tm, D), F32)]
    if sample:
        in_specs += [pl.BlockSpec((FFN_CONV - 1, tm, tf), lambda i, j: (0, i, j)),
                     pl.BlockSpec((FFN_CONV - 1, tm, tf), lambda i, j: (0, i, nj + j))]
        args += [inj, inj]
        tail_specs = [pl.BlockSpec((tm, tf), lambda i, j: (i, j))] * 2
        tail_shapes = [SDS((T, D_FF), F32)] * 2
    else:
        scratch += [pltpu.VMEM((nj, SUBLANES, tf), F32)] * 2
        tail_specs = [pl.BlockSpec((1, SUBLANES, tf), lambda i, j: (i, 0, j))] * 2
        tail_shapes = [SDS((ni, SUBLANES, D_FF), F32)] * 2
    return pl.pallas_call(
        functools.partial(_ffn_kernel, tm=tm, tiles_per_seq=tiles_per_seq, seq_len=seq_len, sample=sample),
        grid=(ni, nj),
        in_specs=in_specs,
        out_specs=[pl.BlockSpec((tm, D), lambda i, j: (i, 0))] + tail_specs,
        out_shape=[SDS((T, D), F32)] + tail_shapes,
        scratch_shapes=scratch,
        compiler_params=_params(("arbitrary", "arbitrary")),
        name="ffn_sample" if sample else "ffn_prompt",
    )(*args)


def _mlstm_intra(q, ks, v, ig_col, lf_cum_col, lf_cum_row, ig_row, m_prev_col, causal):
    log_d = jnp.where(causal, lf_cum_col - lf_cum_row + ig_row, NEG_INF)
    inter = lf_cum_col + m_prev_col
    m_t = jnp.maximum(inter, jnp.max(log_d, axis=1, keepdims=True))
    d_m = jnp.exp(log_d - m_t)
    w_inter = jnp.exp(inter - m_t)
    kb = _pad_rows(ks.astype(BF16), causal.shape[1])
    vb = _pad_rows(v.astype(BF16), causal.shape[1])
    s = _dot_nt(q.astype(BF16), kb) * d_m
    num = _dot(s.astype(BF16), vb)
    den = jnp.sum(s, axis=1, keepdims=True)
    return m_t, w_inter, inter, num, den


def _mlstm_prompt_kernel(q_ref, k_ref, v_ref, o_ref, g_ref, gt_ref, br_ref, bc_ref,
                         hg_ref, co_ref, no_ref, mo_ref, c_s, n_s, m_s, *, c):
    h = pl.program_id(1)
    ci = pl.program_id(2)

    @pl.when(ci == 0)
    def _():
        c_s[...] = jnp.zeros_like(c_s)
        n_s[...] = jnp.zeros_like(n_s)
        m_s[...] = jnp.zeros_like(m_s)

    g = g_ref[...] + br_ref[...]
    lane = lax.broadcasted_iota(jnp.int32, g.shape, 1)
    lf_cum = _cumsum_rows(_log_sigmoid(g))

    def col_of(a, idx):
        return jnp.sum(jnp.where(lane == idx, a, 0.0), axis=1, keepdims=True)

    ig_col = col_of(g, h)
    b_col = col_of(lf_cum, h + A_HEADS)
    gt = gt_ref[...] + bc_ref[...]
    sub = lax.broadcasted_iota(jnp.int32, gt.shape, 0)
    lf_cum_t = _cumsum_lanes(_log_sigmoid(gt))
    ig_row = jnp.sum(jnp.where(sub == h, gt, 0.0), axis=0, keepdims=True)
    b_row = jnp.sum(jnp.where(sub == h + A_HEADS, lf_cum_t, 0.0), axis=0, keepdims=True)

    row = lax.broadcasted_iota(jnp.int32, (c, c), 0)
    col = lax.broadcasted_iota(jnp.int32, (c, c), 1)
    q = q_ref[...]
    ks = k_ref[...] * (A_DK ** -0.5)
    v = v_ref[...]
    m_prev = m_s[:, 0:1]
    m_t, w_inter, inter, num, den = _mlstm_intra(q, ks, v, ig_col, b_col, b_row, ig_row, m_prev, col <= row)
    cmat = c_s[...]
    nvec = n_s[...]
    num = w_inter * _dot(q.astype(BF16), cmat.astype(BF16)) + num
    den = w_inter * jnp.sum(q * nvec, axis=1, keepdims=True) + den
    hh = num / jnp.maximum(jnp.abs(den), jnp.exp(-m_t))
    hg_ref[...] = (_sigmoid(o_ref[...]) * hh).astype(BF16)

    m_new = m_t[c - 1:c, :]
    w_state = jnp.exp(b_col[c - 1:c, :] - b_col + ig_col - m_new)
    decay = jnp.exp(inter[c - 1:c, :] - m_new)
    kw = ks * w_state
    c_new = decay * cmat + _dot_tn(kw.astype(BF16), v.astype(BF16))
    n_new = decay * nvec + jnp.sum(kw, axis=0, keepdims=True)
    c_s[...] = c_new
    n_s[...] = n_new
    m_s[...] = jnp.broadcast_to(m_new, m_s.shape)

    @pl.when(ci == pl.num_programs(2) - 1)
    def _():
        co_ref[0, 0] = c_new
        no_ref[0, 0] = n_new
        mo_ref[0, 0] = jnp.broadcast_to(m_new, (1, LANES))


def _mlstm_prompt(p, gates, gates_t, bias_row, bias_col, *, nseq, seq_len, c):
    T = p.shape[0]
    nc = seq_len // c
    kq = (A_HEADS * A_DK) // A_DK
    kv = (2 * A_HEADS * A_DK) // A_DV
    ko = kv + A_HEADS
    rows = lambda b, h, ci: b * nc + ci
    return pl.pallas_call(
        functools.partial(_mlstm_prompt_kernel, c=c),
        grid=(nseq, A_HEADS, nc),
        in_specs=[pl.BlockSpec((c, A_DK), lambda b, h, ci: (rows(b, h, ci), h)),
                  pl.BlockSpec((c, A_DK), lambda b, h, ci: (rows(b, h, ci), kq + h)),
                  pl.BlockSpec((c, A_DV), lambda b, h, ci: (rows(b, h, ci), kv + h)),
                  pl.BlockSpec((c, A_DV), lambda b, h, ci: (rows(b, h, ci), ko + h)),
                  pl.BlockSpec((c, LANES), lambda b, h, ci: (rows(b, h, ci), 0)),
                  pl.BlockSpec((2 * A_HEADS, c), lambda b, h, ci: (0, rows(b, h, ci))),
                  pl.BlockSpec((1, LANES), lambda b, h, ci: (0, 0)),
                  pl.BlockSpec((2 * A_HEADS, 1), lambda b, h, ci: (0, 0))],
        out_specs=[pl.BlockSpec((c, A_DV), lambda b, h, ci: (rows(b, h, ci), h)),
                   pl.BlockSpec((1, 1, A_DK, A_DV), lambda b, h, ci: (b, h, 0, 0)),
                   pl.BlockSpec((1, 1, 1, A_DK), lambda b, h, ci: (b, h, 0, 0)),
                   pl.BlockSpec((1, 1, 1, LANES), lambda b, h, ci: (b, h, 0, 0))],
        out_shape=[SDS((T, A_HEADS * A_DV), BF16),
                   SDS((nseq, A_HEADS, A_DK, A_DV), F32),
                   SDS((nseq, A_HEADS, 1, A_DK), F32),
                   SDS((nseq, A_HEADS, 1, LANES), F32)],
        scratch_shapes=[pltpu.VMEM((A_DK, A_DV), F32), pltpu.VMEM((1, A_DK), F32), pltpu.VMEM((1, LANES), F32)],
        compiler_params=_params(("arbitrary", "arbitrary", "arbitrary")),
        name="mlstm_prompt",
    )(p, p, p, p, gates, gates_t, bias_row, bias_col)


def _mlstm_sample_kernel(q_ref, k_ref, v_ref, o_ref, g_ref, br_ref, mp_ref, ci_ref, ni_ref,
                         hg_ref, co_ref, no_ref, mo_ref, *, nbs, seq_len):
    rs = nbs * seq_len
    g = g_ref[...] + br_ref[...]
    lf_all = _log_sigmoid(g)
    m_prev_all = mp_ref[...]
    row = lax.broadcasted_iota(jnp.int32, (rs, LANES), 0)
    col = lax.broadcasted_iota(jnp.int32, (rs, LANES), 1)
    same = (row // seq_len) == (col // seq_len)
    causal = same & (col <= row)
    causal_t = same & (row <= col)
    eye = row == col
    last = same & (col % seq_len == seq_len - 1)
    seq_of_row = lax.broadcasted_iota(jnp.int32, (rs, 1), 0) // seq_len
    seq_of_prow = lax.broadcasted_iota(jnp.int32, (LANES, 1), 0) // seq_len

    def to_row(colvec):
        return jnp.sum(jnp.where(eye, colvec, 0.0), axis=0, keepdims=True)

    def at_last(rowvec):
        return jnp.sum(jnp.where(last, rowvec, 0.0), axis=1, keepdims=True)

    m_out = jnp.zeros((rs, LANES), F32)
    for h in range(A_HEADS):
        ig_col = g[:, h:h + 1]
        lf_col = lf_all[:, A_HEADS + h:A_HEADS + h + 1]
        m_prev = m_prev_all[:, h:h + 1]
        ig_row = to_row(ig_col)
        lf_row = to_row(lf_col)
        b_col = jnp.sum(jnp.where(causal, lf_row, 0.0), axis=1, keepdims=True)
        b_row = jnp.sum(jnp.where(causal_t, lf_col, 0.0), axis=0, keepdims=True)
        q = q_ref[:, h * A_DK:(h + 1) * A_DK]
        ks = k_ref[:, h * A_DK:(h + 1) * A_DK] * (A_DK ** -0.5)
        v = v_ref[:, h * A_DV:(h + 1) * A_DV]
        m_t, w_inter, inter, num, den = _mlstm_intra(q, ks, v, ig_col, b_col, b_row, ig_row, m_prev, causal)
        qb = q.astype(BF16)
        num_i = jnp.zeros((rs, A_DV), F32)
        qn = jnp.zeros((rs, 1), F32)
        for bb in range(nbs):
            mine = seq_of_row == bb
            num_i = jnp.where(mine, _dot(qb, ci_ref[bb, h].astype(BF16)), num_i)
            qn = jnp.where(mine, jnp.sum(q * ni_ref[bb, h], axis=1, keepdims=True), qn)
        num = w_inter * num_i + num
        den = w_inter * qn + den
        hh = num / jnp.maximum(jnp.abs(den), jnp.exp(-m_t))
        hg_ref[:, h * A_DV:(h + 1) * A_DV] = (_sigmoid(o_ref[:, h * A_DV:(h + 1) * A_DV]) * hh).astype(BF16)

        m_new = at_last(to_row(m_t))
        w_state = jnp.exp(at_last(b_row) - b_col + ig_col - m_new)
        decay = jnp.exp(at_last(to_row(inter)) - m_new)
        kw = ks * w_state
        kw_t = _pad_rows(kw, LANES).T.astype(BF16)
        v_pad = _pad_rows(v, LANES)
        for bb in range(nbs):
            v_mine = jnp.where(seq_of_prow == bb, v_pad, 0.0).astype(BF16)
            dec = decay[bb * seq_len:bb * seq_len + 1, :]
            co_ref[bb, h] = dec * ci_ref[bb, h] + _dot(kw_t, v_mine)
            no_ref[bb, h] = dec * ni_ref[bb, h] + jnp.sum(jnp.where(seq_of_row == bb, kw, 0.0), axis=0, keepdims=True)
        m_out = jnp.where(col == h, m_new, m_out)
    mo_ref[...] = m_out


def _mlstm_sample(p, gates, bias_row, m_rows, c_in, n_in, *, nbs, seq_len):
    T = p.shape[0]
    rs = nbs * seq_len
    nb = c_in.shape[0]
    hk = A_HEADS * A_DK
    hv = A_HEADS * A_DV
    return pl.pallas_call(
        functools.partial(_mlstm_sample_kernel, nbs=nbs, seq_len=seq_len),
        grid=(nb // nbs,),
        in_specs=[pl.BlockSpec((rs, hk), lambda i: (i, 0)),
                  pl.BlockSpec((rs, hk), lambda i: (i, 1)),
                  pl.BlockSpec((rs, hv), lambda i: (i, 1)),
                  pl.BlockSpec((rs, hv), lambda i: (i, 2)),
                  pl.BlockSpec((rs, LANES), lambda i: (i, 0)),
                  pl.BlockSpec((1, LANES), lambda i: (0, 0)),
                  pl.BlockSpec((rs, LANES), lambda i: (i, 0)),
                  pl.BlockSpec((nbs, A_HEADS, A_DK, A_DV), lambda i: (i, 0, 0, 0)),
                  pl.BlockSpec((nbs, A_HEADS, 1, A_DK), lambda i: (i, 0, 0, 0))],
        out_specs=[pl.BlockSpec((rs, hv), lambda i: (i, 0)),
                   pl.BlockSpec((nbs, A_HEADS, A_DK, A_DV), lambda i: (i, 0, 0, 0)),
                   pl.BlockSpec((nbs, A_HEADS, 1, A_DK), lambda i: (i, 0, 0, 0)),
                   pl.BlockSpec((rs, LANES), lambda i: (i, 0))],
        out_shape=[SDS((T, hv), BF16),
                   SDS((nb, A_HEADS, A_DK, A_DV), F32),
                   SDS((nb, A_HEADS, 1, A_DK), F32),
                   SDS((T, LANES), F32)],
        compiler_params=_params(("arbitrary",)),
        name="mlstm_sample",
    )(p, p, p, p, gates, bias_row, m_rows, c_in, n_in)


def _pair_select(a, h0):
    lane = lax.broadcasted_iota(jnp.int32, (a.shape[0], LANES), 1)
    return jnp.where(lane < B_HEADDIM, a[:, h0:h0 + 1], a[:, h0 + 1:h0 + 2])


def _group_norm_gate(y, z, nw):
    gz = y * _silu(z)
    return gz * lax.rsqrt(jnp.mean(gz * gz, axis=-1, keepdims=True) + EPS) * nw


def _ssd_prompt_kernel(x_ref, b_ref, c_ref, z_ref, dtc_ref, dtr_ref,
                       cwx_ref, cwb_ref, cwc_ref, cbx_ref, cbb_ref, cbc_ref,
                       dbr_ref, dbc_ref, alr_ref, alc_ref, d_ref, nw_ref,
                       y_ref, so_ref, s_s, xp_s, bp_s, cp_s, *, c):
    ci = pl.program_id(2)

    @pl.when(ci == 0)
    def _():
        s_s[...] = jnp.zeros_like(s_s)
        xp_s[...] = jnp.zeros_like(xp_s)
        bp_s[...] = jnp.zeros_like(bp_s)
        cp_s[...] = jnp.zeros_like(cp_s)

    def conv(raw_ref, prev_s, cw_ref, cb_ref):
        raw = raw_ref[...]
        prev = prev_s[...]
        out = _conv_taps(raw, lambda k: _shift_with_prev(raw, prev, k), cw_ref, cb_ref, B_DCONV)
        prev_s[...] = raw[c - SUBLANES:, :]
        return _silu(out)

    x = conv(x_ref, xp_s, cwx_ref, cbx_ref)
    bm = conv(b_ref, bp_s, cwb_ref, cbb_ref)
    cm = conv(c_ref, cp_s, cwc_ref, cbc_ref)

    dt_c = _softplus(dtc_ref[...] + dbr_ref[...])
    ac_c = _cumsum_rows(dt_c * (-jnp.exp(alr_ref[...])))
    dt_r = _softplus(dtr_ref[...] + dbc_ref[...])
    ac_r = _cumsum_lanes(dt_r * (-jnp.exp(alc_ref[...])))

    row = lax.broadcasted_iota(jnp.int32, (c, c), 0)
    col = lax.broadcasted_iota(jnp.int32, (c, c), 1)
    causal = col <= row
    bmb = bm.astype(BF16)
    cmb = cm.astype(BF16)
    cb = _dot_nt(cmb, bmb)
    state = s_s[...]
    y_inter = _dot(cmb, state.astype(BF16))
    ac_last = ac_c[c - 1:c, :]
    to_end = jnp.exp(ac_last - ac_c) * dt_c
    e_ac = jnp.exp(ac_c)
    e_last = jnp.exp(ac_last)
    lane = lax.broadcasted_iota(jnp.int32, (c, LANES), 1)
    d_all = d_ref[...]

    ys = []
    for pr in range(HEADS_PER_GROUP // 2):
        h0 = 2 * pr
        sl = slice(pr * LANES, (pr + 1) * LANES)
        xp = x[:, sl]
        xpb = xp.astype(BF16)
        full = []
        for r in (h0, h0 + 1):
            seg = ac_c[:, r:r + 1] - ac_r[r:r + 1, :]
            wts = cb * jnp.exp(jnp.where(causal, seg, NEG_INF)) * dt_r[r:r + 1, :]
            full.append(_dot(wts.astype(BF16), xpb))
        yp = jnp.where(lane < B_HEADDIM, full[0], full[1])
        yp = yp + y_inter[:, sl] * _pair_select(e_ac, h0) + d_all[:, sl] * xp
        ys.append(yp)
        xw = (xp * _pair_select(to_end, h0)).astype(BF16)
        s_s[:, sl] = _pair_select(e_last, h0) * state[:, sl] + _dot_tn(bmb, xw)
    y = jnp.concatenate(ys, axis=1)
    y_ref[...] = _group_norm_gate(y, z_ref[...], nw_ref[...]).astype(BF16)

    @pl.when(ci == pl.num_programs(2) - 1)
    def _():
        so_ref[0, 0] = s_s[...]


def _ssd_prompt(p, dt_cols, dt_rows, conv_w, conv_b, dtb_cols, dtb_rows, alog_cols, alog_rows, d_lane, norm_w,
                *, nseq, seq_len, c):
    T = p.shape[0]
    nc = seq_len // c
    rows = lambda b, g, ci: b * nc + ci
    kx = B_DINNER // GROUP_CH
    kb = (2 * B_DINNER) // B_DSTATE
    kc = kb + B_GROUPS
    wb = B_DINNER // B_DSTATE
    wc = wb + B_GROUPS
    G = B_GROUPS
    return pl.pallas_call(
        functools.partial(_ssd_prompt_kernel, c=c),
        grid=(nseq, G, nc),
        in_specs=[pl.BlockSpec((c, GROUP_CH), lambda b, g, ci: (rows(b, g, ci), kx + g)),
                  pl.BlockSpec((c, B_DSTATE), lambda b, g, ci: (rows(b, g, ci), kb + g)),
                  pl.BlockSpec((c, B_DSTATE), lambda b, g, ci: (rows(b, g, ci), kc + g)),
                  pl.BlockSpec((c, GROUP_CH), lambda b, g, ci: (rows(b, g, ci), g)),
                  pl.BlockSpec((c, LANES), lambda b, g, ci: (rows(b, g, ci), g)),
                  pl.BlockSpec((HEADS_PER_GROUP, c), lambda b, g, ci: (g, rows(b, g, ci))),
                  pl.BlockSpec((B_DCONV, GROUP_CH), lambda b, g, ci: (0, g)),
                  pl.BlockSpec((B_DCONV, B_DSTATE), lambda b, g, ci: (0, wb + g)),
                  pl.BlockSpec((B_DCONV, B_DSTATE), lambda b, g, ci: (0, wc + g)),
                  pl.BlockSpec((1, GROUP_CH), lambda b, g, ci: (0, g)),
                  pl.BlockSpec((1, B_DSTATE), lambda b, g, ci: (0, wb + g)),
                  pl.BlockSpec((1, B_DSTATE), lambda b, g, ci: (0, wc + g)),
                  pl.BlockSpec((1, LANES), lambda b, g, ci: (0, g)),
                  pl.BlockSpec((HEADS_PER_GROUP, 1), lambda b, g, ci: (g, 0)),
                  pl.BlockSpec((1, LANES), lambda b, g, ci: (0, g)),
                  pl.BlockSpec((HEADS_PER_GROUP, 1), lambda b, g, ci: (g, 0)),
                  pl.BlockSpec((1, GROUP_CH), lambda b, g, ci: (0, g)),
                  pl.BlockSpec((1, GROUP_CH), lambda b, g, ci: (0, g))],
        out_specs=[pl.BlockSpec((c, GROUP_CH), lambda b, g, ci: (rows(b, g, ci), g)),
                   pl.BlockSpec((1, 1, B_DSTATE, GROUP_CH), lambda b, g, ci: (b, g, 0, 0))],
        out_shape=[SDS((T, B_DINNER), BF16), SDS((nseq, G, B_DSTATE, GROUP_CH), F32)],
        scratch_shapes=[pltpu.VMEM((B_DSTATE, GROUP_CH), F32),
                        pltpu.VMEM((SUBLANES, GROUP_CH), F32),
                        pltpu.VMEM((SUBLANES, B_DSTATE), F32),
                        pltpu.VMEM((SUBLANES, B_DSTATE), F32)],
        compiler_params=_params(("arbitrary", "arbitrary", "arbitrary")),
        name="ssd_prompt",
    )(p, p, p, p, dt_cols, dt_rows, conv_w, conv_w, conv_w, conv_b, conv_b, conv_b,
      dtb_cols, dtb_rows, alog_cols, alog_rows, d_lane, norm_w)


def _ssd_sample_kernel(x_ref, b_ref, c_ref, z_ref, dt_ref, ix_ref, ib_ref, ic_ref,
                       cwx_ref, cwb_ref, cwc_ref, cbx_ref, cbb_ref, cbc_ref,
                       dtb_ref, al_ref, d_ref, nw_ref, hs_ref,
                       y_ref, hso_ref, *, nbs, seq_len):
    rs = nbs * seq_len
    t_in = lax.broadcasted_iota(jnp.int32, (rs, 1), 0) % seq_len
    seq_of_row = lax.broadcasted_iota(jnp.int32, (rs, 1), 0) // seq_len
    seq_of_prow = lax.broadcasted_iota(jnp.int32, (LANES, 1), 0) // seq_len
    lane = lax.broadcasted_iota(jnp.int32, (rs, LANES), 1)

    def back(a, k):
        return jnp.where(t_in >= k, pltpu.roll(a, k, 0), 0.0)

    def conv(raw_ref, inj_ref, cw_ref, cb_ref):
        raw = raw_ref[...]
        return _silu(_conv_taps(raw, lambda k: back(raw, k) + inj_ref[k - 1], cw_ref, cb_ref, B_DCONV))

    x = conv(x_ref, ix_ref, cwx_ref, cbx_ref)
    bm = conv(b_ref, ib_ref, cwb_ref, cbb_ref)
    cm = conv(c_ref, ic_ref, cwc_ref, cbc_ref)

    dt = _softplus(dt_ref[...] + dtb_ref[...])
    a = dt * (-jnp.exp(al_ref[...]))
    acum = a
    total = a
    for k in range(1, seq_len):
        acum = acum + back(a, k)
        total = total + back(a, k) + jnp.where(t_in + k < seq_len, pltpu.roll(a, rs - k, 0), 0.0)

    coefs = []
    for d in range(seq_len):
        prod = cm * (back(bm, d) if d else bm)
        cbh = jnp.zeros((rs, LANES), F32)
        for g in range(B_GROUPS):
            red = jnp.sum(prod[:, g * B_DSTATE:(g + 1) * B_DSTATE], axis=1, keepdims=True)
            cbh = jnp.where(lane // HEADS_PER_GROUP == g, red, cbh)
        if d:
            coef = jnp.where(t_in >= d, cbh * jnp.exp(acum - pltpu.roll(acum, d, 0)) * pltpu.roll(dt, d, 0), 0.0)
        else:
            coef = cbh * dt
        coefs.append(coef)
    x_back = [x] + [pltpu.roll(x, d, 0) for d in range(1, seq_len)]

    e_ac = jnp.exp(acum)
    e_tot = jnp.exp(total)
    to_end = jnp.exp(total - acum) * dt
    d_all = d_ref[...]

    for g in range(B_GROUPS):
        gsl = slice(g * GROUP_CH, (g + 1) * GROUP_CH)
        nsl = slice(g * B_DSTATE, (g + 1) * B_DSTATE)
        cmb = cm[:, nsl].astype(BF16)
        bm_pad = _pad_rows(bm[:, nsl], LANES)
        ys = []
        xws = []
        for pr in range(HEADS_PER_GROUP // 2):
            h0 = g * HEADS_PER_GROUP + 2 * pr
            sl = slice(g * GROUP_CH + pr * LANES, g * GROUP_CH + (pr + 1) * LANES)
            yp = d_all[:, sl] * x[:, sl]
            for d in range(seq_len):
                yp = yp + _pair_select(coefs[d], h0) * x_back[d][:, sl]
            ys.append(yp)
            xws.append(x[:, sl] * _pair_select(to_end, h0))
        xw_t = _pad_rows(jnp.concatenate(xws, axis=1), LANES).T.astype(BF16)
        y_int = jnp.zeros((rs, GROUP_CH), F32)
        for bb in range(nbs):
            hs_g = hs_ref[bb, g * HEADS_PER_GROUP:(g + 1) * HEADS_PER_GROUP].reshape(GROUP_CH, B_DSTATE)
            y_int = jnp.where(seq_of_row == bb, _dot_nt(cmb, hs_g.astype(BF16)), y_int)
            b_mine = jnp.where(seq_of_prow == bb, bm_pad, 0.0).astype(BF16)
            upd = _dot(xw_t, b_mine)
            for r in range(HEADS_PER_GROUP):
                hd = g * HEADS_PER_GROUP + r
                dec = e_tot[bb * seq_len:bb * seq_len + 1, hd:hd + 1]
                hso_ref[bb, hd] = dec * hs_ref[bb, hd] + upd[r * B_HEADDIM:(r + 1) * B_HEADDIM, :]
        e_pairs = jnp.concatenate(
            [_pair_select(e_ac, g * HEADS_PER_GROUP + 2 * pr) for pr in range(HEADS_PER_GROUP // 2)], axis=1)
        y = jnp.concatenate(ys, axis=1) + y_int * e_pairs
        y_ref[:, gsl] = _group_norm_gate(y, z_ref[:, gsl], nw_ref[:, gsl]).astype(BF16)


def _ssd_sample(p, dt, inj, conv_w, conv_b, dtb, alog, d_lane, norm_w, hs, *, nbs, seq_len):
    T = p.shape[0]
    rs = nbs * seq_len
    nb = hs.shape[0]
    gn = B_GROUPS * B_DSTATE
    nk = B_DCONV - 1
    whole = lambda shape: pl.BlockSpec(shape, lambda i: (0,) * len(shape))
    return pl.pallas_call(
        functools.partial(_ssd_sample_kernel, nbs=nbs, seq_len=seq_len),
        grid=(nb // nbs,),
        in_specs=[pl.BlockSpec((rs, B_DINNER), lambda i: (i, 1)),
                  pl.BlockSpec((rs, gn), lambda i: (i, 2 * B_DINNER // gn)),
                  pl.BlockSpec((rs, gn), lambda i: (i, 2 * B_DINNER // gn + 1)),
                  pl.BlockSpec((rs, B_DINNER), lambda i: (i, 0)),
                  pl.BlockSpec((rs, LANES), lambda i: (i, 0)),
                  pl.BlockSpec((nk, rs, B_DINNER), lambda i: (0, i, 0)),
                  pl.BlockSpec((nk, rs, gn), lambda i: (0, i, B_DINNER // gn)),
                  pl.BlockSpec((nk, rs, gn), lambda i: (0, i, B_DINNER // gn + 1)),
                  pl.BlockSpec((B_DCONV, B_DINNER), lambda i: (0, 0)),
                  pl.BlockSpec((B_DCONV, gn), lambda i: (0, B_DINNER // gn)),
                  pl.BlockSpec((B_DCONV, gn), lambda i: (0, B_DINNER // gn + 1)),
                  pl.BlockSpec((1, B_DINNER), lambda i: (0, 0)),
                  pl.BlockSpec((1, gn), lambda i: (0, B_DINNER // gn)),
                  pl.BlockSpec((1, gn), lambda i: (0, B_DINNER // gn + 1)),
                  whole((1, LANES)), whole((1, LANES)), whole((1, B_DINNER)), whole((1, B_DINNER)),
                  pl.BlockSpec((nbs, B_HEADS, B_HEADDIM, B_DSTATE), lambda i: (i, 0, 0, 0))],
        out_specs=[pl.BlockSpec((rs, B_DINNER), lambda i: (i, 0)),
                   pl.BlockSpec((nbs, B_HEADS, B_HEADDIM, B_DSTATE), lambda i: (i, 0, 0, 0))],
        out_shape=[SDS((T, B_DINNER), BF16), SDS(hs.shape, F32)],
        compiler_params=_params(("arbitrary",)),
        name="ssd_sample",
    )(p, p, p, p, dt, inj, inj, inj, conv_w, conv_w, conv_w, conv_b, conv_b, conv_b,
      dtb, alog, d_lane, norm_w, hs)


def _conv_inject(buf, seq_len):
    nb, nk, ch = buf.shape
    ext = jnp.concatenate([buf, jnp.zeros((nb, seq_len, ch), buf.dtype)], axis=1)
    inj = [ext[:, nk - k:nk - k + seq_len].reshape(nb * seq_len, ch) for k in range(1, nk + 1)]
    return jnp.stack(inj)


def _pad_lanes(a, width=LANES):
    return jnp.pad(a, ((0, 0), (0, width - a.shape[1])))


def _forward(x, nseq, seq_len, state, w, *, prompt):
    T = x.shape[0]
    tm = 1024 if prompt else T
    hk = A_HEADS * A_DK
    hv = A_HEADS * A_DV

    p_a, gates = _norm_matmul(x, w["norm_mix"][0:1], w["a_w_main"], w["a_w_gate"], tm=tm, tn=1024)
    bias_row = _pad_lanes(w["a_b_if"])
    if prompt:
        gates_t = gates[:, :2 * A_HEADS].T
        hg, m_c, m_n, m_m = _mlstm_prompt(p_a, gates, gates_t, bias_row, w["a_b_if"].T,
                                          nseq=nseq, seq_len=seq_len, c=256)
        m_n = m_n[:, :, 0, :]
        m_m = m_m[:, :, 0, 0]
    else:
        c0, n0, m0 = state["mlstm"]
        m_rows = _pad_lanes(jnp.repeat(m0, seq_len, axis=0))
        hg, m_c, m_n, m_rows_new = _mlstm_sample(p_a, gates, bias_row, m_rows, c0, n0[:, :, None, :],
                                                 nbs=8, seq_len=seq_len)
        m_n = m_n[:, :, 0, :]
        m_m = m_rows_new.reshape(nseq, seq_len, LANES)[:, seq_len - 1, :A_HEADS]
    x = _matmul_res(hg, w["a_w_out"], x, tm=tm, tn=512)

    def ffn(x, layer):
        args = (w["norm_ffn"][layer:layer + 1], w["f_w_up"][layer], w["f_conv_w"][layer],
                w["f_conv_b"][layer:layer + 1], w["f_w_down"][layer])
        if prompt:
            ftm = 512
            x_new, tg, tv = _ffn(x, *args, tm=ftm, tf=512, seq_len=seq_len)
            tails = jnp.concatenate([tg, tv], axis=-1)
            tps = seq_len // ftm
            tails = tails.reshape(nseq, tps, SUBLANES, 2 * D_FF)[:, tps - 1, SUBLANES - (FFN_CONV - 1):]
        else:
            inj = _conv_inject(state["ffn_conv"][layer], seq_len)
            x_new, ug, uv = _ffn(x, *args, tm=T, tf=512, seq_len=seq_len, inj=inj)
            u = jnp.concatenate([ug, uv], axis=-1).reshape(nseq, seq_len, 2 * D_FF)
            tails = u[:, seq_len - (FFN_CONV - 1):]
        return x_new, tails

    x, f_conv0 = ffn(x, 0)

    p_b, dt_raw = _norm_matmul(x, w["norm_mix"][1:2], w["b_w_main"], w["b_w_dt"], tm=tm, tn=1024)
    xbc = p_b[:, B_DINNER:].reshape(nseq, seq_len, B_CONV_DIM)
    s_conv = xbc[:, seq_len - (B_DCONV - 1):]
    d_lane = jnp.repeat(w["b_D"], B_HEADDIM, axis=1)
    if prompt:
        def per_group(a):
            r = a.shape[0]
            a = a.reshape(r, B_GROUPS, HEADS_PER_GROUP)
            return jnp.pad(a, ((0, 0), (0, 0), (0, LANES - HEADS_PER_GROUP))).reshape(r, B_GROUPS * LANES)
        dt_cols = per_group(dt_raw[:, :B_HEADS])
        dt_rows = dt_raw[:, :B_HEADS].T
        yn, s_t = _ssd_prompt(p_b, dt_cols, dt_rows, w["b_conv_w"], w["b_conv_b"],
                              per_group(w["b_dt_bias"]), w["b_dt_bias"].T,
                              per_group(w["b_A_log"]), w["b_A_log"].T, d_lane, w["b_norm_w"],
                              nseq=nseq, seq_len=seq_len, c=256)
        ssm = s_t.reshape(nseq, B_GROUPS, B_DSTATE, HEADS_PER_GROUP, B_HEADDIM)
        ssm = ssm.transpose(0, 1, 3, 4, 2).reshape(nseq, B_HEADS, B_HEADDIM, B_DSTATE)
    else:
        inj = _conv_inject(state["ssm_conv"], seq_len)
        yn, ssm = _ssd_sample(p_b, dt_raw, inj, w["b_conv_w"], w["b_conv_b"], _pad_lanes(w["b_dt_bias"]),
                              _pad_lanes(w["b_A_log"]), d_lane, w["b_norm_w"], state["ssm"],
                              nbs=4, seq_len=seq_len)
    x = _matmul_res(yn, w["b_w_out"], x, tm=tm, tn=512)

    x, f_conv1 = ffn(x, 1)
    y = _rmsnorm(x, w["norm_final"], tm=tm if prompt else T)
    return (y.reshape(nseq, seq_len, D_MODEL), m_c[None], m_n[None], m_m[None], ssm[None], s_conv[None],
            jnp.stack([f_conv0, f_conv1]))


def kernel(x_prompt, x_sample, state_mlstm_C, state_mlstm_n, state_mlstm_m, state_ssm, state_ssm_conv,
           state_ffn_conv, norm_mix, norm_ffn, norm_final, a_w_in, a_b_if, a_w_out, b_w_in, b_conv_w,
           b_conv_b, b_dt_bias, b_A_log, b_D, b_norm_w, b_w_out, f_w_up, f_conv_w, f_conv_b, f_w_down):
    a_main = 2 * A_HEADS * A_DK + 2 * A_HEADS * A_DV
    b_main = B_DINNER + B_CONV_DIM
    w = {
        "norm_mix": norm_mix, "norm_ffn": norm_ffn, "norm_final": norm_final[None],
        "a_w_main": a_w_in[0, :, :a_main].astype(BF16),
        "a_w_gate": _pad_lanes(a_w_in[0, :, a_main:]).astype(BF16),
        "a_b_if": a_b_if, "a_w_out": a_w_out[0].astype(BF16),
        "b_w_main": b_w_in[0, :, :b_main].astype(BF16),
        "b_w_dt": _pad_lanes(b_w_in[0, :, b_main:]).astype(BF16),
        "b_conv_w": b_conv_w[0], "b_conv_b": b_conv_b, "b_dt_bias": b_dt_bias, "b_A_log": b_A_log,
        "b_D": b_D, "b_norm_w": b_norm_w, "b_w_out": b_w_out[0].astype(BF16),
        "f_w_up": f_w_up.astype(BF16), "f_conv_w": f_conv_w, "f_conv_b": f_conv_b,
        "f_w_down": f_w_down.astype(BF16),
    }
    bp, lp, _ = x_prompt.shape
    bs, ls, _ = x_sample.shape
    yp, pc, pn, pm, pssm, psconv, pfconv = _forward(x_prompt.reshape(bp * lp, D_MODEL), bp, lp, None, w, prompt=True)
    state = {"mlstm": (state_mlstm_C[0], state_mlstm_n[0], state_mlstm_m[0]),
             "ssm": state_ssm[0], "ssm_conv": state_ssm_conv[0], "ffn_conv": state_ffn_conv}
    ys, sc, sn, sm, sssm, ssconv, sfconv = _forward(x_sample.reshape(bs * ls, D_MODEL), bs, ls, state, w,
                                                    prompt=False)
    return (yp, ys, pc, sc, pn, sn, pm, sm, pssm, sssm, psconv, ssconv, pfconv, sfconv)
```

```python
import functools

import jax
import jax.numpy as jnp
from jax import lax
from jax.experimental import pallas as pl
from jax.experimental.pallas import tpu as pltpu

F32 = jnp.float32
BF16 = jnp.bfloat16
SDS = jax.ShapeDtypeStruct

D_MODEL = 2048
A_HEADS = 8
A_DK = 128
A_DV = 256
B_DINNER = 4096
B_HEADDIM = 64
B_HEADS = 64
B_GROUPS = 8
B_DSTATE = 128
B_DCONV = 4
B_CONV_DIM = 6144
D_FF = 5632
FFN_CONV = 3
EPS = 1e-6
NEG_INF = float("-inf")
LOG2E = 1.4426950408889634

LANES = 128
SUBLANES = 8
VMEM_LIMIT_BYTES = 56 * 1024 * 1024

HEADS_PER_GROUP = B_HEADS // B_GROUPS
GROUP_CH = HEADS_PER_GROUP * B_HEADDIM
SEQ_GROUP = SUBLANES


def _params(sem):
    return pltpu.CompilerParams(dimension_semantics=sem, vmem_limit_bytes=VMEM_LIMIT_BYTES)


def _sigmoid(x):
    return 1.0 / (1.0 + jnp.exp(-x))


def _silu(x):
    return x * _sigmoid(x)


def _softplus(x):
    return jnp.maximum(x, 0.0) + jnp.log1p(jnp.exp(-jnp.abs(x)))


def _log_sigmoid(x):
    return -_softplus(-x)


def _rms(x, w):
    ms = jnp.mean(x * x, axis=-1, keepdims=True)
    return x * lax.rsqrt(ms + EPS) * w


def _cumsum_rows(x):
    n = x.shape[0]
    row = lax.broadcasted_iota(jnp.int32, x.shape, 0)
    k = 1
    while k < n:
        x = x + jnp.where(row >= k, pltpu.roll(x, k, 0), 0.0)
        k *= 2
    return x


def _cumsum_lanes(x):
    n = x.shape[1]
    col = lax.broadcasted_iota(jnp.int32, x.shape, 1)
    k = 1
    while k < n:
        x = x + jnp.where(col >= k, pltpu.roll(x, k, 1), 0.0)
        k *= 2
    return x


def _dot(a, b):
    return jnp.dot(a, b, preferred_element_type=F32)


def _dot_nt(a, b):
    return lax.dot_general(a, b, (((1,), (1,)), ((), ())), preferred_element_type=F32)


def _dot_tn(a, b):
    return lax.dot_general(a, b, (((0,), (0,)), ((), ())), preferred_element_type=F32)


def _conv_taps(u, shifted, w_ref, b_ref, ntaps):
    y = u * w_ref[ntaps - 1:ntaps, :]
    for k in range(1, ntaps):
        y = y + shifted(k) * w_ref[ntaps - 1 - k:ntaps - k, :]
    return y + b_ref[...]


def _shift_with_prev(u, prev8, k):
    ext = jnp.concatenate([prev8, u], axis=0)
    return pltpu.roll(ext, k, 0)[SUBLANES:]


def _pad_rows(a, rows):
    if a.shape[0] == rows:
        return a
    return jnp.concatenate([a, jnp.zeros((rows - a.shape[0], a.shape[1]), a.dtype)], axis=0)


class _SampleRows:
    def __init__(self, rows, seq_len):
        self.rows = rows
        self.seq_len = seq_len
        r = lax.broadcasted_iota(jnp.int32, (rows, 1), 0)
        self.token = (r // SEQ_GROUP) % seq_len
        self.seq = r % SEQ_GROUP

    def back(self, a, k):
        if k == 0:
            return a
        return jnp.where(self.token >= k, pltpu.roll(a, SEQ_GROUP * k, 0), 0.0)

    def ahead(self, a, k):
        if k == 0:
            return a
        return jnp.where(self.token + k < self.seq_len, pltpu.roll(a, self.rows - SEQ_GROUP * k, 0), 0.0)

    def conv_shift(self, u, state, ntaps):
        return lambda k: self.back(u, k) + self.ahead(state, ntaps - 1 - k)


def _norm_matmul_kernel(x_ref, nw_ref, w_ref, ws_ref, o_ref, os_ref, hn_ref):
    @pl.when(pl.program_id(1) == 0)
    def _():
        hn = _rms(x_ref[...], nw_ref[...]).astype(BF16)
        hn_ref[...] = hn
        os_ref[...] = _dot(hn, ws_ref[...])

    o_ref[...] = _dot(hn_ref[...], w_ref[...])


def _norm_matmul(x, nw, layer, w, n_main, ws, *, tm, tn):
    T, D = x.shape
    return pl.pallas_call(
        _norm_matmul_kernel,
        grid=(T // tm, n_main // tn),
        in_specs=[pl.BlockSpec((tm, D), lambda i, j: (i, 0)),
                  pl.BlockSpec((None, 1, D), lambda i, j: (layer, 0, 0)),
                  pl.BlockSpec((None, D, tn), lambda i, j: (0, 0, j)),
                  pl.BlockSpec((D, LANES), lambda i, j: (0, 0))],
        out_specs=[pl.BlockSpec((tm, tn), lambda i, j: (i, j)),
                   pl.BlockSpec((tm, LANES), lambda i, j: (i, 0))],
        out_shape=[SDS((T, n_main), F32), SDS((T, LANES), F32)],
        scratch_shapes=[pltpu.VMEM((tm, D), BF16)],
        compiler_params=_params(("arbitrary", "arbitrary")),
        name="norm_matmul",
    )(x, nw, w, ws)


def _matmul_res_kernel(a_ref, w_ref, r_ref, o_ref):
    o_ref[...] = r_ref[...] + _dot(a_ref[...].astype(BF16), w_ref[...])


def _matmul_res(a, w, res, *, tm, tn):
    T, K = a.shape
    N = w.shape[2]
    return pl.pallas_call(
        _matmul_res_kernel,
        grid=(T // tm, N // tn),
        in_specs=[pl.BlockSpec((tm, K), lambda i, j: (i, 0)),
                  pl.BlockSpec((None, K, tn), lambda i, j: (0, 0, j)),
                  pl.BlockSpec((tm, tn), lambda i, j: (i, j))],
        out_specs=pl.BlockSpec((tm, tn), lambda i, j: (i, j)),
        out_shape=SDS((T, N), F32),
        compiler_params=_params(("arbitrary", "arbitrary")),
        name="matmul_res",
    )(a, w, res)


def _ffn_kernel(*refs, tm, tiles_per_seq, seq_len, sample, final_norm):
    refs = list(refs)
    x_ref, nw_ref, wug_ref, wuv_ref, cwg_ref, cwv_ref, cbg_ref, cbv_ref, wd_ref = refs[:9]
    del refs[:9]
    if sample:
        sg_ref, sv_ref = refs[:2]
        del refs[:2]
    if final_norm:
        fw_ref = refs.pop(0)
    out_ref, tg_ref, tv_ref, hn_s, acc_s = refs[:5]
    i = pl.program_id(0)
    j = pl.program_id(1)

    @pl.when(j == 0)
    def _():
        hn_s[...] = _rms(x_ref[...], nw_ref[...]).astype(BF16)
        acc_s[...] = jnp.zeros_like(acc_s)

    hn = hn_s[...]
    ug = _dot(hn, wug_ref[...])
    uv = _dot(hn, wuv_ref[...])

    if sample:
        tg_ref[...] = ug
        tv_ref[...] = uv
        sr = _SampleRows(tm, seq_len)
        yg = _conv_taps(ug, sr.conv_shift(ug, sg_ref[...], FFN_CONV), cwg_ref, cbg_ref, FFN_CONV)
        yv = _conv_taps(uv, sr.conv_shift(uv, sv_ref[...], FFN_CONV), cwv_ref, cbv_ref, FFN_CONV)
    else:
        cg_s, cv_s = refs[5:7]
        tail_g = ug[tm - SUBLANES:, :]
        tail_v = uv[tm - SUBLANES:, :]
        tg_ref[0] = tail_g
        tv_ref[0] = tail_v

        @pl.when(i % tiles_per_seq == 0)
        def _():
            cg_s[j] = jnp.zeros_like(tail_g)
            cv_s[j] = jnp.zeros_like(tail_v)

        prev_g = cg_s[j]
        prev_v = cv_s[j]
        yg = _conv_taps(ug, lambda k: _shift_with_prev(ug, prev_g, k), cwg_ref, cbg_ref, FFN_CONV)
        yv = _conv_taps(uv, lambda k: _shift_with_prev(uv, prev_v, k), cwv_ref, cbv_ref, FFN_CONV)
        cg_s[j] = tail_g
        cv_s[j] = tail_v

    act = (_silu(yg) * yv).astype(BF16)
    acc_s[...] += _dot(act, wd_ref[...])

    @pl.when(j == pl.num_programs(1) - 1)
    def _():
        y = x_ref[...] + acc_s[...]
        out_ref[...] = _rms(y, fw_ref[...]) if final_norm else y


def _ffn(x, layer, w, *, tm, tf, seq_len, state=None, final_w=None):
    T, D = x.shape
    nj = D_FF // tf
    ni = T // tm
    sample = state is not None
    tiles_per_seq = max(seq_len // tm, 1)
    in_specs = [pl.BlockSpec((tm, D), lambda i, j: (i, 0)),
                pl.BlockSpec((None, 1, D), lambda i, j: (layer, 0, 0)),
                pl.BlockSpec((None, D, tf), lambda i, j: (layer, 0, j)),
                pl.BlockSpec((None, D, tf), lambda i, j: (layer, 0, nj + j)),
                pl.BlockSpec((None, FFN_CONV, tf), lambda i, j: (layer, 0, j)),
                pl.BlockSpec((None, FFN_CONV, tf), lambda i, j: (layer, 0, nj + j)),
                pl.BlockSpec((None, 1, tf), lambda i, j: (layer, 0, j)),
                pl.BlockSpec((None, 1, tf), lambda i, j: (layer, 0, nj + j)),
                pl.BlockSpec((None, tf, D), lambda i, j: (layer, j, 0))]
    args = [x, w["norm_ffn"], w["f_w_up"], w["f_w_up"], w["f_conv_w"], w["f_conv_w"], w["f_conv_b"], w["f_conv_b"],
            w["f_w_down"]]
    scratch = [pltpu.VMEM((tm, D), BF16), pltpu.VMEM((tm, D), F32)]
    if sample:
        in_specs += [pl.BlockSpec((tm, tf), lambda i, j: (i, j)),
                     pl.BlockSpec((tm, tf), lambda i, j: (i, nj + j))]
        args += [state, state]
        tail_specs = [pl.BlockSpec((tm, tf), lambda i, j: (i, j))] * 2
        tail_shapes = [SDS((T, D_FF), F32)] * 2
    else:
        scratch += [pltpu.VMEM((nj, SUBLANES, tf), F32)] * 2
        tail_specs = [pl.BlockSpec((1, SUBLANES, tf), lambda i, j: (i, 0, j))] * 2
        tail_shapes = [SDS((ni, SUBLANES, D_FF), F32)] * 2
    if final_w is not None:
        in_specs.append(pl.BlockSpec((1, D), lambda i, j: (0, 0)))
        args.append(final_w)
    return pl.pallas_call(
        functools.partial(_ffn_kernel, tm=tm, tiles_per_seq=tiles_per_seq, seq_len=seq_len, sample=sample,
                          final_norm=final_w is not None),
        grid=(ni, nj),
        in_specs=in_specs,
        out_specs=[pl.BlockSpec((tm, D), lambda i, j: (i, 0))] + tail_specs,
        out_shape=[SDS((T, D), F32)] + tail_shapes,
        scratch_shapes=scratch,
        compiler_params=_params(("arbitrary", "arbitrary")),
        name="ffn_sample" if sample else "ffn_prompt",
    )(*args)


def _mlstm_intra(q, ks, v, ig_col, lf_cum_col, lf_cum_row, ig_row, m_prev_col, causal):
    log_d = jnp.where(causal, lf_cum_col - lf_cum_row + ig_row, NEG_INF)
    inter = lf_cum_col + m_prev_col
    m_t = jnp.maximum(inter, jnp.max(log_d, axis=1, keepdims=True))
    d_m = jnp.exp(log_d - m_t)
    w_inter = jnp.exp(inter - m_t)
    kb = _pad_rows(ks.astype(BF16), causal.shape[1])
    vb = _pad_rows(v.astype(BF16), causal.shape[1])
    s = _dot_nt(q.astype(BF16), kb) * d_m
    num = _dot(s.astype(BF16), vb)
    den = jnp.sum(s, axis=1, keepdims=True)
    return m_t, w_inter, inter, num, den


def _mlstm_prompt_kernel(q_ref, k_ref, v_ref, o_ref, g_ref, gt_ref, br_ref, bc_ref,
                         hg_ref, co_ref, no_ref, mo_ref, c_s, n_s, m_s, *, c):
    ci = pl.program_id(1)

    @pl.when(ci == 0)
    def _():
        c_s[...] = jnp.zeros_like(c_s)
        n_s[...] = jnp.zeros_like(n_s)
        m_s[...] = jnp.zeros_like(m_s)

    g = g_ref[...] + br_ref[...]
    lf_cum = _cumsum_rows(_log_sigmoid(g))
    gt = gt_ref[...] + bc_ref[...]
    lf_cum_t = _cumsum_lanes(_log_sigmoid(gt))
    row = lax.broadcasted_iota(jnp.int32, (c, c), 0)
    col = lax.broadcasted_iota(jnp.int32, (c, c), 1)
    causal = col <= row
    last = ci == pl.num_programs(1) - 1

    for h in range(A_HEADS):
        ig_col = g[:, h:h + 1]
        b_col = lf_cum[:, A_HEADS + h:A_HEADS + h + 1]
        ig_row = gt[h:h + 1, :]
        b_row = lf_cum_t[A_HEADS + h:A_HEADS + h + 1, :]
        q = q_ref[:, h * A_DK:(h + 1) * A_DK]
        ks = k_ref[:, h * A_DK:(h + 1) * A_DK] * (A_DK ** -0.5)
        v = v_ref[:, h * A_DV:(h + 1) * A_DV]
        m_prev = m_s[h][:, 0:1]
        m_t, w_inter, inter, num, den = _mlstm_intra(q, ks, v, ig_col, b_col, b_row, ig_row, m_prev, causal)
        cmat = c_s[h]
        nvec = n_s[h]
        num = w_inter * _dot(q.astype(BF16), cmat.astype(BF16)) + num
        den = w_inter * jnp.sum(q * nvec, axis=1, keepdims=True) + den
        hh = num / jnp.maximum(jnp.abs(den), jnp.exp(-m_t))
        hg_ref[:, h * A_DV:(h + 1) * A_DV] = (_sigmoid(o_ref[:, h * A_DV:(h + 1) * A_DV]) * hh).astype(BF16)

        m_new = m_t[c - 1:c, :]
        w_state = jnp.exp(b_col[c - 1:c, :] - b_col + ig_col - m_new)
        decay = jnp.exp(inter[c - 1:c, :] - m_new)
        kw = ks * w_state
        c_new = decay * cmat + _dot_tn(kw.astype(BF16), v.astype(BF16))
        n_new = decay * nvec + jnp.sum(kw, axis=0, keepdims=True)
        m_row = jnp.broadcast_to(m_new, (1, LANES))
        c_s[h] = c_new
        n_s[h] = n_new
        m_s[h] = m_row

        @pl.when(last)
        def _():
            co_ref[0, h] = c_new
            no_ref[0, h] = n_new
            mo_ref[0, h] = m_row


def _mlstm_prompt(p, gates, gates_t, bias_row, bias_col, *, nseq, seq_len, c):
    T = p.shape[0]
    nc = seq_len // c
    hk = A_HEADS * A_DK
    hv = A_HEADS * A_DV
    rows = lambda b, ci: b * nc + ci
    return pl.pallas_call(
        functools.partial(_mlstm_prompt_kernel, c=c),
        grid=(nseq, nc),
        in_specs=[pl.BlockSpec((c, hk), lambda b, ci: (rows(b, ci), 0)),
                  pl.BlockSpec((c, hk), lambda b, ci: (rows(b, ci), 1)),
                  pl.BlockSpec((c, hv), lambda b, ci: (rows(b, ci), 1)),
                  pl.BlockSpec((c, hv), lambda b, ci: (rows(b, ci), 2)),
                  pl.BlockSpec((c, LANES), lambda b, ci: (rows(b, ci), 0)),
                  pl.BlockSpec((2 * A_HEADS, c), lambda b, ci: (0, rows(b, ci))),
                  pl.BlockSpec((1, LANES), lambda b, ci: (0, 0)),
                  pl.BlockSpec((2 * A_HEADS, 1), lambda b, ci: (0, 0))],
        out_specs=[pl.BlockSpec((c, hv), lambda b, ci: (rows(b, ci), 0)),
                   pl.BlockSpec((1, A_HEADS, A_DK, A_DV), lambda b, ci: (b, 0, 0, 0)),
                   pl.BlockSpec((1, A_HEADS, 1, A_DK), lambda b, ci: (b, 0, 0, 0)),
                   pl.BlockSpec((1, A_HEADS, 1, LANES), lambda b, ci: (b, 0, 0, 0))],
        out_shape=[SDS((T, hv), BF16),
                   SDS((nseq, A_HEADS, A_DK, A_DV), F32),
                   SDS((nseq, A_HEADS, 1, A_DK), F32),
                   SDS((nseq, A_HEADS, 1, LANES), F32)],
        scratch_shapes=[pltpu.VMEM((A_HEADS, A_DK, A_DV), F32), pltpu.VMEM((A_HEADS, 1, A_DK), F32),
                        pltpu.VMEM((A_HEADS, 1, LANES), F32)],
        compiler_params=_params(("arbitrary", "arbitrary")),
        name="mlstm_prompt",
    )(p, p, p, p, gates, gates_t, bias_row, bias_col)


def _mlstm_sample_kernel(q_ref, k_ref, v_ref, o_ref, g_ref, br_ref, mp_ref, ci_ref, ni_ref,
                         hg_ref, co_ref, no_ref, mo_ref, *, seq_len):
    nbs = SEQ_GROUP
    rs = nbs * seq_len
    g = g_ref[...] + br_ref[...]
    lf_all = _log_sigmoid(g)
    m_prev_all = mp_ref[...]
    row = lax.broadcasted_iota(jnp.int32, (rs, LANES), 0)
    col = lax.broadcasted_iota(jnp.int32, (rs, LANES), 1)
    same = ((row % nbs) == (col % nbs)) & (col < rs)
    causal = same & (col <= row)
    causal_t = same & (row <= col)
    eye = row == col
    last = same & (col // nbs == seq_len - 1)
    seq_of_row = lax.broadcasted_iota(jnp.int32, (rs, 1), 0) % nbs
    prow = lax.broadcasted_iota(jnp.int32, (LANES, 1), 0)
    seq_of_prow = jnp.where(prow < rs, prow % nbs, -1)

    def to_row(colvec):
        return jnp.sum(jnp.where(eye, colvec, 0.0), axis=0, keepdims=True)

    def at_last(rowvec):
        return jnp.sum(jnp.where(last, rowvec, 0.0), axis=1, keepdims=True)

    m_out = jnp.zeros((rs, LANES), F32)
    for h in range(A_HEADS):
        ig_col = g[:, h:h + 1]
        lf_col = lf_all[:, A_HEADS + h:A_HEADS + h + 1]
        m_prev = m_prev_all[:, h:h + 1]
        ig_row = to_row(ig_col)
        lf_row = to_row(lf_col)
        b_col = jnp.sum(jnp.where(causal, lf_row, 0.0), axis=1, keepdims=True)
        b_row = jnp.sum(jnp.where(causal_t, lf_col, 0.0), axis=0, keepdims=True)
        q = q_ref[:, h * A_DK:(h + 1) * A_DK]
        ks = k_ref[:, h * A_DK:(h + 1) * A_DK] * (A_DK ** -0.5)
        v = v_ref[:, h * A_DV:(h + 1) * A_DV]
        m_t, w_inter, inter, num, den = _mlstm_intra(q, ks, v, ig_col, b_col, b_row, ig_row, m_prev, causal)
        qb = q.astype(BF16)
        num_i = jnp.zeros((rs, A_DV), F32)
        qn = jnp.zeros((rs, 1), F32)
        for bb in range(nbs):
            mine = seq_of_row == bb
            num_i = jnp.where(mine, _dot(qb, ci_ref[bb, h].astype(BF16)), num_i)
            qn = jnp.where(mine, jnp.sum(q * ni_ref[bb, h], axis=1, keepdims=True), qn)
        num = w_inter * num_i + num
        den = w_inter * qn + den
        hh = num / jnp.maximum(jnp.abs(den), jnp.exp(-m_t))
        hg_ref[:, h * A_DV:(h + 1) * A_DV] = _sigmoid(o_ref[:, h * A_DV:(h + 1) * A_DV]) * hh

        m_new = at_last(to_row(m_t))
        w_state = jnp.exp(at_last(b_row) - b_col + ig_col - m_new)
        decay = jnp.exp(at_last(to_row(inter)) - m_new)
        kw = ks * w_state
        kw_t = _pad_rows(kw, LANES).T.astype(BF16)
        v_pad = _pad_rows(v, LANES)
        for bb in range(nbs):
            v_mine = jnp.where(seq_of_prow == bb, v_pad, 0.0).astype(BF16)
            dec = decay[bb:bb + 1, :]
            co_ref[bb, h] = dec * ci_ref[bb, h] + _dot(kw_t, v_mine)
            no_ref[bb, h] = dec * ni_ref[bb, h] + jnp.sum(jnp.where(seq_of_row == bb, kw, 0.0), axis=0, keepdims=True)
        m_out = jnp.where(col == h, m_new, m_out)
    mo_ref[...] = m_out


def _mlstm_sample(p, gates, bias_row, m_rows, c_in, n_in, *, seq_len):
    T = p.shape[0]
    nbs = SEQ_GROUP
    rs = nbs * seq_len
    nb = c_in.shape[0]
    hk = A_HEADS * A_DK
    hv = A_HEADS * A_DV
    return pl.pallas_call(
        functools.partial(_mlstm_sample_kernel, seq_len=seq_len),
        grid=(nb // nbs,),
        in_specs=[pl.BlockSpec((rs, hk), lambda i: (i, 0)),
                  pl.BlockSpec((rs, hk), lambda i: (i, 1)),
                  pl.BlockSpec((rs, hv), lambda i: (i, 1)),
                  pl.BlockSpec((rs, hv), lambda i: (i, 2)),
                  pl.BlockSpec((rs, LANES), lambda i: (i, 0)),
                  pl.BlockSpec((1, LANES), lambda i: (0, 0)),
                  pl.BlockSpec((rs, LANES), lambda i: (i, 0)),
                  pl.BlockSpec((nbs, A_HEADS, A_DK, A_DV), lambda i: (i, 0, 0, 0)),
                  pl.BlockSpec((nbs, A_HEADS, 1, A_DK), lambda i: (i, 0, 0, 0))],
        out_specs=[pl.BlockSpec((rs, hv), lambda i: (i, 0)),
                   pl.BlockSpec((nbs, A_HEADS, A_DK, A_DV), lambda i: (i, 0, 0, 0)),
                   pl.BlockSpec((nbs, A_HEADS, 1, A_DK), lambda i: (i, 0, 0, 0)),
                   pl.BlockSpec((rs, LANES), lambda i: (i, 0))],
        out_shape=[SDS((T, hv), F32),
                   SDS((nb, A_HEADS, A_DK, A_DV), F32),
                   SDS((nb, A_HEADS, 1, A_DK), F32),
                   SDS((T, LANES), F32)],
        compiler_params=_params(("arbitrary",)),
        name="mlstm_sample",
    )(p, p, p, p, gates, bias_row, m_rows, c_in, n_in)


def _pair_select(a, h0):
    lane = lax.broadcasted_iota(jnp.int32, (a.shape[0], LANES), 1)
    return jnp.where(lane < B_HEADDIM, a[:, h0:h0 + 1], a[:, h0 + 1:h0 + 2])


def _group_norm_gate(y, z, nw):
    gz = y * _silu(z)
    return gz * lax.rsqrt(jnp.mean(gz * gz, axis=-1, keepdims=True) + EPS) * nw


def _ssd_prompt_kernel(x_ref, b_ref, c_ref, z_ref, dtc_ref, dtr_ref,
                       cwx_ref, cwb_ref, cwc_ref, cbx_ref, cbb_ref, cbc_ref,
                       dbr_ref, dbc_ref, alr_ref, alc_ref, d_ref, nw_ref,
                       y_ref, so_ref, s_s, xp_s, bp_s, cp_s, *, c):
    ci = pl.program_id(2)

    @pl.when(ci == 0)
    def _():
        s_s[...] = jnp.zeros_like(s_s)
        xp_s[...] = jnp.zeros_like(xp_s)
        bp_s[...] = jnp.zeros_like(bp_s)
        cp_s[...] = jnp.zeros_like(cp_s)

    def conv(raw_ref, prev_s, cw_ref, cb_ref):
        raw = raw_ref[...]
        prev = prev_s[...]
        out = _conv_taps(raw, lambda k: _shift_with_prev(raw, prev, k), cw_ref, cb_ref, B_DCONV)
        prev_s[...] = raw[c - SUBLANES:, :]
        return _silu(out)

    x = conv(x_ref, xp_s, cwx_ref, cbx_ref)
    bm = conv(b_ref, bp_s, cwb_ref, cbb_ref)
    cm = conv(c_ref, cp_s, cwc_ref, cbc_ref)

    dt_c = _softplus(dtc_ref[...] + dbr_ref[...])
    ac_c = _cumsum_rows(dt_c * (-jnp.exp(alr_ref[...])))
    dt_r = _softplus(dtr_ref[...] + dbc_ref[...])
    ac_r = _cumsum_lanes(dt_r * (-jnp.exp(alc_ref[...])))
    ac_c2 = ac_c * LOG2E
    ac_r2 = ac_r * LOG2E

    row = lax.broadcasted_iota(jnp.int32, (c, c), 0)
    col = lax.broadcasted_iota(jnp.int32, (c, c), 1)
    bmb = bm.astype(BF16)
    cmb = cm.astype(BF16)
    cb = jnp.where(col <= row, _dot_nt(cmb, bmb), 0.0)
    state = s_s[...]
    y_inter = _dot(cmb, state.astype(BF16))
    ac_last = ac_c[c - 1:c, :]
    to_end = jnp.exp(ac_last - ac_c) * dt_c
    e_ac = jnp.exp(ac_c)
    e_last = jnp.exp(ac_last)
    lane = lax.broadcasted_iota(jnp.int32, (c, LANES), 1)
    d_all = d_ref[...]

    ys = []
    for pr in range(HEADS_PER_GROUP // 2):
        h0 = 2 * pr
        sl = slice(pr * LANES, (pr + 1) * LANES)
        xp = x[:, sl]
        xpb = xp.astype(BF16)
        full = []
        for r in (h0, h0 + 1):
            decay = jnp.exp2(jnp.minimum(ac_c2[:, r:r + 1] - ac_r2[r:r + 1, :], 0.0))
            wts = cb * decay * dt_r[r:r + 1, :]
            full.append(_dot(wts.astype(BF16), xpb))
        yp = jnp.where(lane < B_HEADDIM, full[0], full[1])
        yp = yp + y_inter[:, sl] * _pair_select(e_ac, h0) + d_all[:, sl] * xp
        ys.append(yp)
        xw = (xp * _pair_select(to_end, h0)).astype(BF16)
        s_s[:, sl] = _pair_select(e_last, h0) * state[:, sl] + _dot_tn(bmb, xw)
    y = jnp.concatenate(ys, axis=1)
    y_ref[...] = _group_norm_gate(y, z_ref[...], nw_ref[...]).astype(BF16)

    @pl.when(ci == pl.num_programs(2) - 1)
    def _():
        so_ref[0, 0] = s_s[...]


def _ssd_prompt(p, dt_cols, dt_rows, w, *, nseq, seq_len, c):
    T = p.shape[0]
    nc = seq_len // c
    rows = lambda b, g, ci: b * nc + ci
    kx = B_DINNER // GROUP_CH
    kb = (2 * B_DINNER) // B_DSTATE
    kc = kb + B_GROUPS
    wb = B_DINNER // B_DSTATE
    wc = wb + B_GROUPS
    G = B_GROUPS
    per_lane = lambda width, start=0: pl.BlockSpec((1, width), lambda b, g, ci: (0, start + g))
    per_row = pl.BlockSpec((HEADS_PER_GROUP, 1), lambda b, g, ci: (g, 0))
    return pl.pallas_call(
        functools.partial(_ssd_prompt_kernel, c=c),
        grid=(nseq, G, nc),
        in_specs=[pl.BlockSpec((c, GROUP_CH), lambda b, g, ci: (rows(b, g, ci), kx + g)),
                  pl.BlockSpec((c, B_DSTATE), lambda b, g, ci: (rows(b, g, ci), kb + g)),
                  pl.BlockSpec((c, B_DSTATE), lambda b, g, ci: (rows(b, g, ci), kc + g)),
                  pl.BlockSpec((c, GROUP_CH), lambda b, g, ci: (rows(b, g, ci), g)),
                  pl.BlockSpec((c, LANES), lambda b, g, ci: (rows(b, g, ci), g)),
                  pl.BlockSpec((HEADS_PER_GROUP, c), lambda b, g, ci: (g, rows(b, g, ci))),
                  pl.BlockSpec((None, B_DCONV, GROUP_CH), lambda b, g, ci: (0, 0, g)),
                  pl.BlockSpec((None, B_DCONV, B_DSTATE), lambda b, g, ci: (0, 0, wb + g)),
                  pl.BlockSpec((None, B_DCONV, B_DSTATE), lambda b, g, ci: (0, 0, wc + g)),
                  per_lane(GROUP_CH), per_lane(B_DSTATE, wb), per_lane(B_DSTATE, wc),
                  per_lane(LANES), per_row, per_lane(LANES), per_row,
                  per_lane(GROUP_CH), per_lane(GROUP_CH)],
        out_specs=[pl.BlockSpec((c, GROUP_CH), lambda b, g, ci: (rows(b, g, ci), g)),
                   pl.BlockSpec((1, 1, B_DSTATE, GROUP_CH), lambda b, g, ci: (b, g, 0, 0))],
        out_shape=[SDS((T, B_DINNER), BF16), SDS((nseq, G, B_DSTATE, GROUP_CH), F32)],
        scratch_shapes=[pltpu.VMEM((B_DSTATE, GROUP_CH), F32),
                        pltpu.VMEM((SUBLANES, GROUP_CH), F32),
                        pltpu.VMEM((SUBLANES, B_DSTATE), F32),
                        pltpu.VMEM((SUBLANES, B_DSTATE), F32)],
        compiler_params=_params(("arbitrary", "arbitrary", "arbitrary")),
        name="ssd_prompt",
    )(p, p, p, p, dt_cols, dt_rows, w["b_conv_w"], w["b_conv_w"], w["b_conv_w"],
      w["b_conv_b"], w["b_conv_b"], w["b_conv_b"],
      w["dtb_cols"], w["dtb_rows"], w["alog_cols"], w["alog_rows"], w["d_lane"], w["b_norm_w"])


def _ssd_sample_kernel(x_ref, b_ref, c_ref, z_ref, dt_ref, sx_ref, sb_ref, sc_ref,
                       cwx_ref, cwb_ref, cwc_ref, cbx_ref, cbb_ref, cbc_ref,
                       dtb_ref, al_ref, d_ref, nw_ref, hs_ref,
                       y_ref, hso_ref, *, seq_len):
    nbs = SEQ_GROUP
    rs = nbs * seq_len
    sr = _SampleRows(rs, seq_len)
    prow = lax.broadcasted_iota(jnp.int32, (LANES, 1), 0)
    seq_of_prow = jnp.where(prow < rs, prow % nbs, -1)

    def conv(raw_ref, st_ref, cw_ref, cb_ref):
        raw = raw_ref[...]
        return _silu(_conv_taps(raw, sr.conv_shift(raw, st_ref[...], B_DCONV), cw_ref, cb_ref, B_DCONV))

    x = conv(x_ref, sx_ref, cwx_ref, cbx_ref)
    bm = conv(b_ref, sb_ref, cwb_ref, cbb_ref)
    cm = conv(c_ref, sc_ref, cwc_ref, cbc_ref)

    dt = _softplus(dt_ref[...] + dtb_ref[...])
    a = dt * (-jnp.exp(al_ref[...]))
    acum = a
    total = a
    for k in range(1, seq_len):
        acum = acum + sr.back(a, k)
        total = total + sr.back(a, k) + sr.ahead(a, k)

    coefs = []
    for d in range(seq_len):
        cb_d = jnp.sum(cm * sr.back(bm, d), axis=1, keepdims=True)
        coefs.append(cb_d * jnp.exp(acum - sr.back(acum, d)) * sr.back(dt, d))
    x_back = [sr.back(x, d) for d in range(seq_len)]

    e_ac = jnp.exp(acum)
    e_tot = jnp.exp(total)
    to_end = jnp.exp(total - acum) * dt
    d_all = d_ref[...]

    ys = []
    xws = []
    for pr in range(HEADS_PER_GROUP // 2):
        h0 = 2 * pr
        sl = slice(pr * LANES, (pr + 1) * LANES)
        yp = d_all[:, sl] * x[:, sl]
        for d in range(seq_len):
            yp = yp + _pair_select(coefs[d], h0) * x_back[d][:, sl]
        ys.append(yp)
        xws.append(x[:, sl] * _pair_select(to_end, h0))
    xw_t = _pad_rows(jnp.concatenate(xws, axis=1), LANES).T.astype(BF16)
    cmb = cm.astype(BF16)
    bm_pad = _pad_rows(bm, LANES)
    y_int = jnp.zeros((rs, GROUP_CH), F32)
    for bb in range(nbs):
        hs_g = hs_ref[bb].reshape(GROUP_CH, B_DSTATE)
        y_int = jnp.where(sr.seq == bb, _dot_nt(cmb, hs_g.astype(BF16)), y_int)
        b_mine = jnp.where(seq_of_prow == bb, bm_pad, 0.0).astype(BF16)
        upd = _dot(xw_t, b_mine)
        for r in range(HEADS_PER_GROUP):
            dec = e_tot[bb:bb + 1, r:r + 1]
            hso_ref[bb, r] = dec * hs_ref[bb, r] + upd[r * B_HEADDIM:(r + 1) * B_HEADDIM, :]
    e_pairs = jnp.concatenate([_pair_select(e_ac, 2 * pr) for pr in range(HEADS_PER_GROUP // 2)], axis=1)
    y = jnp.concatenate(ys, axis=1) + y_int * e_pairs
    y_ref[...] = _group_norm_gate(y, z_ref[...], nw_ref[...])


def _ssd_sample(p, dt_cols, conv_state, w, hs, *, seq_len):
    T = p.shape[0]
    nbs = SEQ_GROUP
    rs = nbs * seq_len
    nb = hs.shape[0]
    G = B_GROUPS
    kx = B_DINNER // GROUP_CH
    kb = (2 * B_DINNER) // B_DSTATE
    kc = kb + G
    wb = B_DINNER // B_DSTATE
    wc = wb + G
    per_lane = lambda width, start=0: pl.BlockSpec((1, width), lambda i, g: (0, start + g))
    hs_spec = pl.BlockSpec((nbs, HEADS_PER_GROUP, B_HEADDIM, B_DSTATE), lambda i, g: (i, g, 0, 0))
    return pl.pallas_call(
        functools.partial(_ssd_sample_kernel, seq_len=seq_len),
        grid=(nb // nbs, G),
        in_specs=[pl.BlockSpec((rs, GROUP_CH), lambda i, g: (i, kx + g)),
                  pl.BlockSpec((rs, B_DSTATE), lambda i, g: (i, kb + g)),
                  pl.BlockSpec((rs, B_DSTATE), lambda i, g: (i, kc + g)),
                  pl.BlockSpec((rs, GROUP_CH), lambda i, g: (i, g)),
                  pl.BlockSpec((rs, LANES), lambda i, g: (i, g)),
                  pl.BlockSpec((rs, GROUP_CH), lambda i, g: (i, g)),
                  pl.BlockSpec((rs, B_DSTATE), lambda i, g: (i, wb + g)),
                  pl.BlockSpec((rs, B_DSTATE), lambda i, g: (i, wc + g)),
                  pl.BlockSpec((None, B_DCONV, GROUP_CH), lambda i, g: (0, 0, g)),
                  pl.BlockSpec((None, B_DCONV, B_DSTATE), lambda i, g: (0, 0, wb + g)),
                  pl.BlockSpec((None, B_DCONV, B_DSTATE), lambda i, g: (0, 0, wc + g)),
                  per_lane(GROUP_CH), per_lane(B_DSTATE, wb), per_lane(B_DSTATE, wc),
                  per_lane(LANES), per_lane(LANES), per_lane(GROUP_CH), per_lane(GROUP_CH),
                  hs_spec],
        out_specs=[pl.BlockSpec((rs, GROUP_CH), lambda i, g: (i, g)), hs_spec],
        out_shape=[SDS((T, B_DINNER), F32), SDS(hs.shape, F32)],
        compiler_params=_params(("arbitrary", "arbitrary")),
        name="ssd_sample",
    )(p, p, p, p, dt_cols, conv_state, conv_state, conv_state,
      w["b_conv_w"], w["b_conv_w"], w["b_conv_w"], w["b_conv_b"], w["b_conv_b"], w["b_conv_b"],
      w["dtb_cols"], w["alog_cols"], w["d_lane"], w["b_norm_w"], hs)


def _pad_lanes(a, width=LANES):
    return jnp.pad(a, ((0, 0), (0, width - a.shape[1])))


def _per_group_lanes(a):
    r = a.shape[0]
    a = a.reshape(r, B_GROUPS, HEADS_PER_GROUP)
    return jnp.pad(a, ((0, 0), (0, 0), (0, LANES - HEADS_PER_GROUP))).reshape(r, B_GROUPS * LANES)


def _to_sample_rows(a, seq_len):
    nseq, nt, ch = a.shape
    a = a.reshape(nseq // SEQ_GROUP, SEQ_GROUP, nt, ch).transpose(0, 2, 1, 3)
    a = jnp.pad(a, ((0, 0), (0, seq_len - nt), (0, 0), (0, 0)))
    return a.reshape(nseq * seq_len, ch)


def _from_sample_rows(a, nseq, seq_len, first_token=0):
    ch = a.shape[1]
    a = a.reshape(nseq // SEQ_GROUP, seq_len, SEQ_GROUP, ch)[:, first_token:]
    return a.transpose(0, 2, 1, 3).reshape(nseq, seq_len - first_token, ch)


def _forward(x, nseq, seq_len, state, w, *, prompt):
    T = x.shape[0]
    tm = 1024 if prompt else T
    a_main = 2 * A_HEADS * A_DK + 2 * A_HEADS * A_DV
    b_main = B_DINNER + B_CONV_DIM

    def tail_tokens(rows2d, ntok):
        if prompt:
            return rows2d.reshape(nseq, seq_len, rows2d.shape[1])[:, seq_len - ntok:]
        return _from_sample_rows(rows2d, nseq, seq_len, seq_len - ntok)

    p_a, gates = _norm_matmul(x, w["norm_mix"], 0, w["a_w_in"], a_main, w["a_w_gate"], tm=tm, tn=1024)
    if prompt:
        gates_t = gates[:, :2 * A_HEADS].T
        hg, m_c, m_n, m_m = _mlstm_prompt(p_a, gates, gates_t, w["a_b_row"], w["a_b_col"],
                                          nseq=nseq, seq_len=seq_len, c=256)
        m_m = m_m[:, :, 0, 0]
    else:
        c0, n0, m0 = state["mlstm"]
        m_rows = _pad_lanes(_to_sample_rows(jnp.broadcast_to(m0[:, None, :], (nseq, seq_len, A_HEADS)), seq_len))
        hg, m_c, m_n, m_rows_new = _mlstm_sample(p_a, gates, w["a_b_row"], m_rows, c0, n0[:, :, None, :],
                                                 seq_len=seq_len)
        m_m = _from_sample_rows(m_rows_new, nseq, seq_len, seq_len - 1)[:, 0, :A_HEADS]
    m_n = m_n[:, :, 0, :]
    x = _matmul_res(hg, w["a_w_out"], x, tm=tm, tn=512)

    def ffn(x, layer, final_w=None):
        if prompt:
            ftm = 512
            x_new, tg, tv = _ffn(x, layer, w, tm=ftm, tf=512, seq_len=seq_len, final_w=final_w)
            tps = seq_len // ftm
            pick = lambda t: t.reshape(nseq, tps, SUBLANES, D_FF)[:, tps - 1, SUBLANES - (FFN_CONV - 1):]
            tails = jnp.concatenate([pick(tg), pick(tv)], axis=-1)
        else:
            st = _to_sample_rows(state["ffn_conv"][layer], seq_len)
            x_new, ug, uv = _ffn(x, layer, w, tm=T, tf=512, seq_len=seq_len, state=st, final_w=final_w)
            tails = jnp.concatenate([tail_tokens(ug, FFN_CONV - 1), tail_tokens(uv, FFN_CONV - 1)], axis=-1)
        return x_new, tails

    x, f_conv0 = ffn(x, 0)

    p_b, dt_raw = _norm_matmul(x, w["norm_mix"], 1, w["b_w_in"], b_main, w["b_w_dt"], tm=tm, tn=1024)
    if prompt:
        s_conv = p_b.reshape(nseq, seq_len, b_main)[:, seq_len - (B_DCONV - 1):, B_DINNER:]
    else:
        s_conv = _from_sample_rows(p_b[:, B_DINNER:], nseq, seq_len, seq_len - (B_DCONV - 1))
    dt_cols = _per_group_lanes(dt_raw[:, :B_HEADS])
    if prompt:
        dt_rows = dt_raw[:, :B_HEADS].T
        yn, s_t = _ssd_prompt(p_b, dt_cols, dt_rows, w, nseq=nseq, seq_len=seq_len, c=256)
        ssm = s_t.reshape(nseq, B_GROUPS, B_DSTATE, HEADS_PER_GROUP, B_HEADDIM)
        ssm = ssm.transpose(0, 1, 3, 4, 2).reshape(nseq, B_HEADS, B_HEADDIM, B_DSTATE)
    else:
        st = _to_sample_rows(state["ssm_conv"], seq_len)
        yn, ssm = _ssd_sample(p_b, dt_cols, st, w, state["ssm"], seq_len=seq_len)
    x = _matmul_res(yn, w["b_w_out"], x, tm=tm, tn=512)

    y, f_conv1 = ffn(x, 1, final_w=w["norm_final"])
    return (y, m_c[None], m_n[None], m_m[None], ssm[None], s_conv[None], jnp.stack([f_conv0, f_conv1]))


def kernel(x_prompt, x_sample, state_mlstm_C, state_mlstm_n, state_mlstm_m, state_ssm, state_ssm_conv,
           state_ffn_conv, norm_mix, norm_ffn, norm_final, a_w_in, a_b_if, a_w_out, b_w_in, b_conv_w,
           b_conv_b, b_dt_bias, b_A_log, b_D, b_norm_w, b_w_out, f_w_up, f_conv_w, f_conv_b, f_w_down):
    a_main = 2 * A_HEADS * A_DK + 2 * A_HEADS * A_DV
    b_main = B_DINNER + B_CONV_DIM
    w = {
        "norm_mix": norm_mix[:, None], "norm_ffn": norm_ffn[:, None], "norm_final": norm_final[None],
        "a_w_in": a_w_in.astype(BF16),
        "a_w_gate": _pad_lanes(a_w_in[0, :, a_main:]).astype(BF16),
        "a_b_row": _pad_lanes(a_b_if), "a_b_col": a_b_if.T,
        "a_w_out": a_w_out.astype(BF16),
        "b_w_in": b_w_in.astype(BF16),
        "b_w_dt": _pad_lanes(b_w_in[0, :, b_main:]).astype(BF16),
        "b_conv_w": b_conv_w, "b_conv_b": b_conv_b,
        "dtb_cols": _per_group_lanes(b_dt_bias), "dtb_rows": b_dt_bias.T,
        "alog_cols": _per_group_lanes(b_A_log), "alog_rows": b_A_log.T,
        "d_lane": jnp.repeat(b_D, B_HEADDIM, axis=1), "b_norm_w": b_norm_w,
        "b_w_out": b_w_out.astype(BF16),
        "f_w_up": f_w_up.astype(BF16), "f_conv_w": f_conv_w, "f_conv_b": f_conv_b[:, None],
        "f_w_down": f_w_down.astype(BF16),
    }
    bp, lp, _ = x_prompt.shape
    bs, ls, _ = x_sample.shape
    yp, pc, pn, pm, pssm, psconv, pfconv = _forward(x_prompt.reshape(bp * lp, D_MODEL), bp, lp, None, w, prompt=True)
    state = {"mlstm": (state_mlstm_C[0], state_mlstm_n[0], state_mlstm_m[0]),
             "ssm": state_ssm[0], "ssm_conv": state_ssm_conv[0], "ffn_conv": state_ffn_conv}
    ys, sc, sn, sm, sssm, ssconv, sfconv = _forward(_to_sample_rows(x_sample, ls), bs, ls, state, w, prompt=False)
    return (yp.reshape(bp, lp, D_MODEL), _from_sample_rows(ys, bs, ls), pc, sc, pn, sn, pm, sm,
            pssm, sssm, psconv, ssconv, pfconv, sfconv)
```

```python
import functools

import jax
import jax.numpy as jnp
from jax import lax
from jax.experimental import pallas as pl
from jax.experimental.pallas import tpu as pltpu

F32 = jnp.float32
BF16 = jnp.bfloat16
SDS = jax.ShapeDtypeStruct

D_MODEL = 2048
A_HEADS = 8
A_DK = 128
A_DV = 256
B_DINNER = 4096
B_HEADDIM = 64
B_HEADS = 64
B_GROUPS = 8
B_DSTATE = 128
B_DCONV = 4
B_CONV_DIM = 6144
D_FF = 5632
FFN_CONV = 3
EPS = 1e-6
NEG_INF = float("-inf")
LOG2E = 1.4426950408889634

LANES = 128
SUBLANES = 8
VMEM_LIMIT_BYTES = 56 * 1024 * 1024

HEADS_PER_GROUP = B_HEADS // B_GROUPS
GROUP_CH = HEADS_PER_GROUP * B_HEADDIM
SEQ_GROUP = SUBLANES


def _params(sem):
    return pltpu.CompilerParams(dimension_semantics=sem, vmem_limit_bytes=VMEM_LIMIT_BYTES)


def _sigmoid(x):
    return 1.0 / (1.0 + jnp.exp(-x))


def _silu(x):
    return x * _sigmoid(x)


def _softplus(x):
    return jnp.maximum(x, 0.0) + jnp.log1p(jnp.exp(-jnp.abs(x)))


def _log_sigmoid(x):
    return -_softplus(-x)


def _rms(x, w):
    ms = jnp.mean(x * x, axis=-1, keepdims=True)
    return x * lax.rsqrt(ms + EPS) * w


def _cumsum_rows(x):
    n = x.shape[0]
    row = lax.broadcasted_iota(jnp.int32, x.shape, 0)
    k = 1
    while k < n:
        x = x + jnp.where(row >= k, pltpu.roll(x, k, 0), 0.0)
        k *= 2
    return x


def _cumsum_lanes(x):
    n = x.shape[1]
    col = lax.broadcasted_iota(jnp.int32, x.shape, 1)
    k = 1
    while k < n:
        x = x + jnp.where(col >= k, pltpu.roll(x, k, 1), 0.0)
        k *= 2
    return x


def _dot(a, b):
    return jnp.dot(a, b, preferred_element_type=F32)


def _dot_nt(a, b):
    return lax.dot_general(a, b, (((1,), (1,)), ((), ())), preferred_element_type=F32)


def _dot_tn(a, b):
    return lax.dot_general(a, b, (((0,), (0,)), ((), ())), preferred_element_type=F32)


def _conv_taps(u, shifted, w_ref, b_ref, ntaps):
    y = u * w_ref[ntaps - 1:ntaps, :]
    for k in range(1, ntaps):
        y = y + shifted(k) * w_ref[ntaps - 1 - k:ntaps - k, :]
    return y + b_ref[...]


def _shift_with_prev(u, prev8, k):
    ext = jnp.concatenate([prev8, u], axis=0)
    return pltpu.roll(ext, k, 0)[SUBLANES:]


def _pad_rows(a, rows):
    if a.shape[0] == rows:
        return a
    return jnp.concatenate([a, jnp.zeros((rows - a.shape[0], a.shape[1]), a.dtype)], axis=0)


class _SampleRows:
    def __init__(self, rows, seq_len):
        self.rows = rows
        self.seq_len = seq_len
        r = lax.broadcasted_iota(jnp.int32, (rows, 1), 0)
        self.token = (r // SEQ_GROUP) % seq_len
        self.seq = r % SEQ_GROUP

    def back(self, a, k):
        if k == 0:
            return a
        return jnp.where(self.token >= k, pltpu.roll(a, SEQ_GROUP * k, 0), 0.0)

    def ahead(self, a, k):
        if k == 0:
            return a
        return jnp.where(self.token + k < self.seq_len, pltpu.roll(a, self.rows - SEQ_GROUP * k, 0), 0.0)

    def conv_shift(self, u, state, ntaps):
        return lambda k: self.back(u, k) + self.ahead(state, ntaps - 1 - k)


def _tokens_to_rows(slots, seq_len):
    nseq, ch = slots[0].shape
    pad = jnp.zeros(((seq_len - len(slots)) * SEQ_GROUP, ch), slots[0].dtype)
    parts = []
    for g in range(nseq // SEQ_GROUP):
        parts += [s[g * SEQ_GROUP:(g + 1) * SEQ_GROUP] for s in slots] + [pad]
    return jnp.concatenate(parts, axis=0)


def _rows_of_token(a, seq_len, t):
    step = seq_len * SEQ_GROUP
    return jnp.concatenate([a[g * step + t * SEQ_GROUP:g * step + (t + 1) * SEQ_GROUP]
                            for g in range(a.shape[0] // step)], axis=0)


def _norm_matmul_kernel(x_ref, nw_ref, w_ref, ws_ref, o_ref, os_ref, *rest):
    hn_ref = rest[-1]

    @pl.when(pl.program_id(1) == 0)
    def _():
        hn = _rms(x_ref[...], nw_ref[...]).astype(BF16)
        hn_ref[...] = hn
        os_ref[...] = _dot(hn, ws_ref[...])

    wb = w_ref[...].astype(BF16)
    if len(rest) == 2:
        rest[0][...] = wb
    o_ref[...] = _dot(hn_ref[...], wb)


def _norm_matmul(x, nw, layer, w, n_main, ws, *, tm, tn):
    T, D = x.shape
    emit = w.dtype != BF16
    w_spec = pl.BlockSpec((None, D, tn), lambda i, j: (0, 0, j))
    out_specs = [pl.BlockSpec((tm, tn), lambda i, j: (i, j)), pl.BlockSpec((tm, LANES), lambda i, j: (i, 0))]
    out_shape = [SDS((T, n_main), F32), SDS((T, LANES), F32)]
    if emit:
        assert T == tm, "the bfloat16 copy is written once, by a single row tile"
        out_specs.append(w_spec)
        out_shape.append(SDS((1, D, n_main), BF16))
    return pl.pallas_call(
        _norm_matmul_kernel,
        grid=(T // tm, n_main // tn),
        in_specs=[pl.BlockSpec((tm, D), lambda i, j: (i, 0)),
                  pl.BlockSpec((None, 1, D), lambda i, j: (layer, 0, 0)),
                  w_spec,
                  pl.BlockSpec((D, LANES), lambda i, j: (0, 0))],
        out_specs=out_specs,
        out_shape=out_shape,
        scratch_shapes=[pltpu.VMEM((tm, D), BF16)],
        compiler_params=_params(("arbitrary", "arbitrary")),
        name="norm_matmul",
    )(x, nw, w, ws)


def _matmul_res_kernel(a_ref, w_ref, r_ref, o_ref, *rest):
    wb = w_ref[...].astype(BF16)
    if rest:
        rest[0][...] = wb
    o_ref[...] = r_ref[...] + _dot(a_ref[...].astype(BF16), wb)


def _matmul_res(a, w, res, *, tm, tn):
    T, K = a.shape
    N = w.shape[2]
    emit = w.dtype != BF16
    w_spec = pl.BlockSpec((None, K, tn), lambda i, j: (0, 0, j))
    out_specs = [pl.BlockSpec((tm, tn), lambda i, j: (i, j))]
    out_shape = [SDS((T, N), F32)]
    if emit:
        assert T == tm, "the bfloat16 copy is written once, by a single row tile"
        out_specs.append(w_spec)
        out_shape.append(SDS((1, K, N), BF16))
    return pl.pallas_call(
        _matmul_res_kernel,
        grid=(T // tm, N // tn),
        in_specs=[pl.BlockSpec((tm, K), lambda i, j: (i, 0)),
                  w_spec,
                  pl.BlockSpec((tm, tn), lambda i, j: (i, j))],
        out_specs=out_specs,
        out_shape=out_shape,
        compiler_params=_params(("arbitrary", "arbitrary")),
        name="matmul_res",
    )(a, w, res)


def _ffn_kernel(*refs, tm, tiles_per_seq, seq_len, sample, final_norm):
    refs = list(refs)
    x_ref, nw_ref, wug_ref, wuv_ref, cwg_ref, cwv_ref, cbg_ref, cbv_ref, wd_ref = refs[:9]
    del refs[:9]
    nstate = FFN_CONV - 1
    if sample:
        sg_refs, sv_refs = refs[:nstate], refs[nstate:2 * nstate]
        del refs[:2 * nstate]
    if final_norm:
        fw_ref = refs.pop(0)
    out_ref = refs.pop(0)
    if sample:
        tg_refs, tv_refs = refs[:nstate], refs[nstate:2 * nstate]
        wbg_ref, wbv_ref, wbd_ref = refs[2 * nstate:2 * nstate + 3]
        del refs[:2 * nstate + 3]
    else:
        tg_ref, tv_ref = refs[:2]
        del refs[:2]
    hn_s, acc_s = refs[:2]
    i = pl.program_id(0)
    j = pl.program_id(1)

    @pl.when(j == 0)
    def _():
        hn_s[...] = _rms(x_ref[...], nw_ref[...]).astype(BF16)
        acc_s[...] = jnp.zeros_like(acc_s)

    wug = wug_ref[...].astype(BF16)
    wuv = wuv_ref[...].astype(BF16)
    wd = wd_ref[...].astype(BF16)
    hn = hn_s[...]
    ug = _dot(hn, wug)
    uv = _dot(hn, wuv)

    if sample:
        wbg_ref[...] = wug
        wbv_ref[...] = wuv
        wbd_ref[...] = wd
        for t in range(nstate):
            tg_refs[t][...] = _rows_of_token(ug, seq_len, seq_len - nstate + t)
            tv_refs[t][...] = _rows_of_token(uv, seq_len, seq_len - nstate + t)
        sr = _SampleRows(tm, seq_len)
        sg = _tokens_to_rows([r[...] for r in sg_refs], seq_len)
        sv = _tokens_to_rows([r[...] for r in sv_refs], seq_len)
        yg = _conv_taps(ug, sr.conv_shift(ug, sg, FFN_CONV), cwg_ref, cbg_ref, FFN_CONV)
        yv = _conv_taps(uv, sr.conv_shift(uv, sv, FFN_CONV), cwv_ref, cbv_ref, FFN_CONV)
    else:
        cg_s, cv_s = refs[2:4]
        tail_g = ug[tm - SUBLANES:, :]
        tail_v = uv[tm - SUBLANES:, :]
        tg_ref[0] = tail_g
        tv_ref[0] = tail_v

        @pl.when(i % tiles_per_seq == 0)
        def _():
            cg_s[j] = jnp.zeros_like(tail_g)
            cv_s[j] = jnp.zeros_like(tail_v)

        prev_g = cg_s[j]
        prev_v = cv_s[j]
        yg = _conv_taps(ug, lambda k: _shift_with_prev(ug, prev_g, k), cwg_ref, cbg_ref, FFN_CONV)
        yv = _conv_taps(uv, lambda k: _shift_with_prev(uv, prev_v, k), cwv_ref, cbv_ref, FFN_CONV)
        cg_s[j] = tail_g
        cv_s[j] = tail_v

    act = (_silu(yg) * yv).astype(BF16)
    acc_s[...] += _dot(act, wd)

    @pl.when(j == pl.num_programs(1) - 1)
    def _():
        y = x_ref[...] + acc_s[...]
        out_ref[...] = _rms(y, fw_ref[...]) if final_norm else y


def _ffn(x, layer, w, *, tm, tf, seq_len, state=None, final_w=None):
    T, D = x.shape
    nj = D_FF // tf
    ni = T // tm
    sample = state is not None
    nstate = FFN_CONV - 1
    tiles_per_seq = max(seq_len // tm, 1)
    in_specs = [pl.BlockSpec((tm, D), lambda i, j: (i, 0)),
                pl.BlockSpec((None, 1, D), lambda i, j: (layer, 0, 0))]
    args = [x, w["norm_ffn"]]
    if sample:
        in_specs += [pl.BlockSpec((None, D, tf), lambda i, j: (layer, 0, j)),
                     pl.BlockSpec((None, D, tf), lambda i, j: (layer, 0, nj + j))]
        args += [w["f_w_up"], w["f_w_up"]]
    else:
        in_specs += [pl.BlockSpec((D, tf), lambda i, j: (0, j))] * 2
        args += list(w["ffn_bf16"][layer][:2])
    in_specs += [pl.BlockSpec((None, FFN_CONV, tf), lambda i, j: (layer, 0, j)),
                 pl.BlockSpec((None, FFN_CONV, tf), lambda i, j: (layer, 0, nj + j)),
                 pl.BlockSpec((None, 1, tf), lambda i, j: (layer, 0, j)),
                 pl.BlockSpec((None, 1, tf), lambda i, j: (layer, 0, nj + j))]
    args += [w["f_conv_w"], w["f_conv_w"], w["f_conv_b"], w["f_conv_b"]]
    scratch = [pltpu.VMEM((tm, D), BF16), pltpu.VMEM((tm, D), F32)]
    if sample:
        assert ni == 1, "sample rows form a single tile"
        nseq = T // seq_len
        in_specs.append(pl.BlockSpec((None, tf, D), lambda i, j: (layer, j, 0)))
        args.append(w["f_w_down"])
        in_specs += [pl.BlockSpec((nseq, tf), lambda i, j, t=t: (0, t * 2 * nj + j)) for t in range(nstate)]
        in_specs += [pl.BlockSpec((nseq, tf), lambda i, j, t=t: (0, t * 2 * nj + nj + j)) for t in range(nstate)]
        args += [state] * (2 * nstate)
        extra_specs = [pl.BlockSpec((nseq, tf), lambda i, j: (0, j))] * (2 * nstate)
        extra_shapes = [SDS((nseq, D_FF), F32)] * (2 * nstate)
        extra_specs += [pl.BlockSpec((D, tf), lambda i, j: (0, j))] * 2 + [pl.BlockSpec((tf, D), lambda i, j: (j, 0))]
        extra_shapes += [SDS((D, D_FF), BF16)] * 2 + [SDS((D_FF, D), BF16)]
    else:
        in_specs.append(pl.BlockSpec((tf, D), lambda i, j: (j, 0)))
        args.append(w["ffn_bf16"][layer][2])
        scratch += [pltpu.VMEM((nj, SUBLANES, tf), F32)] * 2
        extra_specs = [pl.BlockSpec((1, SUBLANES, tf), lambda i, j: (i, 0, j))] * 2
        extra_shapes = [SDS((ni, SUBLANES, D_FF), F32)] * 2
    if final_w is not None:
        in_specs.append(pl.BlockSpec((1, D), lambda i, j: (0, 0)))
        args.append(final_w)
    return pl.pallas_call(
        functools.partial(_ffn_kernel, tm=tm, tiles_per_seq=tiles_per_seq, seq_len=seq_len, sample=sample,
                          final_norm=final_w is not None),
        grid=(ni, nj),
        in_specs=in_specs,
        out_specs=[pl.BlockSpec((tm, D), lambda i, j: (i, 0))] + extra_specs,
        out_shape=[SDS((T, D), F32)] + extra_shapes,
        scratch_shapes=scratch,
        compiler_params=_params(("arbitrary", "arbitrary")),
        name="ffn_sample" if sample else "ffn_prompt",
    )(*args)


def _mlstm_intra(q, ks, v, ig_col, lf_cum_col, lf_cum_row, ig_row, m_prev_col, causal):
    log_d = jnp.where(causal, lf_cum_col - lf_cum_row + ig_row, NEG_INF)
    inter = lf_cum_col + m_prev_col
    m_t = jnp.maximum(inter, jnp.max(log_d, axis=1, keepdims=True))
    d_m = jnp.exp(log_d - m_t)
    w_inter = jnp.exp(inter - m_t)
    kb = _pad_rows(ks.astype(BF16), causal.shape[1])
    vb = _pad_rows(v.astype(BF16), causal.shape[1])
    s = _dot_nt(q.astype(BF16), kb) * d_m
    num = _dot(s.astype(BF16), vb)
    den = jnp.sum(s, axis=1, keepdims=True)
    return m_t, w_inter, inter, num, den


def _mlstm_prompt_kernel(q_ref, k_ref, v_ref, o_ref, g_ref, gt_ref, br_ref, bc_ref,
                         hg_ref, co_ref, no_ref, mo_ref, c_s, n_s, m_s, *, c):
    ci = pl.program_id(1)

    @pl.when(ci == 0)
    def _():
        c_s[...] = jnp.zeros_like(c_s)
        n_s[...] = jnp.zeros_like(n_s)
        m_s[...] = jnp.zeros_like(m_s)

    g = g_ref[...] + br_ref[...]
    lf_cum = _cumsum_rows(_log_sigmoid(g))
    gt = gt_ref[...] + bc_ref[...]
    lf_cum_t = _cumsum_lanes(_log_sigmoid(gt))
    row = lax.broadcasted_iota(jnp.int32, (c, c), 0)
    col = lax.broadcasted_iota(jnp.int32, (c, c), 1)
    causal = col <= row
    last = ci == pl.num_programs(1) - 1

    for h in range(A_HEADS):
        ig_col = g[:, h:h + 1]
        b_col = lf_cum[:, A_HEADS + h:A_HEADS + h + 1]
        ig_row = gt[h:h + 1, :]
        b_row = lf_cum_t[A_HEADS + h:A_HEADS + h + 1, :]
        q = q_ref[:, h * A_DK:(h + 1) * A_DK]
        ks = k_ref[:, h * A_DK:(h + 1) * A_DK] * (A_DK ** -0.5)
        v = v_ref[:, h * A_DV:(h + 1) * A_DV]
        m_prev = m_s[h][:, 0:1]
        m_t, w_inter, inter, num, den = _mlstm_intra(q, ks, v, ig_col, b_col, b_row, ig_row, m_prev, causal)
        cmat = c_s[h]
        nvec = n_s[h]
        num = w_inter * _dot(q.astype(BF16), cmat.astype(BF16)) + num
        den = w_inter * jnp.sum(q * nvec, axis=1, keepdims=True) + den
        hh = num / jnp.maximum(jnp.abs(den), jnp.exp(-m_t))
        hg_ref[:, h * A_DV:(h + 1) * A_DV] = (_sigmoid(o_ref[:, h * A_DV:(h + 1) * A_DV]) * hh).astype(BF16)

        m_new = m_t[c - 1:c, :]
        w_state = jnp.exp(b_col[c - 1:c, :] - b_col + ig_col - m_new)
        decay = jnp.exp(inter[c - 1:c, :] - m_new)
        kw = ks * w_state
        c_new = decay * cmat + _dot_tn(kw.astype(BF16), v.astype(BF16))
        n_new = decay * nvec + jnp.sum(kw, axis=0, keepdims=True)
        m_row = jnp.broadcast_to(m_new, (1, LANES))
        c_s[h] = c_new
        n_s[h] = n_new
        m_s[h] = m_row

        @pl.when(last)
        def _():
            co_ref[0, h] = c_new
            no_ref[0, h] = n_new
            mo_ref[0, h] = m_row


def _mlstm_prompt(p, gates, gates_t, bias_row, bias_col, *, nseq, seq_len, c):
    T = p.shape[0]
    nc = seq_len // c
    hk = A_HEADS * A_DK
    hv = A_HEADS * A_DV
    rows = lambda b, ci: b * nc + ci
    return pl.pallas_call(
        functools.partial(_mlstm_prompt_kernel, c=c),
        grid=(nseq, nc),
        in_specs=[pl.BlockSpec((c, hk), lambda b, ci: (rows(b, ci), 0)),
                  pl.BlockSpec((c, hk), lambda b, ci: (rows(b, ci), 1)),
                  pl.BlockSpec((c, hv), lambda b, ci: (rows(b, ci), 1)),
                  pl.BlockSpec((c, hv), lambda b, ci: (rows(b, ci), 2)),
                  pl.BlockSpec((c, LANES), lambda b, ci: (rows(b, ci), 0)),
                  pl.BlockSpec((2 * A_HEADS, c), lambda b, ci: (0, rows(b, ci))),
                  pl.BlockSpec((1, LANES), lambda b, ci: (0, 0)),
                  pl.BlockSpec((2 * A_HEADS, 1), lambda b, ci: (0, 0))],
        out_specs=[pl.BlockSpec((c, hv), lambda b, ci: (rows(b, ci), 0)),
                   pl.BlockSpec((1, A_HEADS, A_DK, A_DV), lambda b, ci: (b, 0, 0, 0)),
                   pl.BlockSpec((1, A_HEADS, 1, A_DK), lambda b, ci: (b, 0, 0, 0)),
                   pl.BlockSpec((1, A_HEADS, 1, LANES), lambda b, ci: (b, 0, 0, 0))],
        out_shape=[SDS((T, hv), BF16),
                   SDS((nseq, A_HEADS, A_DK, A_DV), F32),
                   SDS((nseq, A_HEADS, 1, A_DK), F32),
                   SDS((nseq, A_HEADS, 1, LANES), F32)],
        scratch_shapes=[pltpu.VMEM((A_HEADS, A_DK, A_DV), F32), pltpu.VMEM((A_HEADS, 1, A_DK), F32),
                        pltpu.VMEM((A_HEADS, 1, LANES), F32)],
        compiler_params=_params(("arbitrary", "arbitrary")),
        name="mlstm_prompt",
    )(p, p, p, p, gates, gates_t, bias_row, bias_col)


def _mlstm_sample_kernel(q_ref, k_ref, v_ref, o_ref, g_ref, br_ref, mp_ref, ci_ref, ni_ref,
                         hg_ref, co_ref, no_ref, mo_ref, *, seq_len):
    nbs = SEQ_GROUP
    rs = nbs * seq_len
    g = g_ref[...] + br_ref[...]
    lf_all = _log_sigmoid(g)
    m_prev_all = mp_ref[...]
    row = lax.broadcasted_iota(jnp.int32, (rs, LANES), 0)
    col = lax.broadcasted_iota(jnp.int32, (rs, LANES), 1)
    same = ((row % nbs) == (col % nbs)) & (col < rs)
    causal = same & (col <= row)
    causal_t = same & (row <= col)
    eye = row == col
    last = same & (col // nbs == seq_len - 1)
    seq_of_row = lax.broadcasted_iota(jnp.int32, (rs, 1), 0) % nbs
    prow = lax.broadcasted_iota(jnp.int32, (LANES, 1), 0)
    seq_of_prow = jnp.where(prow < rs, prow % nbs, -1)

    def to_row(colvec):
        return jnp.sum(jnp.where(eye, colvec, 0.0), axis=0, keepdims=True)

    def at_last(rowvec):
        return jnp.sum(jnp.where(last, rowvec, 0.0), axis=1, keepdims=True)

    m_out = jnp.zeros((rs, LANES), F32)
    for h in range(A_HEADS):
        ig_col = g[:, h:h + 1]
        lf_col = lf_all[:, A_HEADS + h:A_HEADS + h + 1]
        m_prev = m_prev_all[:, h:h + 1]
        ig_row = to_row(ig_col)
        lf_row = to_row(lf_col)
        b_col = jnp.sum(jnp.where(causal, lf_row, 0.0), axis=1, keepdims=True)
        b_row = jnp.sum(jnp.where(causal_t, lf_col, 0.0), axis=0, keepdims=True)
        q = q_ref[:, h * A_DK:(h + 1) * A_DK]
        ks = k_ref[:, h * A_DK:(h + 1) * A_DK] * (A_DK ** -0.5)
        v = v_ref[:, h * A_DV:(h + 1) * A_DV]
        m_t, w_inter, inter, num, den = _mlstm_intra(q, ks, v, ig_col, b_col, b_row, ig_row, m_prev, causal)
        qb = q.astype(BF16)
        num_i = jnp.zeros((rs, A_DV), F32)
        qn = jnp.zeros((rs, 1), F32)
        for bb in range(nbs):
            mine = seq_of_row == bb
            num_i = jnp.where(mine, _dot(qb, ci_ref[bb, h].astype(BF16)), num_i)
            qn = jnp.where(mine, jnp.sum(q * ni_ref[bb, h], axis=1, keepdims=True), qn)
        num = w_inter * num_i + num
        den = w_inter * qn + den
        hh = num / jnp.maximum(jnp.abs(den), jnp.exp(-m_t))
        hg_ref[:, h * A_DV:(h + 1) * A_DV] = _sigmoid(o_ref[:, h * A_DV:(h + 1) * A_DV]) * hh

        m_new = at_last(to_row(m_t))
        w_state = jnp.exp(at_last(b_row) - b_col + ig_col - m_new)
        decay = jnp.exp(at_last(to_row(inter)) - m_new)
        kw = ks * w_state
        kw_t = _pad_rows(kw, LANES).T.astype(BF16)
        v_pad = _pad_rows(v, LANES)
        for bb in range(nbs):
            v_mine = jnp.where(seq_of_prow == bb, v_pad, 0.0).astype(BF16)
            dec = decay[bb:bb + 1, :]
            co_ref[bb, h] = dec * ci_ref[bb, h] + _dot(kw_t, v_mine)
            no_ref[bb, h] = dec * ni_ref[bb, h] + jnp.sum(jnp.where(seq_of_row == bb, kw, 0.0), axis=0, keepdims=True)
        m_out = jnp.where(col == h, m_new, m_out)
    mo_ref[...] = m_out


def _mlstm_sample(p, gates, bias_row, m_rows, c_in, n_in, *, seq_len):
    T = p.shape[0]
    nbs = SEQ_GROUP
    rs = nbs * seq_len
    nb = c_in.shape[0]
    hk = A_HEADS * A_DK
    hv = A_HEADS * A_DV
    return pl.pallas_call(
        functools.partial(_mlstm_sample_kernel, seq_len=seq_len),
        grid=(nb // nbs,),
        in_specs=[pl.BlockSpec((rs, hk), lambda i: (i, 0)),
                  pl.BlockSpec((rs, hk), lambda i: (i, 1)),
                  pl.BlockSpec((rs, hv), lambda i: (i, 1)),
                  pl.BlockSpec((rs, hv), lambda i: (i, 2)),
                  pl.BlockSpec((rs, LANES), lambda i: (i, 0)),
                  pl.BlockSpec((1, LANES), lambda i: (0, 0)),
                  pl.BlockSpec((rs, LANES), lambda i: (i, 0)),
                  pl.BlockSpec((nbs, A_HEADS, A_DK, A_DV), lambda i: (i, 0, 0, 0)),
                  pl.BlockSpec((nbs, A_HEADS, 1, A_DK), lambda i: (i, 0, 0, 0))],
        out_specs=[pl.BlockSpec((rs, hv), lambda i: (i, 0)),
                   pl.BlockSpec((nbs, A_HEADS, A_DK, A_DV), lambda i: (i, 0, 0, 0)),
                   pl.BlockSpec((nbs, A_HEADS, 1, A_DK), lambda i: (i, 0, 0, 0)),
                   pl.BlockSpec((rs, LANES), lambda i: (i, 0))],
        out_shape=[SDS((T, hv), F32),
                   SDS((nb, A_HEADS, A_DK, A_DV), F32),
                   SDS((nb, A_HEADS, 1, A_DK), F32),
                   SDS((T, LANES), F32)],
        compiler_params=_params(("arbitrary",)),
        name="mlstm_sample",
    )(p, p, p, p, gates, bias_row, m_rows, c_in, n_in)


def _pair_select(a, h0):
    lane = lax.broadcasted_iota(jnp.int32, (a.shape[0], LANES), 1)
    return jnp.where(lane < B_HEADDIM, a[:, h0:h0 + 1], a[:, h0 + 1:h0 + 2])


def _group_heads_first(a, g):
    return pltpu.roll(a, (LANES - HEADS_PER_GROUP * g) % LANES, 1)


def _group_norm_gate(y, z, nw):
    gz = y * _silu(z)
    return gz * lax.rsqrt(jnp.mean(gz * gz, axis=-1, keepdims=True) + EPS) * nw


def _ssd_prompt_kernel(x_ref, b_ref, c_ref, z_ref, dtc_ref, dtr_ref,
                       cwx_ref, cwb_ref, cwc_ref, cbx_ref, cbb_ref, cbc_ref,
                       dbr_ref, dbc_ref, alr_ref, alc_ref, d_ref, nw_ref,
                       y_ref, so_ref, s_s, xp_s, bp_s, cp_s, *, c):
    ci = pl.program_id(2)

    @pl.when(ci == 0)
    def _():
        s_s[...] = jnp.zeros_like(s_s)
        xp_s[...] = jnp.zeros_like(xp_s)
        bp_s[...] = jnp.zeros_like(bp_s)
        cp_s[...] = jnp.zeros_like(cp_s)

    def conv(raw_ref, prev_s, cw_ref, cb_ref):
        raw = raw_ref[...]
        prev = prev_s[...]
        out = _conv_taps(raw, lambda k: _shift_with_prev(raw, prev, k), cw_ref, cb_ref, B_DCONV)
        prev_s[...] = raw[c - SUBLANES:, :]
        return _silu(out)

    x = conv(x_ref, xp_s, cwx_ref, cbx_ref)
    bm = conv(b_ref, bp_s, cwb_ref, cbb_ref)
    cm = conv(c_ref, cp_s, cwc_ref, cbc_ref)

    dt_c = _softplus(_group_heads_first(dtc_ref[...], pl.program_id(1)) + dbr_ref[...])
    ac_c = _cumsum_rows(dt_c * (-jnp.exp(alr_ref[...])))
    dt_r = _softplus(dtr_ref[...] + dbc_ref[...])
    ac_r = _cumsum_lanes(dt_r * (-jnp.exp(alc_ref[...])))
    ac_c2 = ac_c * LOG2E
    ac_r2 = ac_r * LOG2E

    row = lax.broadcasted_iota(jnp.int32, (c, c), 0)
    col = lax.broadcasted_iota(jnp.int32, (c, c), 1)
    bmb = bm.astype(BF16)
    cmb = cm.astype(BF16)
    cb = jnp.where(col <= row, _dot_nt(cmb, bmb), 0.0)
    state = s_s[...]
    y_inter = _dot(cmb, state.astype(BF16))
    ac_last = ac_c[c - 1:c, :]
    to_end = jnp.exp(ac_last - ac_c) * dt_c
    e_ac = jnp.exp(ac_c)
    e_last = jnp.exp(ac_last)
    lane = lax.broadcasted_iota(jnp.int32, (c, LANES), 1)
    d_all = d_ref[...]

    ys = []
    for pr in range(HEADS_PER_GROUP // 2):
        h0 = 2 * pr
        sl = slice(pr * LANES, (pr + 1) * LANES)
        xp = x[:, sl]
        xpb = xp.astype(BF16)
        full = []
        for r in (h0, h0 + 1):
            decay = jnp.exp2(jnp.minimum(ac_c2[:, r:r + 1] - ac_r2[r:r + 1, :], 0.0))
            wts = cb * decay * dt_r[r:r + 1, :]
            full.append(_dot(wts.astype(BF16), xpb))
        yp = jnp.where(lane < B_HEADDIM, full[0], full[1])
        yp = yp + y_inter[:, sl] * _pair_select(e_ac, h0) + d_all[:, sl] * xp
        ys.append(yp)
        xw = (xp * _pair_select(to_end, h0)).astype(BF16)
        s_s[:, sl] = _pair_select(e_last, h0) * state[:, sl] + _dot_tn(bmb, xw)
    y = jnp.concatenate(ys, axis=1)
    y_ref[...] = _group_norm_gate(y, z_ref[...], nw_ref[...]).astype(BF16)

    @pl.when(ci == pl.num_programs(2) - 1)
    def _():
        so_ref[0, 0] = s_s[...]


def _ssd_prompt(p, dt_cols, dt_rows, w, *, nseq, seq_len, c):
    T = p.shape[0]
    nc = seq_len // c
    rows = lambda b, g, ci: b * nc + ci
    kx = B_DINNER // GROUP_CH
    kb = (2 * B_DINNER) // B_DSTATE
    kc = kb + B_GROUPS
    wb = B_DINNER // B_DSTATE
    wc = wb + B_GROUPS
    G = B_GROUPS
    per_lane = lambda width, start=0: pl.BlockSpec((1, width), lambda b, g, ci: (0, start + g))
    per_row = pl.BlockSpec((HEADS_PER_GROUP, 1), lambda b, g, ci: (g, 0))
    return pl.pallas_call(
        functools.partial(_ssd_prompt_kernel, c=c),
        grid=(nseq, G, nc),
        in_specs=[pl.BlockSpec((c, GROUP_CH), lambda b, g, ci: (rows(b, g, ci), kx + g)),
                  pl.BlockSpec((c, B_DSTATE), lambda b, g, ci: (rows(b, g, ci), kb + g)),
                  pl.BlockSpec((c, B_DSTATE), lambda b, g, ci: (rows(b, g, ci), kc + g)),
                  pl.BlockSpec((c, GROUP_CH), lambda b, g, ci: (rows(b, g, ci), g)),
                  pl.BlockSpec((c, LANES), lambda b, g, ci: (rows(b, g, ci), 0)),
                  pl.BlockSpec((HEADS_PER_GROUP, c), lambda b, g, ci: (g, rows(b, g, ci))),
                  pl.BlockSpec((None, B_DCONV, GROUP_CH), lambda b, g, ci: (0, 0, g)),
                  pl.BlockSpec((None, B_DCONV, B_DSTATE), lambda b, g, ci: (0, 0, wb + g)),
                  pl.BlockSpec((None, B_DCONV, B_DSTATE), lambda b, g, ci: (0, 0, wc + g)),
                  per_lane(GROUP_CH), per_lane(B_DSTATE, wb), per_lane(B_DSTATE, wc),
                  per_lane(LANES), per_row, per_lane(LANES), per_row,
                  per_lane(GROUP_CH), per_lane(GROUP_CH)],
        out_specs=[pl.BlockSpec((c, GROUP_CH), lambda b, g, ci: (rows(b, g, ci), g)),
                   pl.BlockSpec((1, 1, B_DSTATE, GROUP_CH), lambda b, g, ci: (b, g, 0, 0))],
        out_shape=[SDS((T, B_DINNER), BF16), SDS((nseq, G, B_DSTATE, GROUP_CH), F32)],
        scratch_shapes=[pltpu.VMEM((B_DSTATE, GROUP_CH), F32),
                        pltpu.VMEM((SUBLANES, GROUP_CH), F32),
                        pltpu.VMEM((SUBLANES, B_DSTATE), F32),
                        pltpu.VMEM((SUBLANES, B_DSTATE), F32)],
        compiler_params=_params(("arbitrary", "arbitrary", "arbitrary")),
        name="ssd_prompt",
    )(p, p, p, p, dt_cols, dt_rows, w["b_conv_w"], w["b_conv_w"], w["b_conv_w"],
      w["b_conv_b"], w["b_conv_b"], w["b_conv_b"],
      w["dtb_cols"], w["dtb_rows"], w["alog_cols"], w["alog_rows"], w["d_lane"], w["b_norm_w"])


def _ssd_sample_kernel(x_ref, b_ref, c_ref, z_ref, dt_ref, sx_ref, sb_ref, sc_ref,
                       cwx_ref, cwb_ref, cwc_ref, cbx_ref, cbb_ref, cbc_ref,
                       dtb_ref, al_ref, d_ref, nw_ref, hs_ref,
                       y_ref, hso_ref, *, seq_len):
    nbs = SEQ_GROUP
    rs = nbs * seq_len
    sr = _SampleRows(rs, seq_len)
    prow = lax.broadcasted_iota(jnp.int32, (LANES, 1), 0)
    seq_of_prow = jnp.where(prow < rs, prow % nbs, -1)

    def conv(raw_ref, st_ref, cw_ref, cb_ref):
        raw = raw_ref[...]
        return _silu(_conv_taps(raw, sr.conv_shift(raw, st_ref[...], B_DCONV), cw_ref, cb_ref, B_DCONV))

    x = conv(x_ref, sx_ref, cwx_ref, cbx_ref)
    bm = conv(b_ref, sb_ref, cwb_ref, cbb_ref)
    cm = conv(c_ref, sc_ref, cwc_ref, cbc_ref)

    dt = _softplus(_group_heads_first(dt_ref[...], pl.program_id(1)) + dtb_ref[...])
    a = dt * (-jnp.exp(al_ref[...]))
    acum = a
    total = a
    for k in range(1, seq_len):
        acum = acum + sr.back(a, k)
        total = total + sr.back(a, k) + sr.ahead(a, k)

    coefs = []
    for d in range(seq_len):
        cb_d = jnp.sum(cm * sr.back(bm, d), axis=1, keepdims=True)
        coefs.append(cb_d * jnp.exp(acum - sr.back(acum, d)) * sr.back(dt, d))
    x_back = [sr.back(x, d) for d in range(seq_len)]

    e_ac = jnp.exp(acum)
    e_tot = jnp.exp(total)
    to_end = jnp.exp(total - acum) * dt
    d_all = d_ref[...]

    ys = []
    xws = []
    for pr in range(HEADS_PER_GROUP // 2):
        h0 = 2 * pr
        sl = slice(pr * LANES, (pr + 1) * LANES)
        yp = d_all[:, sl] * x[:, sl]
        for d in range(seq_len):
            yp = yp + _pair_select(coefs[d], h0) * x_back[d][:, sl]
        ys.append(yp)
        xws.append(x[:, sl] * _pair_select(to_end, h0))
    xw_t = _pad_rows(jnp.concatenate(xws, axis=1), LANES).T.astype(BF16)
    cmb = cm.astype(BF16)
    bm_pad = _pad_rows(bm, LANES)
    y_int = jnp.zeros((rs, GROUP_CH), F32)
    for bb in range(nbs):
        hs_g = hs_ref[bb].reshape(GROUP_CH, B_DSTATE)
        y_int = jnp.where(sr.seq == bb, _dot_nt(cmb, hs_g.astype(BF16)), y_int)
        b_mine = jnp.where(seq_of_prow == bb, bm_pad, 0.0).astype(BF16)
        upd = _dot(xw_t, b_mine)
        for r in range(HEADS_PER_GROUP):
            dec = e_tot[bb:bb + 1, r:r + 1]
            hso_ref[bb, r] = dec * hs_ref[bb, r] + upd[r * B_HEADDIM:(r + 1) * B_HEADDIM, :]
    e_pairs = jnp.concatenate([_pair_select(e_ac, 2 * pr) for pr in range(HEADS_PER_GROUP // 2)], axis=1)
    y = jnp.concatenate(ys, axis=1) + y_int * e_pairs
    y_ref[...] = _group_norm_gate(y, z_ref[...], nw_ref[...])


def _ssd_sample(p, dt_cols, conv_state, w, hs, *, seq_len):
    T = p.shape[0]
    nbs = SEQ_GROUP
    rs = nbs * seq_len
    nb = hs.shape[0]
    G = B_GROUPS
    kx = B_DINNER // GROUP_CH
    kb = (2 * B_DINNER) // B_DSTATE
    kc = kb + G
    wb = B_DINNER // B_DSTATE
    wc = wb + G
    per_lane = lambda width, start=0: pl.BlockSpec((1, width), lambda i, g: (0, start + g))
    hs_spec = pl.BlockSpec((nbs, HEADS_PER_GROUP, B_HEADDIM, B_DSTATE), lambda i, g: (i, g, 0, 0))
    return pl.pallas_call(
        functools.partial(_ssd_sample_kernel, seq_len=seq_len),
        grid=(nb // nbs, G),
        in_specs=[pl.BlockSpec((rs, GROUP_CH), lambda i, g: (i, kx + g)),
                  pl.BlockSpec((rs, B_DSTATE), lambda i, g: (i, kb + g)),
                  pl.BlockSpec((rs, B_DSTATE), lambda i, g: (i, kc + g)),
                  pl.BlockSpec((rs, GROUP_CH), lambda i, g: (i, g)),
                  pl.BlockSpec((rs, LANES), lambda i, g: (i, 0)),
                  pl.BlockSpec((rs, GROUP_CH), lambda i, g: (i, g)),
                  pl.BlockSpec((rs, B_DSTATE), lambda i, g: (i, wb + g)),
                  pl.BlockSpec((rs, B_DSTATE), lambda i, g: (i, wc + g)),
                  pl.BlockSpec((None, B_DCONV, GROUP_CH), lambda i, g: (0, 0, g)),
                  pl.BlockSpec((None, B_DCONV, B_DSTATE), lambda i, g: (0, 0, wb + g)),
                  pl.BlockSpec((None, B_DCONV, B_DSTATE), lambda i, g: (0, 0, wc + g)),
                  per_lane(GROUP_CH), per_lane(B_DSTATE, wb), per_lane(B_DSTATE, wc),
                  per_lane(LANES), per_lane(LANES), per_lane(GROUP_CH), per_lane(GROUP_CH),
                  hs_spec],
        out_specs=[pl.BlockSpec((rs, GROUP_CH), lambda i, g: (i, g)), hs_spec],
        out_shape=[SDS((T, B_DINNER), F32), SDS(hs.shape, F32)],
        compiler_params=_params(("arbitrary", "arbitrary")),
        name="ssd_sample",
    )(p, p, p, p, dt_cols, conv_state, conv_state, conv_state,
      w["b_conv_w"], w["b_conv_w"], w["b_conv_w"], w["b_conv_b"], w["b_conv_b"], w["b_conv_b"],
      w["dtb_cols"], w["alog_cols"], w["d_lane"], w["b_norm_w"], hs)


def _pad_lanes(a, width=LANES):
    return jnp.pad(a, ((0, 0), (0, width - a.shape[1])))


def _per_group_lanes(a):
    r = a.shape[0]
    a = a.reshape(r, B_GROUPS, HEADS_PER_GROUP)
    return jnp.pad(a, ((0, 0), (0, 0), (0, LANES - HEADS_PER_GROUP))).reshape(r, B_GROUPS * LANES)


def _to_sample_rows(a, seq_len):
    nseq, nt, ch = a.shape
    a = a.reshape(nseq // SEQ_GROUP, SEQ_GROUP, nt, ch).transpose(0, 2, 1, 3)
    a = jnp.pad(a, ((0, 0), (0, seq_len - nt), (0, 0), (0, 0)))
    return a.reshape(nseq * seq_len, ch)


def _from_sample_rows(a, nseq, seq_len, first_token=0):
    ch = a.shape[1]
    a = a.reshape(nseq // SEQ_GROUP, seq_len, SEQ_GROUP, ch)[:, first_token:]
    return a.transpose(0, 2, 1, 3).reshape(nseq, seq_len - first_token, ch)


def _forward(x, nseq, seq_len, state, w, *, prompt):
    T = x.shape[0]
    tm = 1024 if prompt else T
    a_main = 2 * A_HEADS * A_DK + 2 * A_HEADS * A_DV
    b_main = B_DINNER + B_CONV_DIM
    nstate = FFN_CONV - 1
    cast = {"ffn_bf16": [None, None]}

    outs = _norm_matmul(x, w["norm_mix"], 0, w["a_w_in"], a_main, w["a_w_gate"], tm=tm, tn=1024)
    p_a, gates = outs[:2]
    cast["a_w_in"] = outs[-1]
    if prompt:
        gates_t = gates[:, :2 * A_HEADS].T
        hg, m_c, m_n, m_m = _mlstm_prompt(p_a, gates, gates_t, w["a_b_row"], w["a_b_col"],
                                          nseq=nseq, seq_len=seq_len, c=256)
        m_m = m_m[:, :, 0, 0]
    else:
        c0, n0, m0 = state["mlstm"]
        m_rows = _pad_lanes(_to_sample_rows(jnp.broadcast_to(m0[:, None, :], (nseq, seq_len, A_HEADS)), seq_len))
        hg, m_c, m_n, m_rows_new = _mlstm_sample(p_a, gates, w["a_b_row"], m_rows, c0, n0[:, :, None, :],
                                                 seq_len=seq_len)
        m_m = _from_sample_rows(m_rows_new, nseq, seq_len, seq_len - 1)[:, 0, :A_HEADS]
    m_n = m_n[:, :, 0, :]
    outs = _matmul_res(hg, w["a_w_out"], x, tm=tm, tn=512)
    x, cast["a_w_out"] = outs[0], outs[-1]

    def ffn(x, layer, final_w=None):
        if prompt:
            ftm = 512
            x_new, tg, tv = _ffn(x, layer, w, tm=ftm, tf=512, seq_len=seq_len, final_w=final_w)
            tps = seq_len // ftm
            pick = lambda t: t.reshape(nseq, tps, SUBLANES, D_FF)[:, tps - 1, SUBLANES - nstate:]
            tails = jnp.concatenate([pick(tg), pick(tv)], axis=-1)
        else:
            st = state["ffn_conv"][layer].reshape(nseq, nstate * 2 * D_FF)
            outs = _ffn(x, layer, w, tm=T, tf=256, seq_len=seq_len, state=st, final_w=final_w)
            x_new = outs[0]
            tails = jnp.stack([jnp.concatenate([outs[1 + t], outs[1 + nstate + t]], axis=-1) for t in range(nstate)],
                              axis=1)
            cast["ffn_bf16"][layer] = tuple(outs[1 + 2 * nstate:])
        return x_new, tails

    x, f_conv0 = ffn(x, 0)

    outs = _norm_matmul(x, w["norm_mix"], 1, w["b_w_in"], b_main, w["b_w_dt"], tm=tm, tn=1024)
    p_b, dt_raw = outs[:2]
    cast["b_w_in"] = outs[-1]
    if prompt:
        s_conv = p_b.reshape(nseq, seq_len, b_main)[:, seq_len - (B_DCONV - 1):, B_DINNER:]
        dt_rows = dt_raw[:, :B_HEADS].T
        yn, s_t = _ssd_prompt(p_b, dt_raw, dt_rows, w, nseq=nseq, seq_len=seq_len, c=256)
        ssm = s_t.reshape(nseq, B_GROUPS, B_DSTATE, HEADS_PER_GROUP, B_HEADDIM)
        ssm = ssm.transpose(0, 1, 3, 4, 2).reshape(nseq, B_HEADS, B_HEADDIM, B_DSTATE)
    else:
        s_conv = _from_sample_rows(p_b, nseq, seq_len, seq_len - (B_DCONV - 1))[:, :, B_DINNER:]
        st = _to_sample_rows(state["ssm_conv"], seq_len)
        yn, ssm = _ssd_sample(p_b, dt_raw, st, w, state["ssm"], seq_len=seq_len)
    outs = _matmul_res(yn, w["b_w_out"], x, tm=tm, tn=512)
    x, cast["b_w_out"] = outs[0], outs[-1]

    y, f_conv1 = ffn(x, 1, final_w=w["norm_final"])
    return (y, m_c[None], m_n[None], m_m[None], ssm[None], s_conv[None], jnp.stack([f_conv0, f_conv1]), cast)


def kernel(x_prompt, x_sample, state_mlstm_C, state_mlstm_n, state_mlstm_m, state_ssm, state_ssm_conv,
           state_ffn_conv, norm_mix, norm_ffn, norm_final, a_w_in, a_b_if, a_w_out, b_w_in, b_conv_w,
           b_conv_b, b_dt_bias, b_A_log, b_D, b_norm_w, b_w_out, f_w_up, f_conv_w, f_conv_b, f_w_down):
    a_main = 2 * A_HEADS * A_DK + 2 * A_HEADS * A_DV
    b_main = B_DINNER + B_CONV_DIM
    w = {
        "norm_mix": norm_mix[:, None], "norm_ffn": norm_ffn[:, None], "norm_final": norm_final[None],
        "a_w_in": a_w_in,
        "a_w_gate": _pad_lanes(a_w_in[0, :, a_main:]).astype(BF16),
        "a_b_row": _pad_lanes(a_b_if), "a_b_col": a_b_if.T,
        "a_w_out": a_w_out,
        "b_w_in": b_w_in,
        "b_w_dt": _pad_lanes(b_w_in[0, :, b_main:]).astype(BF16),
        "b_conv_w": b_conv_w, "b_conv_b": b_conv_b,
        "dtb_cols": _per_group_lanes(b_dt_bias), "dtb_rows": b_dt_bias.T,
        "alog_cols": _per_group_lanes(b_A_log), "alog_rows": b_A_log.T,
        "d_lane": jnp.repeat(b_D, B_HEADDIM, axis=1), "b_norm_w": b_norm_w,
        "b_w_out": b_w_out,
        "f_w_up": f_w_up, "f_conv_w": f_conv_w, "f_conv_b": f_conv_b[:, None],
        "f_w_down": f_w_down,
    }
    bp, lp, _ = x_prompt.shape
    bs, ls, _ = x_sample.shape
    state = {"mlstm": (state_mlstm_C[0], state_mlstm_n[0], state_mlstm_m[0]),
             "ssm": state_ssm[0], "ssm_conv": state_ssm_conv[0], "ffn_conv": state_ffn_conv}
    ys, sc, sn, sm, sssm, ssconv, sfconv, w_bf16 = _forward(_to_sample_rows(x_sample, ls), bs, ls, state, w,
                                                            prompt=False)
    yp, pc, pn, pm, pssm, psconv, pfconv, _ = _forward(x_prompt.reshape(bp * lp, D_MODEL), bp, lp, None,
                                                       {**w, **w_bf16}, prompt=True)
    return (yp.reshape(bp, lp, D_MODEL), _from_sample_rows(ys, bs, ls), pc, sc, pn, sn, pm, sm,
            pssm, sssm, psconv, ssconv, pfconv, sfconv)
```

```python
import functools

import jax
import jax.numpy as jnp
from jax import lax
from jax.experimental import pallas as pl
from jax.experimental.pallas import tpu as pltpu

F32 = jnp.float32
BF16 = jnp.bfloat16
SDS = jax.ShapeDtypeStruct

D_MODEL = 2048
A_HEADS = 8
A_DK = 128
A_DV = 256
B_DINNER = 4096
B_HEADDIM = 64
B_HEADS = 64
B_GROUPS = 8
B_DSTATE = 128
B_DCONV = 4
B_CONV_DIM = 6144
D_FF = 5632
FFN_CONV = 3
EPS = 1e-6
NEG_INF = float("-inf")
LOG2E = 1.4426950408889634

LANES = 128
SUBLANES = 8
VMEM_LIMIT_BYTES = 56 * 1024 * 1024

HEADS_PER_GROUP = B_HEADS // B_GROUPS
GROUP_CH = HEADS_PER_GROUP * B_HEADDIM
SEQ_GROUP = SUBLANES


def _params(sem, flags=None):
    return pltpu.CompilerParams(dimension_semantics=sem, vmem_limit_bytes=VMEM_LIMIT_BYTES, flags=flags)


def _sigmoid(x):
    return 1.0 / (1.0 + jnp.exp(-x))


def _silu(x):
    return x * _sigmoid(x)


def _softplus(x):
    return jnp.maximum(x, 0.0) + jnp.log1p(jnp.exp(-jnp.abs(x)))


def _log_sigmoid(x):
    return -_softplus(-x)


def _rms(x, w):
    ms = jnp.mean(x * x, axis=-1, keepdims=True)
    return x * lax.rsqrt(ms + EPS) * w


def _cumsum_rows(x):
    n = x.shape[0]
    row = lax.broadcasted_iota(jnp.int32, x.shape, 0)
    k = 1
    while k < n:
        x = x + jnp.where(row >= k, pltpu.roll(x, k, 0), 0.0)
        k *= 2
    return x


def _cumsum_lanes(x):
    n = x.shape[1]
    col = lax.broadcasted_iota(jnp.int32, x.shape, 1)
    k = 1
    while k < n:
        x = x + jnp.where(col >= k, pltpu.roll(x, k, 1), 0.0)
        k *= 2
    return x


def _dot(a, b):
    return jnp.dot(a, b, preferred_element_type=F32)


def _dot_nt(a, b):
    return lax.dot_general(a, b, (((1,), (1,)), ((), ())), preferred_element_type=F32)


def _dot_tn(a, b):
    return lax.dot_general(a, b, (((0,), (0,)), ((), ())), preferred_element_type=F32)


def _conv_taps(u, shifted, w_ref, b_ref, ntaps):
    y = u * w_ref[ntaps - 1:ntaps, :]
    for k in range(1, ntaps):
        y = y + shifted(k) * w_ref[ntaps - 1 - k:ntaps - k, :]
    return y + b_ref[...]


def _shift_with_prev(u, prev8, k):
    ext = jnp.concatenate([prev8, u], axis=0)
    return pltpu.roll(ext, k, 0)[SUBLANES:]


def _pad_rows(a, rows):
    if a.shape[0] == rows:
        return a
    return jnp.concatenate([a, jnp.zeros((rows - a.shape[0], a.shape[1]), a.dtype)], axis=0)


class _SampleRows:
    def __init__(self, rows, seq_len):
        self.rows = rows
        self.seq_len = seq_len
        r = lax.broadcasted_iota(jnp.int32, (rows, 1), 0)
        self.token = (r // SEQ_GROUP) % seq_len
        self.seq = r % SEQ_GROUP

    def back(self, a, k):
        if k == 0:
            return a
        return jnp.where(self.token >= k, pltpu.roll(a, SEQ_GROUP * k, 0), 0.0)

    def ahead(self, a, k):
        if k == 0:
            return a
        return jnp.where(self.token + k < self.seq_len, pltpu.roll(a, self.rows - SEQ_GROUP * k, 0), 0.0)

    def conv_shift(self, u, state, ntaps):
        return lambda k: self.back(u, k) + self.ahead(state, ntaps - 1 - k)


def _tokens_to_rows(slots, seq_len):
    nseq, ch = slots[0].shape
    pad = jnp.zeros(((seq_len - len(slots)) * SEQ_GROUP, ch), slots[0].dtype)
    parts = []
    for g in range(nseq // SEQ_GROUP):
        parts += [s[g * SEQ_GROUP:(g + 1) * SEQ_GROUP] for s in slots] + [pad]
    return jnp.concatenate(parts, axis=0)


def _rows_of_token(a, seq_len, t):
    step = seq_len * SEQ_GROUP
    return jnp.concatenate([a[g * step + t * SEQ_GROUP:g * step + (t + 1) * SEQ_GROUP]
                            for g in range(a.shape[0] // step)], axis=0)


def _norm_matmul_kernel(x_ref, nw_ref, w_ref, ws_ref, o_ref, os_ref, *rest):
    hn_ref = rest[-1]

    @pl.when(pl.program_id(1) == 0)
    def _():
        hn = _rms(x_ref[...], nw_ref[...]).astype(BF16)
        hn_ref[...] = hn
        os_ref[...] = _dot(hn, ws_ref[...])

    wb = w_ref[...].astype(BF16)
    if len(rest) == 2:
        rest[0][...] = wb
    o_ref[...] = _dot_nt(hn_ref[...], wb)


def _norm_matmul(x, nw, layer, w_t, n_main, ws, *, tm, tn):
    T, D = x.shape
    emit = w_t.dtype != BF16
    w_spec = pl.BlockSpec((None, tn, D), lambda i, j: (0, j, 0))
    out_specs = [pl.BlockSpec((tm, tn), lambda i, j: (i, j)), pl.BlockSpec((tm, LANES), lambda i, j: (i, 0))]
    out_shape = [SDS((T, n_main), F32), SDS((T, LANES), F32)]
    if emit:
        assert T == tm, "the bfloat16 copy is written once, by a single row tile"
        out_specs.append(w_spec)
        out_shape.append(SDS((1, n_main, D), BF16))
    return pl.pallas_call(
        _norm_matmul_kernel,
        grid=(T // tm, n_main // tn),
        in_specs=[pl.BlockSpec((tm, D), lambda i, j: (i, 0)),
                  pl.BlockSpec((None, 1, D), lambda i, j: (layer, 0, 0)),
                  w_spec,
                  pl.BlockSpec((D, LANES), lambda i, j: (0, 0))],
        out_specs=out_specs,
        out_shape=out_shape,
        scratch_shapes=[pltpu.VMEM((tm, D), BF16)],
        compiler_params=_params(("arbitrary", "arbitrary")),
        name="norm_matmul",
    )(x, nw, w_t, ws)


def _matmul_res_kernel(a_ref, w_ref, r_ref, o_ref, *rest):
    wb = w_ref[...].astype(BF16)
    if rest:
        rest[0][...] = wb
    o_ref[...] = r_ref[...] + _dot(a_ref[...].astype(BF16), wb)


def _matmul_res(a, w, res, *, tm, tn):
    T, K = a.shape
    N = w.shape[2]
    emit = w.dtype != BF16
    w_spec = pl.BlockSpec((None, K, tn), lambda i, j: (0, 0, j))
    out_specs = [pl.BlockSpec((tm, tn), lambda i, j: (i, j))]
    out_shape = [SDS((T, N), F32)]
    if emit:
        assert T == tm, "the bfloat16 copy is written once, by a single row tile"
        out_specs.append(w_spec)
        out_shape.append(SDS((1, K, N), BF16))
    return pl.pallas_call(
        _matmul_res_kernel,
        grid=(T // tm, N // tn),
        in_specs=[pl.BlockSpec((tm, K), lambda i, j: (i, 0)),
                  w_spec,
                  pl.BlockSpec((tm, tn), lambda i, j: (i, j))],
        out_specs=out_specs,
        out_shape=out_shape,
        compiler_params=_params(("arbitrary", "arbitrary")),
        name="matmul_res",
    )(a, w, res)


def _ffn_sample_kernel(*refs, tm, seq_len, final_norm):
    refs = list(refs)
    x_ref, nw_ref, wug_ref, wuv_ref, cwg_ref, cwv_ref, cbg_ref, cbv_ref, wd_ref = refs[:9]
    del refs[:9]
    nstate = FFN_CONV - 1
    sg_refs, sv_refs = refs[:nstate], refs[nstate:2 * nstate]
    del refs[:2 * nstate]
    fw_ref = refs.pop(0) if final_norm else None
    out_ref = refs.pop(0)
    tg_refs, tv_refs = refs[:nstate], refs[nstate:2 * nstate]
    wbg_ref, wbv_ref, wbd_ref = refs[2 * nstate:2 * nstate + 3]
    hn_s, acc_s = refs[2 * nstate + 3:]
    j = pl.program_id(1)

    @pl.when(j == 0)
    def _():
        hn_s[...] = _rms(x_ref[...], nw_ref[...]).astype(BF16)
        acc_s[...] = jnp.zeros_like(acc_s)

    wug = wug_ref[...].astype(BF16)
    wuv = wuv_ref[...].astype(BF16)
    wd = wd_ref[...].astype(BF16)
    hn = hn_s[...]
    ug = _dot(hn, wug)
    uv = _dot(hn, wuv)

    wbg_ref[...] = wug
    wbv_ref[...] = wuv
    wbd_ref[...] = wd
    for t in range(nstate):
        tg_refs[t][...] = _rows_of_token(ug, seq_len, seq_len - nstate + t)
        tv_refs[t][...] = _rows_of_token(uv, seq_len, seq_len - nstate + t)
    sr = _SampleRows(tm, seq_len)
    sg = _tokens_to_rows([r[...] for r in sg_refs], seq_len)
    sv = _tokens_to_rows([r[...] for r in sv_refs], seq_len)
    yg = _conv_taps(ug, sr.conv_shift(ug, sg, FFN_CONV), cwg_ref, cbg_ref, FFN_CONV)
    yv = _conv_taps(uv, sr.conv_shift(uv, sv, FFN_CONV), cwv_ref, cbv_ref, FFN_CONV)

    act = (_silu(yg) * yv).astype(BF16)
    acc_s[...] += _dot(act, wd)

    @pl.when(j == pl.num_programs(1) - 1)
    def _():
        y = x_ref[...] + acc_s[...]
        out_ref[...] = _rms(y, fw_ref[...]) if final_norm else y


def _ffn_sample(x, layer, w, state, *, tf, seq_len, final_w=None):
    T, D = x.shape
    nj = D_FF // tf
    nstate = FFN_CONV - 1
    nseq = T // seq_len
    in_specs = [pl.BlockSpec((T, D), lambda i, j: (0, 0)),
                pl.BlockSpec((None, 1, D), lambda i, j: (layer, 0, 0)),
                pl.BlockSpec((None, D, tf), lambda i, j: (layer, 0, j)),
                pl.BlockSpec((None, D, tf), lambda i, j: (layer, 0, nj + j)),
                pl.BlockSpec((None, FFN_CONV, tf), lambda i, j: (layer, 0, j)),
                pl.BlockSpec((None, FFN_CONV, tf), lambda i, j: (layer, 0, nj + j)),
                pl.BlockSpec((None, 1, tf), lambda i, j: (layer, 0, j)),
                pl.BlockSpec((None, 1, tf), lambda i, j: (layer, 0, nj + j)),
                pl.BlockSpec((None, tf, D), lambda i, j: (layer, j, 0))]
    args = [x, w["norm_ffn"], w["f_w_up"], w["f_w_up"], w["f_conv_w"], w["f_conv_w"], w["f_conv_b"], w["f_conv_b"],
            w["f_w_down"]]
    in_specs += [pl.BlockSpec((nseq, tf), lambda i, j, t=t: (0, t * 2 * nj + j)) for t in range(nstate)]
    in_specs += [pl.BlockSpec((nseq, tf), lambda i, j, t=t: (0, t * 2 * nj + nj + j)) for t in range(nstate)]
    args += [state] * (2 * nstate)
    if final_w is not None:
        in_specs.append(pl.BlockSpec((1, D), lambda i, j: (0, 0)))
        args.append(final_w)
    out_specs = [pl.BlockSpec((T, D), lambda i, j: (0, 0))]
    out_specs += [pl.BlockSpec((nseq, tf), lambda i, j: (0, j))] * (2 * nstate)
    out_specs += [pl.BlockSpec((D, tf), lambda i, j: (0, j))] * 2 + [pl.BlockSpec((tf, D), lambda i, j: (j, 0))]
    out_shape = [SDS((T, D), F32)] + [SDS((nseq, D_FF), F32)] * (2 * nstate)
    out_shape += [SDS((D, D_FF), BF16)] * 2 + [SDS((D_FF, D), BF16)]
    return pl.pallas_call(
        functools.partial(_ffn_sample_kernel, tm=T, seq_len=seq_len, final_norm=final_w is not None),
        grid=(1, nj),
        in_specs=in_specs,
        out_specs=out_specs,
        out_shape=out_shape,
        scratch_shapes=[pltpu.VMEM((T, D), BF16), pltpu.VMEM((T, D), F32)],
        compiler_params=_params(("arbitrary", "arbitrary")),
        name="ffn_sample",
    )(*args)


def _ffn_prompt_kernel(x_ref, nw_ref, wug_ref, wuv_ref, cwg_ref, cwv_ref, cbg_ref, cbv_ref, wd_ref, *rest,
                       tm, tiles_per_seq, nj, final_norm):
    rest = list(rest)
    fw_ref = rest.pop(0) if final_norm else None
    out_ref, tg_ref, tv_ref, hn_s, acc_s, cg_s, cv_s = rest[:7]
    act_slots = rest[7:9]
    i = pl.program_id(0)
    j = pl.program_id(1)

    def up_act(slot):
        hn = hn_s[...]
        ug = _dot(hn, wug_ref[...])
        uv = _dot(hn, wuv_ref[...])
        tail_g = ug[tm - SUBLANES:, :]
        tail_v = uv[tm - SUBLANES:, :]
        tg_ref[0] = tail_g
        tv_ref[0] = tail_v
        prev_g = cg_s[j]
        prev_v = cv_s[j]
        yg = _conv_taps(ug, lambda k: _shift_with_prev(ug, prev_g, k), cwg_ref, cbg_ref, FFN_CONV)
        yv = _conv_taps(uv, lambda k: _shift_with_prev(uv, prev_v, k), cwv_ref, cbv_ref, FFN_CONV)
        cg_s[j] = tail_g
        cv_s[j] = tail_v
        act_slots[slot][...] = (_silu(yg) * yv).astype(BF16)

    def down_proj(slot):
        acc_s[...] += _dot(act_slots[slot][...], wd_ref[...])

    @pl.when(j == 0)
    def _():
        hn_s[...] = _rms(x_ref[...], nw_ref[...]).astype(BF16)
        acc_s[...] = jnp.zeros_like(acc_s)

        @pl.when(i % tiles_per_seq == 0)
        def _():
            cg_s[...] = jnp.zeros_like(cg_s)
            cv_s[...] = jnp.zeros_like(cv_s)

        up_act(0)

    steady = (j >= 1) & (j < nj)
    for parity in (0, 1):
        @pl.when(steady & (j % 2 == parity))
        def _():
            up_act(parity)
            down_proj(1 - parity)

    @pl.when(j == nj)
    def _():
        down_proj((nj - 1) % 2)
        y = x_ref[...] + acc_s[...]
        out_ref[...] = _rms(y, fw_ref[...]) if final_norm else y


def _ffn_prompt(x, layer, w, *, tm, tf, seq_len, final_w=None):
    T, D = x.shape
    nj = D_FF // tf
    ni = T // tm
    assert nj >= 2 and seq_len % tm == 0
    act_j = lambda j: jnp.minimum(j, nj - 1)
    down_j = lambda j: jnp.maximum(j - 1, 0)
    wug, wuv, wd = w["ffn_bf16"][layer]
    in_specs = [pl.BlockSpec((tm, D), lambda i, j: (i, 0)),
                pl.BlockSpec((None, 1, D), lambda i, j: (layer, 0, 0)),
                pl.BlockSpec((D, tf), lambda i, j: (0, act_j(j))),
                pl.BlockSpec((D, tf), lambda i, j: (0, act_j(j))),
                pl.BlockSpec((None, FFN_CONV, tf), lambda i, j: (layer, 0, act_j(j))),
                pl.BlockSpec((None, FFN_CONV, tf), lambda i, j: (layer, 0, nj + act_j(j))),
                pl.BlockSpec((None, 1, tf), lambda i, j: (layer, 0, act_j(j))),
                pl.BlockSpec((None, 1, tf), lambda i, j: (layer, 0, nj + act_j(j))),
                pl.BlockSpec((tf, D), lambda i, j: (down_j(j), 0))]
    args = [x, w["norm_ffn"], wug, wuv, w["f_conv_w"], w["f_conv_w"], w["f_conv_b"], w["f_conv_b"], wd]
    if final_w is not None:
        in_specs.append(pl.BlockSpec((1, D), lambda i, j: (0, 0)))
        args.append(final_w)
    tail_spec = pl.BlockSpec((1, SUBLANES, tf), lambda i, j: (i, 0, act_j(j)))
    return pl.pallas_call(
        functools.partial(_ffn_prompt_kernel, tm=tm, tiles_per_seq=seq_len // tm, nj=nj,
                          final_norm=final_w is not None),
        grid=(ni, nj + 1),
        in_specs=in_specs,
        out_specs=[pl.BlockSpec((tm, D), lambda i, j: (i, 0)), tail_spec, tail_spec],
        out_shape=[SDS((T, D), F32)] + [SDS((ni, SUBLANES, D_FF), F32)] * 2,
        scratch_shapes=[pltpu.VMEM((tm, D), BF16), pltpu.VMEM((tm, D), F32)]
        + [pltpu.VMEM((nj, SUBLANES, tf), F32)] * 2
        + [pltpu.VMEM((tm, tf), BF16)] * 2,
        compiler_params=_params(("arbitrary", "arbitrary")),
        name="ffn_prompt",
    )(*args)


def _mlstm_intra(q, ks, v, ig_col, lf_cum_col, lf_cum_row, ig_row, m_prev_col, causal):
    log_d = jnp.where(causal, lf_cum_col - lf_cum_row + ig_row, NEG_INF)
    inter = lf_cum_col + m_prev_col
    m_t = jnp.maximum(inter, jnp.max(log_d, axis=1, keepdims=True))
    d_m = jnp.exp(log_d - m_t)
    w_inter = jnp.exp(inter - m_t)
    kb = _pad_rows(ks.astype(BF16), causal.shape[1])
    vb = _pad_rows(v.astype(BF16), causal.shape[1])
    s = _dot_nt(q.astype(BF16), kb) * d_m
    num = _dot(s.astype(BF16), vb)
    den = jnp.sum(s, axis=1, keepdims=True)
    return m_t, w_inter, inter, num, den


def _mlstm_prompt_kernel(q_ref, k_ref, v_ref, o_ref, g_ref, gt_ref, br_ref, bc_ref,
                         hg_ref, co_ref, no_ref, mo_ref, c_s, n_s, m_s, *, c):
    ci = pl.program_id(1)

    @pl.when(ci == 0)
    def _():
        c_s[...] = jnp.zeros_like(c_s)
        n_s[...] = jnp.zeros_like(n_s)
        m_s[...] = jnp.zeros_like(m_s)

    g = g_ref[...] + br_ref[...]
    lf_cum = _cumsum_rows(_log_sigmoid(g))
    gt = gt_ref[...] + bc_ref[...]
    lf_cum_t = _cumsum_lanes(_log_sigmoid(gt))
    row = lax.broadcasted_iota(jnp.int32, (c, c), 0)
    col = lax.broadcasted_iota(jnp.int32, (c, c), 1)
    causal = col <= row
    last = ci == pl.num_programs(1) - 1

    for h in range(A_HEADS):
        ig_col = g[:, h:h + 1]
        b_col = lf_cum[:, A_HEADS + h:A_HEADS + h + 1]
        ig_row = gt[h:h + 1, :]
        b_row = lf_cum_t[A_HEADS + h:A_HEADS + h + 1, :]
        q = q_ref[:, h * A_DK:(h + 1) * A_DK]
        ks = k_ref[:, h * A_DK:(h + 1) * A_DK] * (A_DK ** -0.5)
        v = v_ref[:, h * A_DV:(h + 1) * A_DV]
        m_prev = m_s[h][:, 0:1]
        m_t, w_inter, inter, num, den = _mlstm_intra(q, ks, v, ig_col, b_col, b_row, ig_row, m_prev, causal)
        cmat = c_s[h]
        nvec = n_s[h]
        num = w_inter * _dot(q.astype(BF16), cmat.astype(BF16)) + num
        den = w_inter * jnp.sum(q * nvec, axis=1, keepdims=True) + den
        hh = num / jnp.maximum(jnp.abs(den), jnp.exp(-m_t))
        hg_ref[:, h * A_DV:(h + 1) * A_DV] = (_sigmoid(o_ref[:, h * A_DV:(h + 1) * A_DV]) * hh).astype(BF16)

        m_new = m_t[c - 1:c, :]
        w_state = jnp.exp(b_col[c - 1:c, :] - b_col + ig_col - m_new)
        decay = jnp.exp(inter[c - 1:c, :] - m_new)
        kw = ks * w_state
        c_new = decay * cmat + _dot_tn(kw.astype(BF16), v.astype(BF16))
        n_new = decay * nvec + jnp.sum(kw, axis=0, keepdims=True)
        m_row = jnp.broadcast_to(m_new, (1, LANES))
        c_s[h] = c_new
        n_s[h] = n_new
        m_s[h] = m_row

        @pl.when(last)
        def _():
            co_ref[0, h] = c_new
            no_ref[0, h] = n_new
            mo_ref[0, h] = m_row


def _mlstm_prompt(p, gates, gates_t, bias_row, bias_col, *, nseq, seq_len, c):
    T = p.shape[0]
    nc = seq_len // c
    hk = A_HEADS * A_DK
    hv = A_HEADS * A_DV
    rows = lambda b, ci: b * nc + ci
    return pl.pallas_call(
        functools.partial(_mlstm_prompt_kernel, c=c),
        grid=(nseq, nc),
        in_specs=[pl.BlockSpec((c, hk), lambda b, ci: (rows(b, ci), 0)),
                  pl.BlockSpec((c, hk), lambda b, ci: (rows(b, ci), 1)),
                  pl.BlockSpec((c, hv), lambda b, ci: (rows(b, ci), 1)),
                  pl.BlockSpec((c, hv), lambda b, ci: (rows(b, ci), 2)),
                  pl.BlockSpec((c, LANES), lambda b, ci: (rows(b, ci), 0)),
                  pl.BlockSpec((2 * A_HEADS, c), lambda b, ci: (0, rows(b, ci))),
                  pl.BlockSpec((1, LANES), lambda b, ci: (0, 0)),
                  pl.BlockSpec((2 * A_HEADS, 1), lambda b, ci: (0, 0))],
        out_specs=[pl.BlockSpec((c, hv), lambda b, ci: (rows(b, ci), 0)),
                   pl.BlockSpec((1, A_HEADS, A_DK, A_DV), lambda b, ci: (b, 0, 0, 0)),
                   pl.BlockSpec((1, A_HEADS, 1, A_DK), lambda b, ci: (b, 0, 0, 0)),
                   pl.BlockSpec((1, A_HEADS, 1, LANES), lambda b, ci: (b, 0, 0, 0))],
        out_shape=[SDS((T, hv), BF16),
                   SDS((nseq, A_HEADS, A_DK, A_DV), F32),
                   SDS((nseq, A_HEADS, 1, A_DK), F32),
                   SDS((nseq, A_HEADS, 1, LANES), F32)],
        scratch_shapes=[pltpu.VMEM((A_HEADS, A_DK, A_DV), F32), pltpu.VMEM((A_HEADS, 1, A_DK), F32),
                        pltpu.VMEM((A_HEADS, 1, LANES), F32)],
        compiler_params=_params(("arbitrary", "arbitrary")),
        name="mlstm_prompt",
    )(p, p, p, p, gates, gates_t, bias_row, bias_col)


def _mlstm_sample_kernel(q_ref, k_ref, v_ref, o_ref, g_ref, br_ref, mp_ref, ci_ref, ni_ref,
                         hg_ref, co_ref, no_ref, mo_ref, *, seq_len):
    nbs = SEQ_GROUP
    rs = nbs * seq_len
    g = g_ref[...] + br_ref[...]
    lf_all = _log_sigmoid(g)
    m_prev_all = mp_ref[...]
    row = lax.broadcasted_iota(jnp.int32, (rs, LANES), 0)
    col = lax.broadcasted_iota(jnp.int32, (rs, LANES), 1)
    same = ((row % nbs) == (col % nbs)) & (col < rs)
    causal = same & (col <= row)
    causal_t = same & (row <= col)
    eye = row == col
    last = same & (col // nbs == seq_len - 1)
    seq_of_row = lax.broadcasted_iota(jnp.int32, (rs, 1), 0) % nbs
    prow = lax.broadcasted_iota(jnp.int32, (LANES, 1), 0)
    seq_of_prow = jnp.where(prow < rs, prow % nbs, -1)

    def to_row(colvec):
        return jnp.sum(jnp.where(eye, colvec, 0.0), axis=0, keepdims=True)

    def at_last(rowvec):
        return jnp.sum(jnp.where(last, rowvec, 0.0), axis=1, keepdims=True)

    m_out = jnp.zeros((rs, LANES), F32)
    for h in range(A_HEADS):
        ig_col = g[:, h:h + 1]
        lf_col = lf_all[:, A_HEADS + h:A_HEADS + h + 1]
        m_prev = m_prev_all[:, h:h + 1]
        ig_row = to_row(ig_col)
        lf_row = to_row(lf_col)
        b_col = jnp.sum(jnp.where(causal, lf_row, 0.0), axis=1, keepdims=True)
        b_row = jnp.sum(jnp.where(causal_t, lf_col, 0.0), axis=0, keepdims=True)
        q = q_ref[:, h * A_DK:(h + 1) * A_DK]
        ks = k_ref[:, h * A_DK:(h + 1) * A_DK] * (A_DK ** -0.5)
        v = v_ref[:, h * A_DV:(h + 1) * A_DV]
        m_t, w_inter, inter, num, den = _mlstm_intra(q, ks, v, ig_col, b_col, b_row, ig_row, m_prev, causal)
        qb = q.astype(BF16)
        num_i = jnp.zeros((rs, A_DV), F32)
        qn = jnp.zeros((rs, 1), F32)
        for bb in range(nbs):
            mine = seq_of_row == bb
            num_i = jnp.where(mine, _dot(qb, ci_ref[bb, h].astype(BF16)), num_i)
            qn = jnp.where(mine, jnp.sum(q * ni_ref[bb, h], axis=1, keepdims=True), qn)
        num = w_inter * num_i + num
        den = w_inter * qn + den
        hh = num / jnp.maximum(jnp.abs(den), jnp.exp(-m_t))
        hg_ref[:, h * A_DV:(h + 1) * A_DV] = _sigmoid(o_ref[:, h * A_DV:(h + 1) * A_DV]) * hh

        m_new = at_last(to_row(m_t))
        w_state = jnp.exp(at_last(b_row) - b_col + ig_col - m_new)
        decay = jnp.exp(at_last(to_row(inter)) - m_new)
        kw = ks * w_state
        kw_t = _pad_rows(kw, LANES).T.astype(BF16)
        v_pad = _pad_rows(v, LANES)
        for bb in range(nbs):
            v_mine = jnp.where(seq_of_prow == bb, v_pad, 0.0).astype(BF16)
            dec = decay[bb:bb + 1, :]
            co_ref[bb, h] = dec * ci_ref[bb, h] + _dot(kw_t, v_mine)
            no_ref[bb, h] = dec * ni_ref[bb, h] + jnp.sum(jnp.where(seq_of_row == bb, kw, 0.0), axis=0, keepdims=True)
        m_out = jnp.where(col == h, m_new, m_out)
    mo_ref[...] = m_out


def _mlstm_sample(p, gates, bias_row, m_rows, c_in, n_in, *, seq_len):
    T = p.shape[0]
    nbs = SEQ_GROUP
    rs = nbs * seq_len
    nb = c_in.shape[0]
    hk = A_HEADS * A_DK
    hv = A_HEADS * A_DV
    return pl.pallas_call(
        functools.partial(_mlstm_sample_kernel, seq_len=seq_len),
        grid=(nb // nbs,),
        in_specs=[pl.BlockSpec((rs, hk), lambda i: (i, 0)),
                  pl.BlockSpec((rs, hk), lambda i: (i, 1)),
                  pl.BlockSpec((rs, hv), lambda i: (i, 1)),
                  pl.BlockSpec((rs, hv), lambda i: (i, 2)),
                  pl.BlockSpec((rs, LANES), lambda i: (i, 0)),
                  pl.BlockSpec((1, LANES), lambda i: (0, 0)),
                  pl.BlockSpec((rs, LANES), lambda i: (i, 0)),
                  pl.BlockSpec((nbs, A_HEADS, A_DK, A_DV), lambda i: (i, 0, 0, 0)),
                  pl.BlockSpec((nbs, A_HEADS, 1, A_DK), lambda i: (i, 0, 0, 0))],
        out_specs=[pl.BlockSpec((rs, hv), lambda i: (i, 0)),
                   pl.BlockSpec((nbs, A_HEADS, A_DK, A_DV), lambda i: (i, 0, 0, 0)),
                   pl.BlockSpec((nbs, A_HEADS, 1, A_DK), lambda i: (i, 0, 0, 0)),
                   pl.BlockSpec((rs, LANES), lambda i: (i, 0))],
        out_shape=[SDS((T, hv), F32),
                   SDS((nb, A_HEADS, A_DK, A_DV), F32),
                   SDS((nb, A_HEADS, 1, A_DK), F32),
                   SDS((T, LANES), F32)],
        compiler_params=_params(("arbitrary",)),
        name="mlstm_sample",
    )(p, p, p, p, gates, bias_row, m_rows, c_in, n_in)


def _pair_select(a, h0):
    lane = lax.broadcasted_iota(jnp.int32, (a.shape[0], LANES), 1)
    return jnp.where(lane < B_HEADDIM, a[:, h0:h0 + 1], a[:, h0 + 1:h0 + 2])


def _group_heads_first(a, g):
    return pltpu.roll(a, (LANES - HEADS_PER_GROUP * g) % LANES, 1)


def _group_norm_gate(y, z, nw):
    gz = y * _silu(z)
    return gz * lax.rsqrt(jnp.mean(gz * gz, axis=-1, keepdims=True) + EPS) * nw


def _ssd_prompt_kernel(x_ref, b_ref, c_ref, z_ref, dtc_ref, dtr_ref,
                       cwx_ref, cwb_ref, cwc_ref, cbx_ref, cbb_ref, cbc_ref,
                       dbr_ref, dbc_ref, alr_ref, alc_ref, d_ref, nw_ref,
                       y_ref, so_ref, s_s, xp_s, bp_s, cp_s, *, c):
    ci = pl.program_id(2)

    @pl.when(ci == 0)
    def _():
        s_s[...] = jnp.zeros_like(s_s)
        xp_s[...] = jnp.zeros_like(xp_s)
        bp_s[...] = jnp.zeros_like(bp_s)
        cp_s[...] = jnp.zeros_like(cp_s)

    def conv(raw_ref, prev_s, cw_ref, cb_ref):
        raw = raw_ref[...]
        prev = prev_s[...]
        out = _conv_taps(raw, lambda k: _shift_with_prev(raw, prev, k), cw_ref, cb_ref, B_DCONV)
        prev_s[...] = raw[c - SUBLANES:, :]
        return _silu(out)

    x = conv(x_ref, xp_s, cwx_ref, cbx_ref)
    bm = conv(b_ref, bp_s, cwb_ref, cbb_ref)
    cm = conv(c_ref, cp_s, cwc_ref, cbc_ref)

    dt_c = _softplus(_group_heads_first(dtc_ref[...], pl.program_id(1)) + dbr_ref[...])
    ac_c = _cumsum_rows(dt_c * (-jnp.exp(alr_ref[...])))
    dt_r = _softplus(dtr_ref[...] + dbc_ref[...])
    ac_r = _cumsum_lanes(dt_r * (-jnp.exp(alc_ref[...])))
    ac_c2 = ac_c * LOG2E
    ac_r2 = ac_r * LOG2E

    row = lax.broadcasted_iota(jnp.int32, (c, c), 0)
    col = lax.broadcasted_iota(jnp.int32, (c, c), 1)
    bmb = bm.astype(BF16)
    cmb = cm.astype(BF16)
    cb = jnp.where(col <= row, _dot_nt(cmb, bmb), 0.0)
    state = s_s[...]
    y_inter = _dot(cmb, state.astype(BF16))
    ac_last = ac_c[c - 1:c, :]
    to_end = jnp.exp(ac_last - ac_c) * dt_c
    e_ac = jnp.exp(ac_c)
    e_last = jnp.exp(ac_last)
    lane = lax.broadcasted_iota(jnp.int32, (c, LANES), 1)
    d_all = d_ref[...]

    ys = []
    for pr in range(HEADS_PER_GROUP // 2):
        h0 = 2 * pr
        sl = slice(pr * LANES, (pr + 1) * LANES)
        xp = x[:, sl]
        xpb = xp.astype(BF16)
        full = []
        for r in (h0, h0 + 1):
            decay = jnp.exp2(jnp.minimum(ac_c2[:, r:r + 1] - ac_r2[r:r + 1, :], 0.0))
            wts = cb * decay * dt_r[r:r + 1, :]
            full.append(_dot(wts.astype(BF16), xpb))
        yp = jnp.where(lane < B_HEADDIM, full[0], full[1])
        yp = yp + y_inter[:, sl] * _pair_select(e_ac, h0) + d_all[:, sl] * xp
        ys.append(yp)
        xw = (xp * _pair_select(to_end, h0)).astype(BF16)
        s_s[:, sl] = _pair_select(e_last, h0) * state[:, sl] + _dot_tn(bmb, xw)
    y = jnp.concatenate(ys, axis=1)
    y_ref[...] = _group_norm_gate(y, z_ref[...], nw_ref[...]).astype(BF16)

    @pl.when(ci == pl.num_programs(2) - 1)
    def _():
        so_ref[0, 0] = s_s[...]


def _ssd_prompt(p, dt_cols, dt_rows, w, *, nseq, seq_len, c):
    T = p.shape[0]
    nc = seq_len // c
    rows = lambda b, g, ci: b * nc + ci
    kx = B_DINNER // GROUP_CH
    kb = (2 * B_DINNER) // B_DSTATE
    kc = kb + B_GROUPS
    wb = B_DINNER // B_DSTATE
    wc = wb + B_GROUPS
    G = B_GROUPS
    per_lane = lambda width, start=0: pl.BlockSpec((1, width), lambda b, g, ci: (0, start + g))
    per_row = pl.BlockSpec((HEADS_PER_GROUP, 1), lambda b, g, ci: (g, 0))
    return pl.pallas_call(
        functools.partial(_ssd_prompt_kernel, c=c),
        grid=(nseq, G, nc),
        in_specs=[pl.BlockSpec((c, GROUP_CH), lambda b, g, ci: (rows(b, g, ci), kx + g)),
                  pl.BlockSpec((c, B_DSTATE), lambda b, g, ci: (rows(b, g, ci), kb + g)),
                  pl.BlockSpec((c, B_DSTATE), lambda b, g, ci: (rows(b, g, ci), kc + g)),
                  pl.BlockSpec((c, GROUP_CH), lambda b, g, ci: (rows(b, g, ci), g)),
                  pl.BlockSpec((c, LANES), lambda b, g, ci: (rows(b, g, ci), 0)),
                  pl.BlockSpec((HEADS_PER_GROUP, c), lambda b, g, ci: (g, rows(b, g, ci))),
                  pl.BlockSpec((None, B_DCONV, GROUP_CH), lambda b, g, ci: (0, 0, g)),
                  pl.BlockSpec((None, B_DCONV, B_DSTATE), lambda b, g, ci: (0, 0, wb + g)),
                  pl.BlockSpec((None, B_DCONV, B_DSTATE), lambda b, g, ci: (0, 0, wc + g)),
                  per_lane(GROUP_CH), per_lane(B_DSTATE, wb), per_lane(B_DSTATE, wc),
                  per_lane(LANES), per_row, per_lane(LANES), per_row,
                  per_lane(GROUP_CH), per_lane(GROUP_CH)],
        out_specs=[pl.BlockSpec((c, GROUP_CH), lambda b, g, ci: (rows(b, g, ci), g)),
                   pl.BlockSpec((1, 1, B_DSTATE, GROUP_CH), lambda b, g, ci: (b, g, 0, 0))],
        out_shape=[SDS((T, B_DINNER), BF16), SDS((nseq, G, B_DSTATE, GROUP_CH), F32)],
        scratch_shapes=[pltpu.VMEM((B_DSTATE, GROUP_CH), F32),
                        pltpu.VMEM((SUBLANES, GROUP_CH), F32),
                        pltpu.VMEM((SUBLANES, B_DSTATE), F32),
                        pltpu.VMEM((SUBLANES, B_DSTATE), F32)],
        compiler_params=_params(("arbitrary", "arbitrary", "arbitrary")),
        name="ssd_prompt",
    )(p, p, p, p, dt_cols, dt_rows, w["b_conv_w"], w["b_conv_w"], w["b_conv_w"],
      w["b_conv_b"], w["b_conv_b"], w["b_conv_b"],
      w["dtb_cols"], w["dtb_rows"], w["alog_cols"], w["alog_rows"], w["d_lane"], w["b_norm_w"])


def _ssd_sample_kernel(x_ref, b_ref, c_ref, z_ref, dt_ref, sx_ref, sb_ref, sc_ref,
                       cwx_ref, cwb_ref, cwc_ref, cbx_ref, cbb_ref, cbc_ref,
                       dtb_ref, al_ref, d_ref, nw_ref, hs_ref,
                       y_ref, hso_ref, *, seq_len):
    nbs = SEQ_GROUP
    rs = nbs * seq_len
    sr = _SampleRows(rs, seq_len)
    prow = lax.broadcasted_iota(jnp.int32, (LANES, 1), 0)
    seq_of_prow = jnp.where(prow < rs, prow % nbs, -1)

    def conv(raw_ref, st_ref, cw_ref, cb_ref):
        raw = raw_ref[...]
        return _silu(_conv_taps(raw, sr.conv_shift(raw, st_ref[...], B_DCONV), cw_ref, cb_ref, B_DCONV))

    x = conv(x_ref, sx_ref, cwx_ref, cbx_ref)
    bm = conv(b_ref, sb_ref, cwb_ref, cbb_ref)
    cm = conv(c_ref, sc_ref, cwc_ref, cbc_ref)

    dt = _softplus(_group_heads_first(dt_ref[...], pl.program_id(1)) + dtb_ref[...])
    a = dt * (-jnp.exp(al_ref[...]))
    acum = a
    total = a
    for k in range(1, seq_len):
        acum = acum + sr.back(a, k)
        total = total + sr.back(a, k) + sr.ahead(a, k)

    coefs = []
    for d in range(seq_len):
        cb_d = jnp.sum(cm * sr.back(bm, d), axis=1, keepdims=True)
        coefs.append(cb_d * jnp.exp(acum - sr.back(acum, d)) * sr.back(dt, d))
    x_back = [sr.back(x, d) for d in range(seq_len)]

    e_ac = jnp.exp(acum)
    e_tot = jnp.exp(total)
    to_end = jnp.exp(total - acum) * dt
    d_all = d_ref[...]

    ys = []
    xws = []
    for pr in range(HEADS_PER_GROUP // 2):
        h0 = 2 * pr
        sl = slice(pr * LANES, (pr + 1) * LANES)
        yp = d_all[:, sl] * x[:, sl]
        for d in range(seq_len):
            yp = yp + _pair_select(coefs[d], h0) * x_back[d][:, sl]
        ys.append(yp)
        xws.append(x[:, sl] * _pair_select(to_end, h0))
    xw_t = _pad_rows(jnp.concatenate(xws, axis=1), LANES).T.astype(BF16)
    cmb = cm.astype(BF16)
    bm_pad = _pad_rows(bm, LANES)
    y_int = jnp.zeros((rs, GROUP_CH), F32)
    for bb in range(nbs):
        hs_g = hs_ref[bb].reshape(GROUP_CH, B_DSTATE)
        y_int = jnp.where(sr.seq == bb, _dot_nt(cmb, hs_g.astype(BF16)), y_int)
        b_mine = jnp.where(seq_of_prow == bb, bm_pad, 0.0).astype(BF16)
        upd = _dot(xw_t, b_mine)
        for r in range(HEADS_PER_GROUP):
            dec = e_tot[bb:bb + 1, r:r + 1]
            hso_ref[bb, r] = dec * hs_ref[bb, r] + upd[r * B_HEADDIM:(r + 1) * B_HEADDIM, :]
    e_pairs = jnp.concatenate([_pair_select(e_ac, 2 * pr) for pr in range(HEADS_PER_GROUP // 2)], axis=1)
    y = jnp.concatenate(ys, axis=1) + y_int * e_pairs
    y_ref[...] = _group_norm_gate(y, z_ref[...], nw_ref[...])


def _ssd_sample(p, dt_cols, conv_state, w, hs, *, seq_len):
    T = p.shape[0]
    nbs = SEQ_GROUP
    rs = nbs * seq_len
    nb = hs.shape[0]
    G = B_GROUPS
    kx = B_DINNER // GROUP_CH
    kb = (2 * B_DINNER) // B_DSTATE
    kc = kb + G
    wb = B_DINNER // B_DSTATE
    wc = wb + G
    per_lane = lambda width, start=0: pl.BlockSpec((1, width), lambda i, g: (0, start + g))
    hs_spec = pl.BlockSpec((nbs, HEADS_PER_GROUP, B_HEADDIM, B_DSTATE), lambda i, g: (i, g, 0, 0))
    return pl.pallas_call(
        functools.partial(_ssd_sample_kernel, seq_len=seq_len),
        grid=(nb // nbs, G),
        in_specs=[pl.BlockSpec((rs, GROUP_CH), lambda i, g: (i, kx + g)),
                  pl.BlockSpec((rs, B_DSTATE), lambda i, g: (i, kb + g)),
                  pl.BlockSpec((rs, B_DSTATE), lambda i, g: (i, kc + g)),
                  pl.BlockSpec((rs, GROUP_CH), lambda i, g: (i, g)),
                  pl.BlockSpec((rs, LANES), lambda i, g: (i, 0)),
                  pl.BlockSpec((rs, GROUP_CH), lambda i, g: (i, g)),
                  pl.BlockSpec((rs, B_DSTATE), lambda i, g: (i, wb + g)),
                  pl.BlockSpec((rs, B_DSTATE), lambda i, g: (i, wc + g)),
                  pl.BlockSpec((None, B_DCONV, GROUP_CH), lambda i, g: (0, 0, g)),
                  pl.BlockSpec((None, B_DCONV, B_DSTATE), lambda i, g: (0, 0, wb + g)),
                  pl.BlockSpec((None, B_DCONV, B_DSTATE), lambda i, g: (0, 0, wc + g)),
                  per_lane(GROUP_CH), per_lane(B_DSTATE, wb), per_lane(B_DSTATE, wc),
                  per_lane(LANES), per_lane(LANES), per_lane(GROUP_CH), per_lane(GROUP_CH),
                  hs_spec],
        out_specs=[pl.BlockSpec((rs, GROUP_CH), lambda i, g: (i, g)), hs_spec],
        out_shape=[SDS((T, B_DINNER), F32), SDS(hs.shape, F32)],
        compiler_params=_params(("arbitrary", "arbitrary")),
        name="ssd_sample",
    )(p, p, p, p, dt_cols, conv_state, conv_state, conv_state,
      w["b_conv_w"], w["b_conv_w"], w["b_conv_w"], w["b_conv_b"], w["b_conv_b"], w["b_conv_b"],
      w["dtb_cols"], w["alog_cols"], w["d_lane"], w["b_norm_w"], hs)


def _pad_lanes(a, width=LANES):
    return jnp.pad(a, ((0, 0), (0, width - a.shape[1])))


def _per_group_lanes(a):
    r = a.shape[0]
    a = a.reshape(r, B_GROUPS, HEADS_PER_GROUP)
    return jnp.pad(a, ((0, 0), (0, 0), (0, LANES - HEADS_PER_GROUP))).reshape(r, B_GROUPS * LANES)


def _to_sample_rows(a, seq_len):
    nseq, nt, ch = a.shape
    a = a.reshape(nseq // SEQ_GROUP, SEQ_GROUP, nt, ch).transpose(0, 2, 1, 3)
    a = jnp.pad(a, ((0, 0), (0, seq_len - nt), (0, 0), (0, 0)))
    return a.reshape(nseq * seq_len, ch)


def _from_sample_rows(a, nseq, seq_len, first_token=0):
    ch = a.shape[1]
    a = a.reshape(nseq // SEQ_GROUP, seq_len, SEQ_GROUP, ch)[:, first_token:]
    return a.transpose(0, 2, 1, 3).reshape(nseq, seq_len - first_token, ch)


def _forward(x, nseq, seq_len, state, w, *, prompt):
    T = x.shape[0]
    tm = 1024 if prompt else T
    a_main = 2 * A_HEADS * A_DK + 2 * A_HEADS * A_DV
    b_main = B_DINNER + B_CONV_DIM
    nstate = FFN_CONV - 1
    cast = {"ffn_bf16": [None, None]}

    outs = _norm_matmul(x, w["norm_mix"], 0, w["a_w_in"], a_main, w["a_w_gate"], tm=tm, tn=1024)
    p_a, gates = outs[:2]
    cast["a_w_in"] = outs[-1]
    if prompt:
        gates_t = gates[:, :2 * A_HEADS].T
        hg, m_c, m_n, m_m = _mlstm_prompt(p_a, gates, gates_t, w["a_b_row"], w["a_b_col"],
                                          nseq=nseq, seq_len=seq_len, c=256)
        m_m = m_m[:, :, 0, 0]
    else:
        c0, n0, m0 = state["mlstm"]
        m_rows = _pad_lanes(_to_sample_rows(jnp.broadcast_to(m0[:, None, :], (nseq, seq_len, A_HEADS)), seq_len))
        hg, m_c, m_n, m_rows_new = _mlstm_sample(p_a, gates, w["a_b_row"], m_rows, c0, n0[:, :, None, :],
                                                 seq_len=seq_len)
        m_m = _from_sample_rows(m_rows_new, nseq, seq_len, seq_len - 1)[:, 0, :A_HEADS]
    m_n = m_n[:, :, 0, :]
    outs = _matmul_res(hg, w["a_w_out"], x, tm=tm, tn=512)
    x, cast["a_w_out"] = outs[0], outs[-1]

    def ffn(x, layer, final_w=None):
        if prompt:
            ftm = 512
            x_new, tg, tv = _ffn_prompt(x, layer, w, tm=ftm, tf=512, seq_len=seq_len, final_w=final_w)
            tps = seq_len // ftm
            pick = lambda t: t.reshape(nseq, tps, SUBLANES, D_FF)[:, tps - 1, SUBLANES - nstate:]
            tails = jnp.concatenate([pick(tg), pick(tv)], axis=-1)
        else:
            st = state["ffn_conv"][layer].reshape(nseq, nstate * 2 * D_FF)
            outs = _ffn_sample(x, layer, w, st, tf=256, seq_len=seq_len, final_w=final_w)
            x_new = outs[0]
            tails = jnp.stack([jnp.concatenate([outs[1 + t], outs[1 + nstate + t]], axis=-1) for t in range(nstate)],
                              axis=1)
            cast["ffn_bf16"][layer] = tuple(outs[1 + 2 * nstate:])
        return x_new, tails

    x, f_conv0 = ffn(x, 0)

    outs = _norm_matmul(x, w["norm_mix"], 1, w["b_w_in"], b_main, w["b_w_dt"], tm=tm, tn=1024)
    p_b, dt_raw = outs[:2]
    cast["b_w_in"] = outs[-1]
    if prompt:
        s_conv = p_b.reshape(nseq, seq_len, b_main)[:, seq_len - (B_DCONV - 1):, B_DINNER:]
        dt_rows = dt_raw[:, :B_HEADS].T
        yn, s_t = _ssd_prompt(p_b, dt_raw, dt_rows, w, nseq=nseq, seq_len=seq_len, c=256)
        ssm = s_t.reshape(nseq, B_GROUPS, B_DSTATE, HEADS_PER_GROUP, B_HEADDIM)
        ssm = ssm.transpose(0, 1, 3, 4, 2).reshape(nseq, B_HEADS, B_HEADDIM, B_DSTATE)
    else:
        s_conv = _from_sample_rows(p_b, nseq, seq_len, seq_len - (B_DCONV - 1))[:, :, B_DINNER:]
        st = _to_sample_rows(state["ssm_conv"], seq_len)
        yn, ssm = _ssd_sample(p_b, dt_raw, st, w, state["ssm"], seq_len=seq_len)
    outs = _matmul_res(yn, w["b_w_out"], x, tm=tm, tn=512)
    x, cast["b_w_out"] = outs[0], outs[-1]

    y, f_conv1 = ffn(x, 1, final_w=w["norm_final"])
    return (y, m_c[None], m_n[None], m_m[None], ssm[None], s_conv[None], jnp.stack([f_conv0, f_conv1]), cast)


def kernel(x_prompt, x_sample, state_mlstm_C, state_mlstm_n, state_mlstm_m, state_ssm, state_ssm_conv,
           state_ffn_conv, norm_mix, norm_ffn, norm_final, a_w_in, a_b_if, a_w_out, b_w_in, b_conv_w,
           b_conv_b, b_dt_bias, b_A_log, b_D, b_norm_w, b_w_out, f_w_up, f_conv_w, f_conv_b, f_w_down):
    a_main = 2 * A_HEADS * A_DK + 2 * A_HEADS * A_DV
    b_main = B_DINNER + B_CONV_DIM
    w = {
        "norm_mix": norm_mix[:, None], "norm_ffn": norm_ffn[:, None], "norm_final": norm_final[None],
        "a_w_in": jnp.swapaxes(a_w_in, 1, 2),
        "a_w_gate": _pad_lanes(a_w_in[0, :, a_main:]).astype(BF16),
        "a_b_row": _pad_lanes(a_b_if), "a_b_col": a_b_if.T,
        "a_w_out": a_w_out,
        "b_w_in": jnp.swapaxes(b_w_in, 1, 2),
        "b_w_dt": _pad_lanes(b_w_in[0, :, b_main:]).astype(BF16),
        "b_conv_w": b_conv_w, "b_conv_b": b_conv_b,
        "dtb_cols": _per_group_lanes(b_dt_bias), "dtb_rows": b_dt_bias.T,
        "alog_cols": _per_group_lanes(b_A_log), "alog_rows": b_A_log.T,
        "d_lane": jnp.repeat(b_D, B_HEADDIM, axis=1), "b_norm_w": b_norm_w,
        "b_w_out": b_w_out,
        "f_w_up": f_w_up, "f_conv_w": f_conv_w, "f_conv_b": f_conv_b[:, None],
        "f_w_down": f_w_down,
    }
    bp, lp, _ = x_prompt.shape
    bs, ls, _ = x_sample.shape
    state = {"mlstm": (state_mlstm_C[0], state_mlstm_n[0], state_mlstm_m[0]),
             "ssm": state_ssm[0], "ssm_conv": state_ssm_conv[0], "ffn_conv": state_ffn_conv}
    ys, sc, sn, sm, sssm, ssconv, sfconv, w_bf16 = _forward(_to_sample_rows(x_sample, ls), bs, ls, state, w,
                                                            prompt=False)
    yp, pc, pn, pm, pssm, psconv, pfconv, _ = _forward(x_prompt.reshape(bp * lp, D_MODEL), bp, lp, None,
                                                       {**w, **w_bf16}, prompt=True)
    return (yp.reshape(bp, lp, D_MODEL), _from_sample_rows(ys, bs, ls), pc, sc, pn, sn, pm, sm,
            pssm, sssm, psconv, ssconv, pfconv, sfconv)
```

```python
import functools

import jax
import jax.numpy as jnp
from jax import lax
from jax.experimental import pallas as pl
from jax.experimental.pallas import tpu as pltpu

F32 = jnp.float32
BF16 = jnp.bfloat16
SDS = jax.ShapeDtypeStruct

D_MODEL = 2048
A_HEADS = 8
A_DK = 128
A_DV = 256
B_DINNER = 4096
B_HEADDIM = 64
B_HEADS = 64
B_GROUPS = 8
B_DSTATE = 128
B_DCONV = 4
B_CONV_DIM = 6144
D_FF = 5632
FFN_CONV = 3
EPS = 1e-6
NEG_INF = float("-inf")
LOG2E = 1.4426950408889634
MAX_EXP2 = 126.0

LANES = 128
SUBLANES = 8
VMEM_LIMIT_BYTES = 56 * 1024 * 1024

HEADS_PER_GROUP = B_HEADS // B_GROUPS
GROUP_CH = HEADS_PER_GROUP * B_HEADDIM
SEQ_GROUP = SUBLANES


def _params(sem, flags=None):
    return pltpu.CompilerParams(dimension_semantics=sem, vmem_limit_bytes=VMEM_LIMIT_BYTES, flags=flags)


def _sigmoid(x):
    return 1.0 / (1.0 + jnp.exp(-x))


def _silu(x):
    h = 0.5 * x
    return h + h * jnp.tanh(h)


def _softplus(x):
    return jnp.maximum(x, 0.0) + jnp.log1p(jnp.exp(-jnp.abs(x)))


def _log_sigmoid(x):
    return -_softplus(-x)


def _rms(x, w):
    ms = jnp.mean(x * x, axis=-1, keepdims=True)
    return x * lax.rsqrt(ms + EPS) * w


def _cumsum_rows(x):
    n = x.shape[0]
    row = lax.broadcasted_iota(jnp.int32, x.shape, 0)
    k = 1
    while k < n:
        x = x + jnp.where(row >= k, pltpu.roll(x, k, 0), 0.0)
        k *= 2
    return x


def _dot(a, b):
    return jnp.dot(a, b, preferred_element_type=F32)


def _dot_nt(a, b):
    return lax.dot_general(a, b, (((1,), (1,)), ((), ())), preferred_element_type=F32)


def _dot_tn(a, b):
    return lax.dot_general(a, b, (((0,), (0,)), ((), ())), preferred_element_type=F32)


def _conv_taps(u, shifted, w_ref, b_ref, ntaps):
    y = u * w_ref[ntaps - 1:ntaps, :]
    for k in range(1, ntaps):
        y = y + shifted(k) * w_ref[ntaps - 1 - k:ntaps - k, :]
    return y + b_ref[...]


def _shift_with_prev(u, prev8, k):
    ext = jnp.concatenate([prev8, u], axis=0)
    return pltpu.roll(ext, k, 0)[SUBLANES:]


def _pad_rows(a, rows):
    if a.shape[0] == rows:
        return a
    return jnp.concatenate([a, jnp.zeros((rows - a.shape[0], a.shape[1]), a.dtype)], axis=0)


class _SampleRows:
    def __init__(self, rows, seq_len):
        self.rows = rows
        self.seq_len = seq_len
        r = lax.broadcasted_iota(jnp.int32, (rows, 1), 0)
        self.token = (r // SEQ_GROUP) % seq_len
        self.seq = r % SEQ_GROUP

    def back(self, a, k):
        if k == 0:
            return a
        return jnp.where(self.token >= k, pltpu.roll(a, SEQ_GROUP * k, 0), 0.0)

    def ahead(self, a, k):
        if k == 0:
            return a
        return jnp.where(self.token + k < self.seq_len, pltpu.roll(a, self.rows - SEQ_GROUP * k, 0), 0.0)

    def conv_shift(self, u, state, ntaps):
        return lambda k: self.back(u, k) + self.ahead(state, ntaps - 1 - k)


def _tokens_to_rows(slots, seq_len):
    nseq, ch = slots[0].shape
    pad = jnp.zeros(((seq_len - len(slots)) * SEQ_GROUP, ch), slots[0].dtype)
    parts = []
    for g in range(nseq // SEQ_GROUP):
        parts += [s[g * SEQ_GROUP:(g + 1) * SEQ_GROUP] for s in slots] + [pad]
    return jnp.concatenate(parts, axis=0)


def _rows_of_token(a, seq_len, t):
    step = seq_len * SEQ_GROUP
    return jnp.concatenate([a[g * step + t * SEQ_GROUP:g * step + (t + 1) * SEQ_GROUP]
                            for g in range(a.shape[0] // step)], axis=0)


def _norm_matmul_kernel(x_ref, nw_ref, w_ref, ws_ref, o_ref, os_ref, *rest):
    hn_ref = rest[-1]

    @pl.when(pl.program_id(1) == 0)
    def _():
        hn = _rms(x_ref[...], nw_ref[...]).astype(BF16)
        hn_ref[...] = hn
        os_ref[...] = _dot(hn, ws_ref[...])

    wb = w_ref[...].astype(BF16)
    if len(rest) == 2:
        rest[0][...] = wb
    o_ref[...] = _dot_nt(hn_ref[...], wb)


def _norm_matmul(x, nw, layer, w_t, n_main, ws, *, tm, tn):
    T, D = x.shape
    emit = w_t.dtype != BF16
    w_spec = pl.BlockSpec((None, tn, D), lambda i, j: (0, j, 0))
    out_specs = [pl.BlockSpec((tm, tn), lambda i, j: (i, j)), pl.BlockSpec((tm, LANES), lambda i, j: (i, 0))]
    out_shape = [SDS((T, n_main), F32), SDS((T, LANES), F32)]
    if emit:
        assert T == tm, "the bfloat16 copy is written once, by a single row tile"
        out_specs.append(w_spec)
        out_shape.append(SDS((1, n_main, D), BF16))
    return pl.pallas_call(
        _norm_matmul_kernel,
        grid=(T // tm, n_main // tn),
        in_specs=[pl.BlockSpec((tm, D), lambda i, j: (i, 0)),
                  pl.BlockSpec((None, 1, D), lambda i, j: (layer, 0, 0)),
                  w_spec,
                  pl.BlockSpec((D, LANES), lambda i, j: (0, 0))],
        out_specs=out_specs,
        out_shape=out_shape,
        scratch_shapes=[pltpu.VMEM((tm, D), BF16)],
        compiler_params=_params(("arbitrary", "arbitrary")),
        name="norm_matmul",
    )(x, nw, w_t, ws)


def _matmul_res_kernel(a_ref, w_ref, r_ref, o_ref, *rest):
    wb = w_ref[...].astype(BF16)
    if rest:
        rest[0][...] = wb
    o_ref[...] = r_ref[...] + _dot(a_ref[...].astype(BF16), wb)


def _matmul_res(a, w, res, *, tm, tn):
    T, K = a.shape
    N = w.shape[2]
    emit = w.dtype != BF16
    w_spec = pl.BlockSpec((None, K, tn), lambda i, j: (0, 0, j))
    out_specs = [pl.BlockSpec((tm, tn), lambda i, j: (i, j))]
    out_shape = [SDS((T, N), F32)]
    if emit:
        assert T == tm, "the bfloat16 copy is written once, by a single row tile"
        out_specs.append(w_spec)
        out_shape.append(SDS((1, K, N), BF16))
    return pl.pallas_call(
        _matmul_res_kernel,
        grid=(T // tm, N // tn),
        in_specs=[pl.BlockSpec((tm, K), lambda i, j: (i, 0)),
                  w_spec,
                  pl.BlockSpec((tm, tn), lambda i, j: (i, j))],
        out_specs=out_specs,
        out_shape=out_shape,
        compiler_params=_params(("arbitrary", "arbitrary")),
        name="matmul_res",
    )(a, w, res)


def _ffn_sample_kernel(*refs, tm, seq_len, final_norm):
    refs = list(refs)
    x_ref, nw_ref, wug_ref, wuv_ref, cwg_ref, cwv_ref, cbg_ref, cbv_ref, wd_ref = refs[:9]
    del refs[:9]
    nstate = FFN_CONV - 1
    sg_refs, sv_refs = refs[:nstate], refs[nstate:2 * nstate]
    del refs[:2 * nstate]
    fw_ref = refs.pop(0) if final_norm else None
    out_ref = refs.pop(0)
    tg_refs, tv_refs = refs[:nstate], refs[nstate:2 * nstate]
    wbg_ref, wbv_ref, wbd_ref = refs[2 * nstate:2 * nstate + 3]
    hn_s, acc_s = refs[2 * nstate + 3:]
    j = pl.program_id(1)

    @pl.when(j == 0)
    def _():
        hn_s[...] = _rms(x_ref[...], nw_ref[...]).astype(BF16)
        acc_s[...] = jnp.zeros_like(acc_s)

    wug = wug_ref[...].astype(BF16)
    wuv = wuv_ref[...].astype(BF16)
    wd = wd_ref[...].astype(BF16)
    hn = hn_s[...]
    ug = _dot(hn, wug)
    uv = _dot(hn, wuv)

    wbg_ref[...] = wug
    wbv_ref[...] = wuv
    wbd_ref[...] = wd
    for t in range(nstate):
        tg_refs[t][...] = _rows_of_token(ug, seq_len, seq_len - nstate + t)
        tv_refs[t][...] = _rows_of_token(uv, seq_len, seq_len - nstate + t)
    sr = _SampleRows(tm, seq_len)
    sg = _tokens_to_rows([r[...] for r in sg_refs], seq_len)
    sv = _tokens_to_rows([r[...] for r in sv_refs], seq_len)
    yg = _conv_taps(ug, sr.conv_shift(ug, sg, FFN_CONV), cwg_ref, cbg_ref, FFN_CONV)
    yv = _conv_taps(uv, sr.conv_shift(uv, sv, FFN_CONV), cwv_ref, cbv_ref, FFN_CONV)

    act = (_silu(yg) * yv).astype(BF16)
    acc_s[...] += _dot(act, wd)

    @pl.when(j == pl.num_programs(1) - 1)
    def _():
        y = x_ref[...] + acc_s[...]
        out_ref[...] = _rms(y, fw_ref[...]) if final_norm else y


def _ffn_sample(x, layer, w, state, *, tf, seq_len, final_w=None):
    T, D = x.shape
    nj = D_FF // tf
    nstate = FFN_CONV - 1
    nseq = T // seq_len
    in_specs = [pl.BlockSpec((T, D), lambda i, j: (0, 0)),
                pl.BlockSpec((None, 1, D), lambda i, j: (layer, 0, 0)),
                pl.BlockSpec((None, D, tf), lambda i, j: (layer, 0, j)),
                pl.BlockSpec((None, D, tf), lambda i, j: (layer, 0, nj + j)),
                pl.BlockSpec((None, FFN_CONV, tf), lambda i, j: (layer, 0, j)),
                pl.BlockSpec((None, FFN_CONV, tf), lambda i, j: (layer, 0, nj + j)),
                pl.BlockSpec((None, 1, tf), lambda i, j: (layer, 0, j)),
                pl.BlockSpec((None, 1, tf), lambda i, j: (layer, 0, nj + j)),
                pl.BlockSpec((None, tf, D), lambda i, j: (layer, j, 0))]
    args = [x, w["norm_ffn"], w["f_w_up"], w["f_w_up"], w["f_conv_w"], w["f_conv_w"], w["f_conv_b"], w["f_conv_b"],
            w["f_w_down"]]
    in_specs += [pl.BlockSpec((nseq, tf), lambda i, j, t=t: (0, t * 2 * nj + j)) for t in range(nstate)]
    in_specs += [pl.BlockSpec((nseq, tf), lambda i, j, t=t: (0, t * 2 * nj + nj + j)) for t in range(nstate)]
    args += [state] * (2 * nstate)
    if final_w is not None:
        in_specs.append(pl.BlockSpec((1, D), lambda i, j: (0, 0)))
        args.append(final_w)
    out_specs = [pl.BlockSpec((T, D), lambda i, j: (0, 0))]
    out_specs += [pl.BlockSpec((nseq, tf), lambda i, j: (0, j))] * (2 * nstate)
    out_specs += [pl.BlockSpec((D, tf), lambda i, j: (0, j))] * 2 + [pl.BlockSpec((tf, D), lambda i, j: (j, 0))]
    out_shape = [SDS((T, D), F32)] + [SDS((nseq, D_FF), F32)] * (2 * nstate)
    out_shape += [SDS((D, D_FF), BF16)] * 2 + [SDS((D_FF, D), BF16)]
    return pl.pallas_call(
        functools.partial(_ffn_sample_kernel, tm=T, seq_len=seq_len, final_norm=final_w is not None),
        grid=(1, nj),
        in_specs=in_specs,
        out_specs=out_specs,
        out_shape=out_shape,
        scratch_shapes=[pltpu.VMEM((T, D), BF16), pltpu.VMEM((T, D), F32)],
        compiler_params=_params(("arbitrary", "arbitrary")),
        name="ffn_sample",
    )(*args)


def _ffn_prompt_kernel(x_ref, nw_ref, wug_ref, wuv_ref, cwg_ref, cwv_ref, cbg_ref, cbv_ref, wd_ref, *rest,
                       tm, tiles_per_seq, nj, final_norm):
    rest = list(rest)
    fw_ref = rest.pop(0) if final_norm else None
    out_ref, tg_ref, tv_ref, hn_s, acc_s, cg_s, cv_s = rest[:7]
    act_slots = rest[7:9]
    i = pl.program_id(0)
    j = pl.program_id(1)

    def up_act(slot):
        hn = hn_s[...]
        ug = _dot(hn, wug_ref[...])
        uv = _dot(hn, wuv_ref[...])
        tail_g = ug[tm - SUBLANES:, :]
        tail_v = uv[tm - SUBLANES:, :]
        tg_ref[0] = tail_g
        tv_ref[0] = tail_v
        prev_g = cg_s[j]
        prev_v = cv_s[j]
        yg = _conv_taps(ug, lambda k: _shift_with_prev(ug, prev_g, k), cwg_ref, cbg_ref, FFN_CONV)
        yv = _conv_taps(uv, lambda k: _shift_with_prev(uv, prev_v, k), cwv_ref, cbv_ref, FFN_CONV)
        cg_s[j] = tail_g
        cv_s[j] = tail_v
        act_slots[slot][...] = (_silu(yg) * yv).astype(BF16)

    def down_proj(slot):
        acc_s[...] += _dot(act_slots[slot][...], wd_ref[...])

    @pl.when(j == 0)
    def _():
        hn_s[...] = _rms(x_ref[...], nw_ref[...]).astype(BF16)
        acc_s[...] = jnp.zeros_like(acc_s)

        @pl.when(i % tiles_per_seq == 0)
        def _():
            cg_s[...] = jnp.zeros_like(cg_s)
            cv_s[...] = jnp.zeros_like(cv_s)

        up_act(0)

    steady = (j >= 1) & (j < nj)
    for parity in (0, 1):
        @pl.when(steady & (j % 2 == parity))
        def _():
            up_act(parity)
            down_proj(1 - parity)

    @pl.when(j == nj)
    def _():
        down_proj((nj - 1) % 2)
        y = x_ref[...] + acc_s[...]
        out_ref[...] = _rms(y, fw_ref[...]) if final_norm else y


def _ffn_prompt(x, layer, w, *, tm, tf, seq_len, final_w=None):
    T, D = x.shape
    nj = D_FF // tf
    ni = T // tm
    assert nj >= 2 and seq_len % tm == 0
    act_j = lambda j: jnp.minimum(j, nj - 1)
    down_j = lambda j: jnp.maximum(j - 1, 0)
    wug, wuv, wd = w["ffn_bf16"][layer]
    in_specs = [pl.BlockSpec((tm, D), lambda i, j: (i, 0)),
                pl.BlockSpec((None, 1, D), lambda i, j: (layer, 0, 0)),
                pl.BlockSpec((D, tf), lambda i, j: (0, act_j(j))),
                pl.BlockSpec((D, tf), lambda i, j: (0, act_j(j))),
                pl.BlockSpec((None, FFN_CONV, tf), lambda i, j: (layer, 0, act_j(j))),
                pl.BlockSpec((None, FFN_CONV, tf), lambda i, j: (layer, 0, nj + act_j(j))),
                pl.BlockSpec((None, 1, tf), lambda i, j: (layer, 0, act_j(j))),
                pl.BlockSpec((None, 1, tf), lambda i, j: (layer, 0, nj + act_j(j))),
                pl.BlockSpec((tf, D), lambda i, j: (down_j(j), 0))]
    args = [x, w["norm_ffn"], wug, wuv, w["f_conv_w"], w["f_conv_w"], w["f_conv_b"], w["f_conv_b"], wd]
    if final_w is not None:
        in_specs.append(pl.BlockSpec((1, D), lambda i, j: (0, 0)))
        args.append(final_w)
    tail_spec = pl.BlockSpec((1, SUBLANES, tf), lambda i, j: (i, 0, act_j(j)))
    return pl.pallas_call(
        functools.partial(_ffn_prompt_kernel, tm=tm, tiles_per_seq=seq_len // tm, nj=nj,
                          final_norm=final_w is not None),
        grid=(ni, nj + 1),
        in_specs=in_specs,
        out_specs=[pl.BlockSpec((tm, D), lambda i, j: (i, 0)), tail_spec, tail_spec],
        out_shape=[SDS((T, D), F32)] + [SDS((ni, SUBLANES, D_FF), F32)] * 2,
        scratch_shapes=[pltpu.VMEM((tm, D), BF16), pltpu.VMEM((tm, D), F32)]
        + [pltpu.VMEM((nj, SUBLANES, tf), F32)] * 2
        + [pltpu.VMEM((tm, tf), BF16)] * 2,
        compiler_params=_params(("arbitrary", "arbitrary")),
        name="ffn_prompt",
    )(*args)


def _mlstm_intra(q, ks, v, ig_col, lf_cum_col, lf_cum_row, ig_row, m_prev_col, causal):
    log_d = jnp.where(causal, lf_cum_col - lf_cum_row + ig_row, NEG_INF)
    inter = lf_cum_col + m_prev_col
    m_t = jnp.maximum(inter, jnp.max(log_d, axis=1, keepdims=True))
    d_m = jnp.exp(log_d - m_t)
    w_inter = jnp.exp(inter - m_t)
    kb = _pad_rows(ks.astype(BF16), causal.shape[1])
    vb = _pad_rows(v.astype(BF16), causal.shape[1])
    s = _dot_nt(q.astype(BF16), kb) * d_m
    num = _dot(s.astype(BF16), vb)
    den = jnp.sum(s, axis=1, keepdims=True)
    return m_t, w_inter, inter, num, den


def _mlstm_prompt_kernel(q_ref, k_ref, v_ref, o_ref, g_ref, br_ref,
                         hg_ref, co_ref, no_ref, mo_ref, c_s, n_s, m_s, *, c):
    ci = pl.program_id(1)

    @pl.when(ci == 0)
    def _():
        c_s[...] = jnp.zeros_like(c_s)
        n_s[...] = jnp.zeros_like(n_s)
        m_s[...] = jnp.zeros_like(m_s)

    g = g_ref[...] + br_ref[...]
    lf_cum = _cumsum_rows(_log_sigmoid(g))
    gt = g.T
    lf_cum_t = lf_cum.T
    row = lax.broadcasted_iota(jnp.int32, (c, c), 0)
    col = lax.broadcasted_iota(jnp.int32, (c, c), 1)
    causal = col <= row
    last = ci == pl.num_programs(1) - 1

    for h in range(A_HEADS):
        ig_col = g[:, h:h + 1]
        b_col = lf_cum[:, A_HEADS + h:A_HEADS + h + 1]
        ig_row = gt[h:h + 1, :]
        b_row = lf_cum_t[A_HEADS + h:A_HEADS + h + 1, :]
        q = q_ref[:, h * A_DK:(h + 1) * A_DK]
        ks = k_ref[:, h * A_DK:(h + 1) * A_DK] * (A_DK ** -0.5)
        v = v_ref[:, h * A_DV:(h + 1) * A_DV]
        m_prev = m_s[h][:, 0:1]
        m_t, w_inter, inter, num, den = _mlstm_intra(q, ks, v, ig_col, b_col, b_row, ig_row, m_prev, causal)
        cmat = c_s[h]
        nvec = n_s[h]
        num = w_inter * _dot(q.astype(BF16), cmat.astype(BF16)) + num
        den = w_inter * jnp.sum(q * nvec, axis=1, keepdims=True) + den
        hh = num / jnp.maximum(jnp.abs(den), jnp.exp(-m_t))
        hg_ref[:, h * A_DV:(h + 1) * A_DV] = (_sigmoid(o_ref[:, h * A_DV:(h + 1) * A_DV]) * hh).astype(BF16)

        m_new = m_t[c - 1:c, :]
        w_state = jnp.exp(b_col[c - 1:c, :] - b_col + ig_col - m_new)
        decay = jnp.exp(inter[c - 1:c, :] - m_new)
        kw = ks * w_state
        c_new = decay * cmat + _dot_tn(kw.astype(BF16), v.astype(BF16))
        n_new = decay * nvec + jnp.sum(kw, axis=0, keepdims=True)
        m_row = jnp.broadcast_to(m_new, (1, LANES))
        c_s[h] = c_new
        n_s[h] = n_new
        m_s[h] = m_row

        @pl.when(last)
        def _():
            co_ref[0, h] = c_new
            no_ref[0, h] = n_new
            mo_ref[0, h] = m_row


def _mlstm_prompt(p, gates, bias_row, *, nseq, seq_len, c):
    T = p.shape[0]
    nc = seq_len // c
    hk = A_HEADS * A_DK
    hv = A_HEADS * A_DV
    rows = lambda b, ci: b * nc + ci
    return pl.pallas_call(
        functools.partial(_mlstm_prompt_kernel, c=c),
        grid=(nseq, nc),
        in_specs=[pl.BlockSpec((c, hk), lambda b, ci: (rows(b, ci), 0)),
                  pl.BlockSpec((c, hk), lambda b, ci: (rows(b, ci), 1)),
                  pl.BlockSpec((c, hv), lambda b, ci: (rows(b, ci), 1)),
                  pl.BlockSpec((c, hv), lambda b, ci: (rows(b, ci), 2)),
                  pl.BlockSpec((c, LANES), lambda b, ci: (rows(b, ci), 0)),
                  pl.BlockSpec((1, LANES), lambda b, ci: (0, 0))],
        out_specs=[pl.BlockSpec((c, hv), lambda b, ci: (rows(b, ci), 0)),
                   pl.BlockSpec((1, A_HEADS, A_DK, A_DV), lambda b, ci: (b, 0, 0, 0)),
                   pl.BlockSpec((1, A_HEADS, 1, A_DK), lambda b, ci: (b, 0, 0, 0)),
                   pl.BlockSpec((1, A_HEADS, 1, LANES), lambda b, ci: (b, 0, 0, 0))],
        out_shape=[SDS((T, hv), BF16),
                   SDS((nseq, A_HEADS, A_DK, A_DV), F32),
                   SDS((nseq, A_HEADS, 1, A_DK), F32),
                   SDS((nseq, A_HEADS, 1, LANES), F32)],
        scratch_shapes=[pltpu.VMEM((A_HEADS, A_DK, A_DV), F32), pltpu.VMEM((A_HEADS, 1, A_DK), F32),
                        pltpu.VMEM((A_HEADS, 1, LANES), F32)],
        compiler_params=_params(("arbitrary", "arbitrary")),
        name="mlstm_prompt",
    )(p, p, p, p, gates, bias_row)


def _mlstm_sample_kernel(q_ref, k_ref, v_ref, o_ref, g_ref, br_ref, mp_ref, ci_ref, ni_ref,
                         hg_ref, co_ref, no_ref, mo_ref, *, seq_len):
    nbs = SEQ_GROUP
    rs = nbs * seq_len
    g = g_ref[...] + br_ref[...]
    lf_all = _log_sigmoid(g)
    m_prev_all = mp_ref[...]
    row = lax.broadcasted_iota(jnp.int32, (rs, LANES), 0)
    col = lax.broadcasted_iota(jnp.int32, (rs, LANES), 1)
    same = ((row % nbs) == (col % nbs)) & (col < rs)
    causal = same & (col <= row)
    causal_t = same & (row <= col)
    eye = row == col
    last = same & (col // nbs == seq_len - 1)
    seq_of_row = lax.broadcasted_iota(jnp.int32, (rs, 1), 0) % nbs
    prow = lax.broadcasted_iota(jnp.int32, (LANES, 1), 0)
    seq_of_prow = jnp.where(prow < rs, prow % nbs, -1)

    def to_row(colvec):
        return jnp.sum(jnp.where(eye, colvec, 0.0), axis=0, keepdims=True)

    def at_last(rowvec):
        return jnp.sum(jnp.where(last, rowvec, 0.0), axis=1, keepdims=True)

    m_out = jnp.zeros((rs, LANES), F32)
    for h in range(A_HEADS):
        ig_col = g[:, h:h + 1]
        lf_col = lf_all[:, A_HEADS + h:A_HEADS + h + 1]
        m_prev = m_prev_all[:, h:h + 1]
        ig_row = to_row(ig_col)
        lf_row = to_row(lf_col)
        b_col = jnp.sum(jnp.where(causal, lf_row, 0.0), axis=1, keepdims=True)
        b_row = jnp.sum(jnp.where(causal_t, lf_col, 0.0), axis=0, keepdims=True)
        q = q_ref[:, h * A_DK:(h + 1) * A_DK]
        ks = k_ref[:, h * A_DK:(h + 1) * A_DK] * (A_DK ** -0.5)
        v = v_ref[:, h * A_DV:(h + 1) * A_DV]
        m_t, w_inter, inter, num, den = _mlstm_intra(q, ks, v, ig_col, b_col, b_row, ig_row, m_prev, causal)
        qb = q.astype(BF16)
        num_i = jnp.zeros((rs, A_DV), F32)
        qn = jnp.zeros((rs, 1), F32)
        for bb in range(nbs):
            mine = seq_of_row == bb
            num_i = jnp.where(mine, _dot(qb, ci_ref[bb, h].astype(BF16)), num_i)
            qn = jnp.where(mine, jnp.sum(q * ni_ref[bb, h], axis=1, keepdims=True), qn)
        num = w_inter * num_i + num
        den = w_inter * qn + den
        hh = num / jnp.maximum(jnp.abs(den), jnp.exp(-m_t))
        hg_ref[:, h * A_DV:(h + 1) * A_DV] = _sigmoid(o_ref[:, h * A_DV:(h + 1) * A_DV]) * hh

        m_new = at_last(to_row(m_t))
        w_state = jnp.exp(at_last(b_row) - b_col + ig_col - m_new)
        decay = jnp.exp(at_last(to_row(inter)) - m_new)
        kw = ks * w_state
        kw_t = _pad_rows(kw, LANES).T.astype(BF16)
        v_pad = _pad_rows(v, LANES)
        for bb in range(nbs):
            v_mine = jnp.where(seq_of_prow == bb, v_pad, 0.0).astype(BF16)
            dec = decay[bb:bb + 1, :]
            co_ref[bb, h] = dec * ci_ref[bb, h] + _dot(kw_t, v_mine)
            no_ref[bb, h] = dec * ni_ref[bb, h] + jnp.sum(jnp.where(seq_of_row == bb, kw, 0.0), axis=0, keepdims=True)
        m_out = jnp.where(col == h, m_new, m_out)
    mo_ref[...] = m_out


def _mlstm_sample(p, gates, bias_row, m_rows, c_in, n_in, *, seq_len):
    T = p.shape[0]
    nbs = SEQ_GROUP
    rs = nbs * seq_len
    nb = c_in.shape[0]
    hk = A_HEADS * A_DK
    hv = A_HEADS * A_DV
    return pl.pallas_call(
        functools.partial(_mlstm_sample_kernel, seq_len=seq_len),
        grid=(nb // nbs,),
        in_specs=[pl.BlockSpec((rs, hk), lambda i: (i, 0)),
                  pl.BlockSpec((rs, hk), lambda i: (i, 1)),
                  pl.BlockSpec((rs, hv), lambda i: (i, 1)),
                  pl.BlockSpec((rs, hv), lambda i: (i, 2)),
                  pl.BlockSpec((rs, LANES), lambda i: (i, 0)),
                  pl.BlockSpec((1, LANES), lambda i: (0, 0)),
                  pl.BlockSpec((rs, LANES), lambda i: (i, 0)),
                  pl.BlockSpec((nbs, A_HEADS, A_DK, A_DV), lambda i: (i, 0, 0, 0)),
                  pl.BlockSpec((nbs, A_HEADS, 1, A_DK), lambda i: (i, 0, 0, 0))],
        out_specs=[pl.BlockSpec((rs, hv), lambda i: (i, 0)),
                   pl.BlockSpec((nbs, A_HEADS, A_DK, A_DV), lambda i: (i, 0, 0, 0)),
                   pl.BlockSpec((nbs, A_HEADS, 1, A_DK), lambda i: (i, 0, 0, 0)),
                   pl.BlockSpec((rs, LANES), lambda i: (i, 0))],
        out_shape=[SDS((T, hv), F32),
                   SDS((nb, A_HEADS, A_DK, A_DV), F32),
                   SDS((nb, A_HEADS, 1, A_DK), F32),
                   SDS((T, LANES), F32)],
        compiler_params=_params(("arbitrary",)),
        name="mlstm_sample",
    )(p, p, p, p, gates, bias_row, m_rows, c_in, n_in)


def _pair_select(a, h0):
    lane = lax.broadcasted_iota(jnp.int32, (a.shape[0], LANES), 1)
    return jnp.where(lane < B_HEADDIM, a[:, h0:h0 + 1], a[:, h0 + 1:h0 + 2])


def _group_heads_first(a, g):
    return pltpu.roll(a, (LANES - HEADS_PER_GROUP * g) % LANES, 1)


def _group_norm_gate(y, z, nw):
    gz = y * _silu(z)
    return gz * lax.rsqrt(jnp.mean(gz * gz, axis=-1, keepdims=True) + EPS) * nw


def _ssd_prompt_kernel(x_ref, b_ref, c_ref, z_ref, dtc_ref,
                       cwx_ref, cwb_ref, cwc_ref, cbx_ref, cbb_ref, cbc_ref,
                       dbr_ref, alr_ref, d_ref, nw_ref,
                       y_ref, so_ref, s_s, xp_s, bp_s, cp_s, *, c):
    ci = pl.program_id(2)

    @pl.when(ci == 0)
    def _():
        s_s[...] = jnp.zeros_like(s_s)
        xp_s[...] = jnp.zeros_like(xp_s)
        bp_s[...] = jnp.zeros_like(bp_s)
        cp_s[...] = jnp.zeros_like(cp_s)

    def conv(raw_ref, prev_s, cw_ref, cb_ref):
        raw = raw_ref[...]
        prev = prev_s[...]
        out = _conv_taps(raw, lambda k: _shift_with_prev(raw, prev, k), cw_ref, cb_ref, B_DCONV)
        prev_s[...] = raw[c - SUBLANES:, :]
        return _silu(out)

    x = conv(x_ref, xp_s, cwx_ref, cbx_ref)
    bm = conv(b_ref, bp_s, cwb_ref, cbb_ref)
    cm = conv(c_ref, cp_s, cwc_ref, cbc_ref)

    dt_c = _softplus(_group_heads_first(dtc_ref[...], pl.program_id(1)) + dbr_ref[...])
    ac_c = _cumsum_rows(dt_c * (-jnp.exp(alr_ref[...])))
    ac_c2 = ac_c * LOG2E
    src_r2 = (ac_c2 - jnp.log(dt_c) * LOG2E).T[:HEADS_PER_GROUP, :]

    row = lax.broadcasted_iota(jnp.int32, (c, c), 0)
    col = lax.broadcasted_iota(jnp.int32, (c, c), 1)
    bmb = bm.astype(BF16)
    cmb = cm.astype(BF16)
    cb = jnp.where(col <= row, _dot_nt(cmb, bmb), 0.0)
    state = s_s[...]
    y_inter = _dot(cmb, state.astype(BF16))
    ac_last = ac_c[c - 1:c, :]
    to_end = jnp.exp(ac_last - ac_c) * dt_c
    e_ac = jnp.exp(ac_c)
    e_last = jnp.exp(ac_last)
    lane = lax.broadcasted_iota(jnp.int32, (c, LANES), 1)
    d_all = d_ref[...]

    ys = []
    for pr in range(HEADS_PER_GROUP // 2):
        h0 = 2 * pr
        sl = slice(pr * LANES, (pr + 1) * LANES)
        xp = x[:, sl]
        xpb = xp.astype(BF16)
        full = []
        for r in (h0, h0 + 1):
            blocks = []
            for rb in range(c // LANES):
                rsl = slice(rb * LANES, (rb + 1) * LANES)
                ncol = (rb + 1) * LANES
                fac = jnp.exp2(jnp.minimum(ac_c2[rsl, r:r + 1] - src_r2[r:r + 1, :ncol], MAX_EXP2))
                blocks.append(_dot((cb[rsl, :ncol] * fac).astype(BF16), xpb[:ncol]))
            full.append(jnp.concatenate(blocks, axis=0))
        yp = jnp.where(lane < B_HEADDIM, full[0], full[1])
        yp = yp + y_inter[:, sl] * _pair_select(e_ac, h0) + d_all[:, sl] * xp
        ys.append(yp)
        xw = (xp * _pair_select(to_end, h0)).astype(BF16)
        s_s[:, sl] = _pair_select(e_last, h0) * state[:, sl] + _dot_tn(bmb, xw)
    y = jnp.concatenate(ys, axis=1)
    y_ref[...] = _group_norm_gate(y, z_ref[...], nw_ref[...]).astype(BF16)

    @pl.when(ci == pl.num_programs(2) - 1)
    def _():
        so_ref[0, 0] = s_s[...]


def _ssd_prompt(p, dt_cols, w, *, nseq, seq_len, c):
    T = p.shape[0]
    nc = seq_len // c
    rows = lambda b, g, ci: b * nc + ci
    kx = B_DINNER // GROUP_CH
    kb = (2 * B_DINNER) // B_DSTATE
    kc = kb + B_GROUPS
    wb = B_DINNER // B_DSTATE
    wc = wb + B_GROUPS
    G = B_GROUPS
    per_lane = lambda width, start=0: pl.BlockSpec((1, width), lambda b, g, ci: (0, start + g))
    return pl.pallas_call(
        functools.partial(_ssd_prompt_kernel, c=c),
        grid=(nseq, G, nc),
        in_specs=[pl.BlockSpec((c, GROUP_CH), lambda b, g, ci: (rows(b, g, ci), kx + g)),
                  pl.BlockSpec((c, B_DSTATE), lambda b, g, ci: (rows(b, g, ci), kb + g)),
                  pl.BlockSpec((c, B_DSTATE), lambda b, g, ci: (rows(b, g, ci), kc + g)),
                  pl.BlockSpec((c, GROUP_CH), lambda b, g, ci: (rows(b, g, ci), g)),
                  pl.BlockSpec((c, LANES), lambda b, g, ci: (rows(b, g, ci), 0)),
                  pl.BlockSpec((None, B_DCONV, GROUP_CH), lambda b, g, ci: (0, 0, g)),
                  pl.BlockSpec((None, B_DCONV, B_DSTATE), lambda b, g, ci: (0, 0, wb + g)),
                  pl.BlockSpec((None, B_DCONV, B_DSTATE), lambda b, g, ci: (0, 0, wc + g)),
                  per_lane(GROUP_CH), per_lane(B_DSTATE, wb), per_lane(B_DSTATE, wc),
                  per_lane(LANES), per_lane(LANES),
                  per_lane(GROUP_CH), per_lane(GROUP_CH)],
        out_specs=[pl.BlockSpec((c, GROUP_CH), lambda b, g, ci: (rows(b, g, ci), g)),
                   pl.BlockSpec((1, 1, B_DSTATE, GROUP_CH), lambda b, g, ci: (b, g, 0, 0))],
        out_shape=[SDS((T, B_DINNER), BF16), SDS((nseq, G, B_DSTATE, GROUP_CH), F32)],
        scratch_shapes=[pltpu.VMEM((B_DSTATE, GROUP_CH), F32),
                        pltpu.VMEM((SUBLANES, GROUP_CH), F32),
                        pltpu.VMEM((SUBLANES, B_DSTATE), F32),
                        pltpu.VMEM((SUBLANES, B_DSTATE), F32)],
        compiler_params=_params(("arbitrary", "arbitrary", "arbitrary")),
        name="ssd_prompt",
    )(p, p, p, p, dt_cols, w["b_conv_w"], w["b_conv_w"], w["b_conv_w"],
      w["b_conv_b"], w["b_conv_b"], w["b_conv_b"],
      w["dtb_cols"], w["alog_cols"], w["d_lane"], w["b_norm_w"])


def _ssd_sample_kernel(x_ref, b_ref, c_ref, z_ref, dt_ref, sx_ref, sb_ref, sc_ref,
                       cwx_ref, cwb_ref, cwc_ref, cbx_ref, cbb_ref, cbc_ref,
                       dtb_ref, al_ref, d_ref, nw_ref, hs_ref,
                       y_ref, hso_ref, *, seq_len):
    nbs = SEQ_GROUP
    rs = nbs * seq_len
    sr = _SampleRows(rs, seq_len)
    prow = lax.broadcasted_iota(jnp.int32, (LANES, 1), 0)
    seq_of_prow = jnp.where(prow < rs, prow % nbs, -1)

    def conv(raw_ref, st_ref, cw_ref, cb_ref):
        raw = raw_ref[...]
        return _silu(_conv_taps(raw, sr.conv_shift(raw, st_ref[...], B_DCONV), cw_ref, cb_ref, B_DCONV))

    x = conv(x_ref, sx_ref, cwx_ref, cbx_ref)
    bm = conv(b_ref, sb_ref, cwb_ref, cbb_ref)
    cm = conv(c_ref, sc_ref, cwc_ref, cbc_ref)

    dt = _softplus(_group_heads_first(dt_ref[...], pl.program_id(1)) + dtb_ref[...])
    a = dt * (-jnp.exp(al_ref[...]))
    acum = a
    total = a
    for k in range(1, seq_len):
        acum = acum + sr.back(a, k)
        total = total + sr.back(a, k) + sr.ahead(a, k)

    coefs = []
    for d in range(seq_len):
        cb_d = jnp.sum(cm * sr.back(bm, d), axis=1, keepdims=True)
        coefs.append(cb_d * jnp.exp(acum - sr.back(acum, d)) * sr.back(dt, d))
    x_back = [sr.back(x, d) for d in range(seq_len)]

    e_ac = jnp.exp(acum)
    e_tot = jnp.exp(total)
    to_end = jnp.exp(total - acum) * dt
    d_all = d_ref[...]

    ys = []
    xws = []
    for pr in range(HEADS_PER_GROUP // 2):
        h0 = 2 * pr
        sl = slice(pr * LANES, (pr + 1) * LANES)
        yp = d_all[:, sl] * x[:, sl]
        for d in range(seq_len):
            yp = yp + _pair_select(coefs[d], h0) * x_back[d][:, sl]
        ys.append(yp)
        xws.append(x[:, sl] * _pair_select(to_end, h0))
    xw_t = _pad_rows(jnp.concatenate(xws, axis=1), LANES).T.astype(BF16)
    cmb = cm.astype(BF16)
    bm_pad = _pad_rows(bm, LANES)
    y_int = jnp.zeros((rs, GROUP_CH), F32)
    for bb in range(nbs):
        hs_g = hs_ref[bb].reshape(GROUP_CH, B_DSTATE)
        y_int = jnp.where(sr.seq == bb, _dot_nt(cmb, hs_g.astype(BF16)), y_int)
        b_mine = jnp.where(seq_of_prow == bb, bm_pad, 0.0).astype(BF16)
        upd = _dot(xw_t, b_mine)
        for r in range(HEADS_PER_GROUP):
            dec = e_tot[bb:bb + 1, r:r + 1]
            hso_ref[bb, r] = dec * hs_ref[bb, r] + upd[r * B_HEADDIM:(r + 1) * B_HEADDIM, :]
    e_pairs = jnp.concatenate([_pair_select(e_ac, 2 * pr) for pr in range(HEADS_PER_GROUP // 2)], axis=1)
    y = jnp.concatenate(ys, axis=1) + y_int * e_pairs
    y_ref[...] = _group_norm_gate(y, z_ref[...], nw_ref[...])


def _ssd_sample(p, dt_cols, conv_state, w, hs, *, seq_len):
    T = p.shape[0]
    nbs = SEQ_GROUP
    rs = nbs * seq_len
    nb = hs.shape[0]
    G = B_GROUPS
    kx = B_DINNER // GROUP_CH
    kb = (2 * B_DINNER) // B_DSTATE
    kc = kb + G
    wb = B_DINNER // B_DSTATE
    wc = wb + G
    per_lane = lambda width, start=0: pl.BlockSpec((1, width), lambda i, g: (0, start + g))
    hs_spec = pl.BlockSpec((nbs, HEADS_PER_GROUP, B_HEADDIM, B_DSTATE), lambda i, g: (i, g, 0, 0))
    return pl.pallas_call(
        functools.partial(_ssd_sample_kernel, seq_len=seq_len),
        grid=(nb // nbs, G),
        in_specs=[pl.BlockSpec((rs, GROUP_CH), lambda i, g: (i, kx + g)),
                  pl.BlockSpec((rs, B_DSTATE), lambda i, g: (i, kb + g)),
                  pl.BlockSpec((rs, B_DSTATE), lambda i, g: (i, kc + g)),
                  pl.BlockSpec((rs, GROUP_CH), lambda i, g: (i, g)),
                  pl.BlockSpec((rs, LANES), lambda i, g: (i, 0)),
                  pl.BlockSpec((rs, GROUP_CH), lambda i, g: (i, g)),
                  pl.BlockSpec((rs, B_DSTATE), lambda i, g: (i, wb + g)),
                  pl.BlockSpec((rs, B_DSTATE), lambda i, g: (i, wc + g)),
                  pl.BlockSpec((None, B_DCONV, GROUP_CH), lambda i, g: (0, 0, g)),
                  pl.BlockSpec((None, B_DCONV, B_DSTATE), lambda i, g: (0, 0, wb + g)),
                  pl.BlockSpec((None, B_DCONV, B_DSTATE), lambda i, g: (0, 0, wc + g)),
                  per_lane(GROUP_CH), per_lane(B_DSTATE, wb), per_lane(B_DSTATE, wc),
                  per_lane(LANES), per_lane(LANES), per_lane(GROUP_CH), per_lane(GROUP_CH),
                  hs_spec],
        out_specs=[pl.BlockSpec((rs, GROUP_CH), lambda i, g: (i, g)), hs_spec],
        out_shape=[SDS((T, B_DINNER), F32), SDS(hs.shape, F32)],
        compiler_params=_params(("arbitrary", "arbitrary")),
        name="ssd_sample",
    )(p, p, p, p, dt_cols, conv_state, conv_state, conv_state,
      w["b_conv_w"], w["b_conv_w"], w["b_conv_w"], w["b_conv_b"], w["b_conv_b"], w["b_conv_b"],
      w["dtb_cols"], w["alog_cols"], w["d_lane"], w["b_norm_w"], hs)


def _pad_lanes(a, width=LANES):
    return jnp.pad(a, ((0, 0), (0, width - a.shape[1])))


def _per_group_lanes(a):
    r = a.shape[0]
    a = a.reshape(r, B_GROUPS, HEADS_PER_GROUP)
    return jnp.pad(a, ((0, 0), (0, 0), (0, LANES - HEADS_PER_GROUP))).reshape(r, B_GROUPS * LANES)


def _to_sample_rows(a, seq_len):
    nseq, nt, ch = a.shape
    a = a.reshape(nseq // SEQ_GROUP, SEQ_GROUP, nt, ch).transpose(0, 2, 1, 3)
    a = jnp.pad(a, ((0, 0), (0, seq_len - nt), (0, 0), (0, 0)))
    return a.reshape(nseq * seq_len, ch)


def _from_sample_rows(a, nseq, seq_len, first_token=0):
    ch = a.shape[1]
    a = a.reshape(nseq // SEQ_GROUP, seq_len, SEQ_GROUP, ch)[:, first_token:]
    return a.transpose(0, 2, 1, 3).reshape(nseq, seq_len - first_token, ch)


def _forward(x, nseq, seq_len, state, w, *, prompt):
    T = x.shape[0]
    tm = 1024 if prompt else T
    a_main = 2 * A_HEADS * A_DK + 2 * A_HEADS * A_DV
    b_main = B_DINNER + B_CONV_DIM
    nstate = FFN_CONV - 1
    cast = {"ffn_bf16": [None, None]}

    outs = _norm_matmul(x, w["norm_mix"], 0, w["a_w_in"], a_main, w["a_w_gate"], tm=tm, tn=1024)
    p_a, gates = outs[:2]
    cast["a_w_in"] = outs[-1]
    if prompt:
        hg, m_c, m_n, m_m = _mlstm_prompt(p_a, gates, w["a_b_row"], nseq=nseq, seq_len=seq_len, c=256)
        m_m = m_m[:, :, 0, 0]
    else:
        c0, n0, m0 = state["mlstm"]
        m_rows = _pad_lanes(_to_sample_rows(jnp.broadcast_to(m0[:, None, :], (nseq, seq_len, A_HEADS)), seq_len))
        hg, m_c, m_n, m_rows_new = _mlstm_sample(p_a, gates, w["a_b_row"], m_rows, c0, n0[:, :, None, :],
                                                 seq_len=seq_len)
        m_m = _from_sample_rows(m_rows_new, nseq, seq_len, seq_len - 1)[:, 0, :A_HEADS]
    m_n = m_n[:, :, 0, :]
    outs = _matmul_res(hg, w["a_w_out"], x, tm=tm, tn=512)
    x, cast["a_w_out"] = outs[0], outs[-1]

    def ffn(x, layer, final_w=None):
        if prompt:
            ftm = 512
            x_new, tg, tv = _ffn_prompt(x, layer, w, tm=ftm, tf=512, seq_len=seq_len, final_w=final_w)
            tps = seq_len // ftm
            pick = lambda t: t.reshape(nseq, tps, SUBLANES, D_FF)[:, tps - 1, SUBLANES - nstate:]
            tails = jnp.concatenate([pick(tg), pick(tv)], axis=-1)
        else:
            st = state["ffn_conv"][layer].reshape(nseq, nstate * 2 * D_FF)
            outs = _ffn_sample(x, layer, w, st, tf=256, seq_len=seq_len, final_w=final_w)
            x_new = outs[0]
            tails = jnp.stack([jnp.concatenate([outs[1 + t], outs[1 + nstate + t]], axis=-1) for t in range(nstate)],
                              axis=1)
            cast["ffn_bf16"][layer] = tuple(outs[1 + 2 * nstate:])
        return x_new, tails

    x, f_conv0 = ffn(x, 0)

    outs = _norm_matmul(x, w["norm_mix"], 1, w["b_w_in"], b_main, w["b_w_dt"], tm=tm, tn=1024)
    p_b, dt_raw = outs[:2]
    cast["b_w_in"] = outs[-1]
    if prompt:
        s_conv = p_b.reshape(nseq, seq_len, b_main)[:, seq_len - (B_DCONV - 1):, B_DINNER:]
        yn, s_t = _ssd_prompt(p_b, dt_raw, w, nseq=nseq, seq_len=seq_len, c=256)
        ssm = s_t.reshape(nseq, B_GROUPS, B_DSTATE, HEADS_PER_GROUP, B_HEADDIM)
        ssm = ssm.transpose(0, 1, 3, 4, 2).reshape(nseq, B_HEADS, B_HEADDIM, B_DSTATE)
    else:
        s_conv = _from_sample_rows(p_b, nseq, seq_len, seq_len - (B_DCONV - 1))[:, :, B_DINNER:]
        st = _to_sample_rows(state["ssm_conv"], seq_len)
        yn, ssm = _ssd_sample(p_b, dt_raw, st, w, state["ssm"], seq_len=seq_len)
    outs = _matmul_res(yn, w["b_w_out"], x, tm=tm, tn=512)
    x, cast["b_w_out"] = outs[0], outs[-1]

    y, f_conv1 = ffn(x, 1, final_w=w["norm_final"])
    return (y, m_c[None], m_n[None], m_m[None], ssm[None], s_conv[None], jnp.stack([f_conv0, f_conv1]), cast)


def kernel(x_prompt, x_sample, state_mlstm_C, state_mlstm_n, state_mlstm_m, state_ssm, state_ssm_conv,
           state_ffn_conv, norm_mix, norm_ffn, norm_final, a_w_in, a_b_if, a_w_out, b_w_in, b_conv_w,
           b_conv_b, b_dt_bias, b_A_log, b_D, b_norm_w, b_w_out, f_w_up, f_conv_w, f_conv_b, f_w_down):
    a_main = 2 * A_HEADS * A_DK + 2 * A_HEADS * A_DV
    b_main = B_DINNER + B_CONV_DIM
    w = {
        "norm_mix": norm_mix[:, None], "norm_ffn": norm_ffn[:, None], "norm_final": norm_final[None],
        "a_w_in": jnp.swapaxes(a_w_in, 1, 2),
        "a_w_gate": _pad_lanes(a_w_in[0, :, a_main:]).astype(BF16),
        "a_b_row": _pad_lanes(a_b_if),
        "a_w_out": a_w_out,
        "b_w_in": jnp.swapaxes(b_w_in, 1, 2),
        "b_w_dt": _pad_lanes(b_w_in[0, :, b_main:]).astype(BF16),
        "b_conv_w": b_conv_w, "b_conv_b": b_conv_b,
        "dtb_cols": _per_group_lanes(b_dt_bias), "alog_cols": _per_group_lanes(b_A_log),
        "d_lane": jnp.repeat(b_D, B_HEADDIM, axis=1), "b_norm_w": b_norm_w,
        "b_w_out": b_w_out,
        "f_w_up": f_w_up, "f_conv_w": f_conv_w, "f_conv_b": f_conv_b[:, None],
        "f_w_down": f_w_down,
    }
    bp, lp, _ = x_prompt.shape
    bs, ls, _ = x_sample.shape
    state = {"mlstm": (state_mlstm_C[0], state_mlstm_n[0], state_mlstm_m[0]),
             "ssm": state_ssm[0], "ssm_conv": state_ssm_conv[0], "ffn_conv": state_ffn_conv}
    ys, sc, sn, sm, sssm, ssconv, sfconv, w_bf16 = _forward(_to_sample_rows(x_sample, ls), bs, ls, state, w,
                                                            prompt=False)
    yp, pc, pn, pm, pssm, psconv, pfconv, _ = _forward(x_prompt.reshape(bp * lp, D_MODEL), bp, lp, None,
                                                       {**w, **w_bf16}, prompt=True)
    return (yp.reshape(bp, lp, D_MODEL), _from_sample_rows(ys, bs, ls), pc, sc, pn, sn, pm, sm,
            pssm, sssm, psconv, ssconv, pfconv, sfconv)
```

```python
import functools

import jax
import jax.numpy as jnp
from jax import lax
from jax.experimental import pallas as pl
from jax.experimental.pallas import tpu as pltpu

F32 = jnp.float32
BF16 = jnp.bfloat16
SDS = jax.ShapeDtypeStruct

D_MODEL = 2048
A_HEADS = 8
A_DK = 128
A_DV = 256
B_DINNER = 4096
B_HEADDIM = 64
B_HEADS = 64
B_GROUPS = 8
B_DSTATE = 128
B_DCONV = 4
B_CONV_DIM = 6144
D_FF = 5632
FFN_CONV = 3
EPS = 1e-6
NEG_INF = float("-inf")
LOG2E = 1.4426950408889634
MAX_EXP2 = 126.0

LANES = 128
SUBLANES = 8
VMEM_LIMIT_BYTES = 56 * 1024 * 1024

HEADS_PER_GROUP = B_HEADS // B_GROUPS
GROUP_CH = HEADS_PER_GROUP * B_HEADDIM
SEQ_GROUP = SUBLANES


def _params(sem, flags=None):
    return pltpu.CompilerParams(dimension_semantics=sem, vmem_limit_bytes=VMEM_LIMIT_BYTES, flags=flags)


def _sigmoid(x):
    return 1.0 / (1.0 + jnp.exp(-x))


def _silu(x):
    h = 0.5 * x
    return h + h * jnp.tanh(h)


def _softplus(x):
    return jnp.maximum(x, 0.0) + jnp.log1p(jnp.exp(-jnp.abs(x)))


def _log_sigmoid(x):
    return -_softplus(-x)


def _rms(x, w):
    ms = jnp.mean(x * x, axis=-1, keepdims=True)
    return x * lax.rsqrt(ms + EPS) * w


def _cumsum_rows(x):
    n = x.shape[0]
    row = lax.broadcasted_iota(jnp.int32, x.shape, 0)
    k = 1
    while k < n:
        x = x + jnp.where(row >= k, pltpu.roll(x, k, 0), 0.0)
        k *= 2
    return x


def _dot(a, b):
    return jnp.dot(a, b, preferred_element_type=F32)


def _dot_nt(a, b):
    return lax.dot_general(a, b, (((1,), (1,)), ((), ())), preferred_element_type=F32)


def _dot_tn(a, b):
    return lax.dot_general(a, b, (((0,), (0,)), ((), ())), preferred_element_type=F32)


def _conv_taps(u, shifted, w_ref, b_ref, ntaps):
    y = u * w_ref[ntaps - 1:ntaps, :]
    for k in range(1, ntaps):
        y = y + shifted(k) * w_ref[ntaps - 1 - k:ntaps - k, :]
    return y + b_ref[...]


def _shift_with_prev(u, prev8, k):
    ext = jnp.concatenate([prev8, u], axis=0)
    return pltpu.roll(ext, k, 0)[SUBLANES:]


def _pad_rows(a, rows):
    if a.shape[0] == rows:
        return a
    return jnp.concatenate([a, jnp.zeros((rows - a.shape[0], a.shape[1]), a.dtype)], axis=0)


class _SampleRows:
    def __init__(self, rows, seq_len):
        self.rows = rows
        self.seq_len = seq_len
        r = lax.broadcasted_iota(jnp.int32, (rows, 1), 0)
        self.token = (r // SEQ_GROUP) % seq_len
        self.seq = r % SEQ_GROUP

    def back(self, a, k):
        if k == 0:
            return a
        return jnp.where(self.token >= k, pltpu.roll(a, SEQ_GROUP * k, 0), 0.0)

    def ahead(self, a, k):
        if k == 0:
            return a
        return jnp.where(self.token + k < self.seq_len, pltpu.roll(a, self.rows - SEQ_GROUP * k, 0), 0.0)

    def conv_shift(self, u, state, ntaps):
        return lambda k: self.back(u, k) + self.ahead(state, ntaps - 1 - k)


def _tokens_to_rows(slots, seq_len):
    nseq, ch = slots[0].shape
    pad = jnp.zeros(((seq_len - len(slots)) * SEQ_GROUP, ch), slots[0].dtype)
    parts = []
    for g in range(nseq // SEQ_GROUP):
        parts += [s[g * SEQ_GROUP:(g + 1) * SEQ_GROUP] for s in slots] + [pad]
    return jnp.concatenate(parts, axis=0)


def _rows_of_token(a, seq_len, t):
    step = seq_len * SEQ_GROUP
    return jnp.concatenate([a[g * step + t * SEQ_GROUP:g * step + (t + 1) * SEQ_GROUP]
                            for g in range(a.shape[0] // step)], axis=0)


def _norm_matmul_kernel(x_ref, nw_ref, w_ref, ws_ref, o_ref, os_ref, *rest):
    hn_ref = rest[-1]

    @pl.when(pl.program_id(1) == 0)
    def _():
        hn = _rms(x_ref[...], nw_ref[...]).astype(BF16)
        hn_ref[...] = hn
        os_ref[...] = _dot(hn, ws_ref[...])

    wb = w_ref[...].astype(BF16)
    if len(rest) == 2:
        rest[0][...] = wb
    o_ref[...] = _dot_nt(hn_ref[...], wb)


def _norm_matmul(x, nw, layer, w_t, n_main, ws, *, tm, tn):
    T, D = x.shape
    emit = w_t.dtype != BF16
    w_spec = pl.BlockSpec((None, tn, D), lambda i, j: (0, j, 0))
    out_specs = [pl.BlockSpec((tm, tn), lambda i, j: (i, j)), pl.BlockSpec((tm, LANES), lambda i, j: (i, 0))]
    out_shape = [SDS((T, n_main), F32), SDS((T, LANES), F32)]
    if emit:
        assert T == tm, "the bfloat16 copy is written once, by a single row tile"
        out_specs.append(w_spec)
        out_shape.append(SDS((1, n_main, D), BF16))
    return pl.pallas_call(
        _norm_matmul_kernel,
        grid=(T // tm, n_main // tn),
        in_specs=[pl.BlockSpec((tm, D), lambda i, j: (i, 0)),
                  pl.BlockSpec((None, 1, D), lambda i, j: (layer, 0, 0)),
                  w_spec,
                  pl.BlockSpec((D, LANES), lambda i, j: (0, 0))],
        out_specs=out_specs,
        out_shape=out_shape,
        scratch_shapes=[pltpu.VMEM((tm, D), BF16)],
        compiler_params=_params(("arbitrary", "arbitrary")),
        name="norm_matmul",
    )(x, nw, w_t, ws)


def _matmul_res_kernel(a_ref, w_ref, r_ref, o_ref, *rest):
    wb = w_ref[...].astype(BF16)
    if rest:
        rest[0][...] = wb
    o_ref[...] = r_ref[...] + _dot(a_ref[...].astype(BF16), wb)


def _matmul_res(a, w, res, *, tm, tn):
    T, K = a.shape
    N = w.shape[2]
    emit = w.dtype != BF16
    w_spec = pl.BlockSpec((None, K, tn), lambda i, j: (0, 0, j))
    out_specs = [pl.BlockSpec((tm, tn), lambda i, j: (i, j))]
    out_shape = [SDS((T, N), F32)]
    if emit:
        assert T == tm, "the bfloat16 copy is written once, by a single row tile"
        out_specs.append(w_spec)
        out_shape.append(SDS((1, K, N), BF16))
    return pl.pallas_call(
        _matmul_res_kernel,
        grid=(T // tm, N // tn),
        in_specs=[pl.BlockSpec((tm, K), lambda i, j: (i, 0)),
                  w_spec,
                  pl.BlockSpec((tm, tn), lambda i, j: (i, j))],
        out_specs=out_specs,
        out_shape=out_shape,
        compiler_params=_params(("arbitrary", "arbitrary")),
        name="matmul_res",
    )(a, w, res)


def _ffn_sample_kernel(*refs, tm, seq_len, final_norm):
    refs = list(refs)
    x_ref, nw_ref, wug_ref, wuv_ref, cwg_ref, cwv_ref, cbg_ref, cbv_ref, wd_ref = refs[:9]
    del refs[:9]
    nstate = FFN_CONV - 1
    sg_refs, sv_refs = refs[:nstate], refs[nstate:2 * nstate]
    del refs[:2 * nstate]
    fw_ref = refs.pop(0) if final_norm else None
    out_ref = refs.pop(0)
    tg_refs, tv_refs = refs[:nstate], refs[nstate:2 * nstate]
    wbg_ref, wbv_ref, wbd_ref = refs[2 * nstate:2 * nstate + 3]
    hn_s, acc_s = refs[2 * nstate + 3:]
    j = pl.program_id(1)

    @pl.when(j == 0)
    def _():
        hn_s[...] = _rms(x_ref[...], nw_ref[...]).astype(BF16)
        acc_s[...] = jnp.zeros_like(acc_s)

    wug = wug_ref[...].astype(BF16)
    wuv = wuv_ref[...].astype(BF16)
    wd = wd_ref[...].astype(BF16)
    hn = hn_s[...]
    ug = _dot(hn, wug)
    uv = _dot(hn, wuv)

    wbg_ref[...] = wug
    wbv_ref[...] = wuv
    wbd_ref[...] = wd
    for t in range(nstate):
        tg_refs[t][...] = _rows_of_token(ug, seq_len, seq_len - nstate + t)
        tv_refs[t][...] = _rows_of_token(uv, seq_len, seq_len - nstate + t)
    sr = _SampleRows(tm, seq_len)
    sg = _tokens_to_rows([r[...] for r in sg_refs], seq_len)
    sv = _tokens_to_rows([r[...] for r in sv_refs], seq_len)
    yg = _conv_taps(ug, sr.conv_shift(ug, sg, FFN_CONV), cwg_ref, cbg_ref, FFN_CONV)
    yv = _conv_taps(uv, sr.conv_shift(uv, sv, FFN_CONV), cwv_ref, cbv_ref, FFN_CONV)

    act = (_silu(yg) * yv).astype(BF16)
    acc_s[...] += _dot(act, wd)

    @pl.when(j == pl.num_programs(1) - 1)
    def _():
        y = x_ref[...] + acc_s[...]
        out_ref[...] = _rms(y, fw_ref[...]) if final_norm else y


def _ffn_sample(x, layer, w, state, *, tf, seq_len, final_w=None):
    T, D = x.shape
    nj = D_FF // tf
    nstate = FFN_CONV - 1
    nseq = T // seq_len
    in_specs = [pl.BlockSpec((T, D), lambda i, j: (0, 0)),
                pl.BlockSpec((None, 1, D), lambda i, j: (layer, 0, 0)),
                pl.BlockSpec((None, D, tf), lambda i, j: (layer, 0, j)),
                pl.BlockSpec((None, D, tf), lambda i, j: (layer, 0, nj + j)),
                pl.BlockSpec((None, FFN_CONV, tf), lambda i, j: (layer, 0, j)),
                pl.BlockSpec((None, FFN_CONV, tf), lambda i, j: (layer, 0, nj + j)),
                pl.BlockSpec((None, 1, tf), lambda i, j: (layer, 0, j)),
                pl.BlockSpec((None, 1, tf), lambda i, j: (layer, 0, nj + j)),
                pl.BlockSpec((None, tf, D), lambda i, j: (layer, j, 0))]
    args = [x, w["norm_ffn"], w["f_w_up"], w["f_w_up"], w["f_conv_w"], w["f_conv_w"], w["f_conv_b"], w["f_conv_b"],
            w["f_w_down"]]
    in_specs += [pl.BlockSpec((nseq, tf), lambda i, j, t=t: (0, t * 2 * nj + j)) for t in range(nstate)]
    in_specs += [pl.BlockSpec((nseq, tf), lambda i, j, t=t: (0, t * 2 * nj + nj + j)) for t in range(nstate)]
    args += [state] * (2 * nstate)
    if final_w is not None:
        in_specs.append(pl.BlockSpec((1, D), lambda i, j: (0, 0)))
        args.append(final_w)
    out_specs = [pl.BlockSpec((T, D), lambda i, j: (0, 0))]
    out_specs += [pl.BlockSpec((nseq, tf), lambda i, j: (0, j))] * (2 * nstate)
    out_specs += [pl.BlockSpec((D, tf), lambda i, j: (0, j))] * 2 + [pl.BlockSpec((tf, D), lambda i, j: (j, 0))]
    out_shape = [SDS((T, D), F32)] + [SDS((nseq, D_FF), F32)] * (2 * nstate)
    out_shape += [SDS((D, D_FF), BF16)] * 2 + [SDS((D_FF, D), BF16)]
    return pl.pallas_call(
        functools.partial(_ffn_sample_kernel, tm=T, seq_len=seq_len, final_norm=final_w is not None),
        grid=(1, nj),
        in_specs=in_specs,
        out_specs=out_specs,
        out_shape=out_shape,
        scratch_shapes=[pltpu.VMEM((T, D), BF16), pltpu.VMEM((T, D), F32)],
        compiler_params=_params(("arbitrary", "arbitrary")),
        name="ffn_sample",
    )(*args)


def _ffn_prompt_kernel(x_ref, nw_ref, wug_ref, wuv_ref, cwg_ref, cwv_ref, cbg_ref, cbv_ref, wd_ref, *rest,
                       tm, tiles_per_seq, nj, final_norm):
    rest = list(rest)
    fw_ref = rest.pop(0) if final_norm else None
    out_ref, tg_ref, tv_ref, hn_s, cg_s, cv_s = rest[:6]
    act_slots = rest[6:8]
    i = pl.program_id(0)
    j = pl.program_id(1)

    def up_act(slot):
        hn = hn_s[...]
        ug = _dot(hn, wug_ref[...])
        uv = _dot(hn, wuv_ref[...])
        tail_g = ug[tm - SUBLANES:, :]
        tail_v = uv[tm - SUBLANES:, :]
        tg_ref[0] = tail_g
        tv_ref[0] = tail_v
        prev_g = cg_s[j]
        prev_v = cv_s[j]
        yg = _conv_taps(ug, lambda k: _shift_with_prev(ug, prev_g, k), cwg_ref, cbg_ref, FFN_CONV)
        yv = _conv_taps(uv, lambda k: _shift_with_prev(uv, prev_v, k), cwv_ref, cbv_ref, FFN_CONV)
        cg_s[j] = tail_g
        cv_s[j] = tail_v
        act_slots[slot][...] = (_silu(yg) * yv).astype(BF16)

    def down_proj(slot):
        out_ref[...] += _dot(act_slots[slot][...], wd_ref[...])

    @pl.when(j == 0)
    def _():
        x = x_ref[...]
        hn_s[...] = _rms(x, nw_ref[...]).astype(BF16)
        out_ref[...] = x

        @pl.when(i % tiles_per_seq == 0)
        def _():
            cg_s[...] = jnp.zeros_like(cg_s)
            cv_s[...] = jnp.zeros_like(cv_s)

        up_act(0)

    steady = (j >= 1) & (j < nj)
    for parity in (0, 1):
        @pl.when(steady & (j % 2 == parity))
        def _():
            up_act(parity)
            down_proj(1 - parity)

    @pl.when(j == nj)
    def _():
        down_proj((nj - 1) % 2)
        if final_norm:
            out_ref[...] = _rms(out_ref[...], fw_ref[...])


def _ffn_prompt(x, layer, w, *, tm, tf, seq_len, final_w=None):
    T, D = x.shape
    nj = D_FF // tf
    ni = T // tm
    assert nj >= 2 and seq_len % tm == 0
    act_j = lambda j: jnp.minimum(j, nj - 1)
    down_j = lambda j: jnp.maximum(j - 1, 0)
    wug, wuv, wd = w["ffn_bf16"][layer]
    in_specs = [pl.BlockSpec((tm, D), lambda i, j: (i, 0), pipeline_mode=pl.Buffered(1)),
                pl.BlockSpec((None, 1, D), lambda i, j: (layer, 0, 0)),
                pl.BlockSpec((D, tf), lambda i, j: (0, act_j(j))),
                pl.BlockSpec((D, tf), lambda i, j: (0, act_j(j))),
                pl.BlockSpec((None, FFN_CONV, tf), lambda i, j: (layer, 0, act_j(j))),
                pl.BlockSpec((None, FFN_CONV, tf), lambda i, j: (layer, 0, nj + act_j(j))),
                pl.BlockSpec((None, 1, tf), lambda i, j: (layer, 0, act_j(j))),
                pl.BlockSpec((None, 1, tf), lambda i, j: (layer, 0, nj + act_j(j))),
                pl.BlockSpec((tf, D), lambda i, j: (down_j(j), 0))]
    args = [x, w["norm_ffn"], wug, wuv, w["f_conv_w"], w["f_conv_w"], w["f_conv_b"], w["f_conv_b"], wd]
    if final_w is not None:
        in_specs.append(pl.BlockSpec((1, D), lambda i, j: (0, 0)))
        args.append(final_w)
    tail_spec = pl.BlockSpec((1, SUBLANES, tf), lambda i, j: (i, 0, act_j(j)))
    return pl.pallas_call(
        functools.partial(_ffn_prompt_kernel, tm=tm, tiles_per_seq=seq_len // tm, nj=nj,
                          final_norm=final_w is not None),
        grid=(ni, nj + 1),
        in_specs=in_specs,
        out_specs=[pl.BlockSpec((tm, D), lambda i, j: (i, 0)), tail_spec, tail_spec],
        out_shape=[SDS((T, D), F32)] + [SDS((ni, SUBLANES, D_FF), F32)] * 2,
        scratch_shapes=[pltpu.VMEM((tm, D), BF16)]
        + [pltpu.VMEM((nj, SUBLANES, tf), F32)] * 2
        + [pltpu.VMEM((tm, tf), BF16)] * 2,
        compiler_params=_params(("arbitrary", "arbitrary")),
        name="ffn_prompt",
    )(*args)


def _mlstm_intra(q, ks, v, ig_col, lf_cum_col, lf_cum_row, ig_row, m_prev_col, causal):
    log_d = jnp.where(causal, lf_cum_col - lf_cum_row + ig_row, NEG_INF)
    inter = lf_cum_col + m_prev_col
    m_t = jnp.maximum(inter, jnp.max(log_d, axis=1, keepdims=True))
    d_m = jnp.exp(log_d - m_t)
    w_inter = jnp.exp(inter - m_t)
    kb = _pad_rows(ks.astype(BF16), causal.shape[1])
    vb = _pad_rows(v.astype(BF16), causal.shape[1])
    s = _dot_nt(q.astype(BF16), kb) * d_m
    num = _dot(s.astype(BF16), vb)
    den = jnp.sum(s, axis=1, keepdims=True)
    return m_t, w_inter, inter, num, den


def _mlstm_prompt_kernel(q_ref, k_ref, v_ref, o_ref, g_ref, br_ref,
                         hg_ref, co_ref, no_ref, mo_ref, c_s, n_s, m_s, *, c):
    ci = pl.program_id(1)

    @pl.when(ci == 0)
    def _():
        c_s[...] = jnp.zeros_like(c_s)
        n_s[...] = jnp.zeros_like(n_s)
        m_s[...] = jnp.zeros_like(m_s)

    g = g_ref[...] + br_ref[...]
    lf_cum = _cumsum_rows(_log_sigmoid(g))
    gt = g.T
    lf_cum_t = lf_cum.T
    row = lax.broadcasted_iota(jnp.int32, (c, c), 0)
    col = lax.broadcasted_iota(jnp.int32, (c, c), 1)
    causal = col <= row
    last = ci == pl.num_programs(1) - 1

    for h in range(A_HEADS):
        ig_col = g[:, h:h + 1]
        b_col = lf_cum[:, A_HEADS + h:A_HEADS + h + 1]
        ig_row = gt[h:h + 1, :]
        b_row = lf_cum_t[A_HEADS + h:A_HEADS + h + 1, :]
        q = q_ref[:, h * A_DK:(h + 1) * A_DK]
        ks = k_ref[:, h * A_DK:(h + 1) * A_DK] * (A_DK ** -0.5)
        v = v_ref[:, h * A_DV:(h + 1) * A_DV]
        m_prev = m_s[h][:, 0:1]
        m_t, w_inter, inter, num, den = _mlstm_intra(q, ks, v, ig_col, b_col, b_row, ig_row, m_prev, causal)
        cmat = c_s[h]
        nvec = n_s[h]
        num = w_inter * _dot(q.astype(BF16), cmat.astype(BF16)) + num
        den = w_inter * jnp.sum(q * nvec, axis=1, keepdims=True) + den
        hh = num / jnp.maximum(jnp.abs(den), jnp.exp(-m_t))
        hg_ref[:, h * A_DV:(h + 1) * A_DV] = (_sigmoid(o_ref[:, h * A_DV:(h + 1) * A_DV]) * hh).astype(BF16)

        m_new = m_t[c - 1:c, :]
        w_state = jnp.exp(b_col[c - 1:c, :] - b_col + ig_col - m_new)
        decay = jnp.exp(inter[c - 1:c, :] - m_new)
        kw = ks * w_state
        c_new = decay * cmat + _dot_tn(kw.astype(BF16), v.astype(BF16))
        n_new = decay * nvec + jnp.sum(kw, axis=0, keepdims=True)
        m_row = jnp.broadcast_to(m_new, (1, LANES))
        c_s[h] = c_new
        n_s[h] = n_new
        m_s[h] = m_row

        @pl.when(last)
        def _():
            co_ref[0, h] = c_new
            no_ref[0, h] = n_new
            mo_ref[0, h] = m_row


def _mlstm_prompt(p, gates, bias_row, *, nseq, seq_len, c):
    T = p.shape[0]
    nc = seq_len // c
    hk = A_HEADS * A_DK
    hv = A_HEADS * A_DV
    rows = lambda b, ci: b * nc + ci
    return pl.pallas_call(
        functools.partial(_mlstm_prompt_kernel, c=c),
        grid=(nseq, nc),
        in_specs=[pl.BlockSpec((c, hk), lambda b, ci: (rows(b, ci), 0)),
                  pl.BlockSpec((c, hk), lambda b, ci: (rows(b, ci), 1)),
                  pl.BlockSpec((c, hv), lambda b, ci: (rows(b, ci), 1)),
                  pl.BlockSpec((c, hv), lambda b, ci: (rows(b, ci), 2)),
                  pl.BlockSpec((c, LANES), lambda b, ci: (rows(b, ci), 0)),
                  pl.BlockSpec((1, LANES), lambda b, ci: (0, 0))],
        out_specs=[pl.BlockSpec((c, hv), lambda b, ci: (rows(b, ci), 0)),
                   pl.BlockSpec((1, A_HEADS, A_DK, A_DV), lambda b, ci: (b, 0, 0, 0)),
                   pl.BlockSpec((1, A_HEADS, 1, A_DK), lambda b, ci: (b, 0, 0, 0)),
                   pl.BlockSpec((1, A_HEADS, 1, LANES), lambda b, ci: (b, 0, 0, 0))],
        out_shape=[SDS((T, hv), BF16),
                   SDS((nseq, A_HEADS, A_DK, A_DV), F32),
                   SDS((nseq, A_HEADS, 1, A_DK), F32),
                   SDS((nseq, A_HEADS, 1, LANES), F32)],
        scratch_shapes=[pltpu.VMEM((A_HEADS, A_DK, A_DV), F32), pltpu.VMEM((A_HEADS, 1, A_DK), F32),
                        pltpu.VMEM((A_HEADS, 1, LANES), F32)],
        compiler_params=_params(("arbitrary", "arbitrary")),
        name="mlstm_prompt",
    )(p, p, p, p, gates, bias_row)


def _mlstm_sample_kernel(q_ref, k_ref, v_ref, o_ref, g_ref, br_ref, mp_ref, ci_ref, ni_ref,
                         hg_ref, co_ref, no_ref, mo_ref, *, seq_len):
    nbs = SEQ_GROUP
    rs = nbs * seq_len
    g = g_ref[...] + br_ref[...]
    lf_all = _log_sigmoid(g)
    m_prev_all = mp_ref[...]
    row = lax.broadcasted_iota(jnp.int32, (rs, LANES), 0)
    col = lax.broadcasted_iota(jnp.int32, (rs, LANES), 1)
    same = ((row % nbs) == (col % nbs)) & (col < rs)
    causal = same & (col <= row)
    causal_t = same & (row <= col)
    eye = row == col
    last = same & (col // nbs == seq_len - 1)
    seq_of_row = lax.broadcasted_iota(jnp.int32, (rs, 1), 0) % nbs
    prow = lax.broadcasted_iota(jnp.int32, (LANES, 1), 0)
    seq_of_prow = jnp.where(prow < rs, prow % nbs, -1)

    def to_row(colvec):
        return jnp.sum(jnp.where(eye, colvec, 0.0), axis=0, keepdims=True)

    def at_last(rowvec):
        return jnp.sum(jnp.where(last, rowvec, 0.0), axis=1, keepdims=True)

    m_out = jnp.zeros((rs, LANES), F32)
    for h in range(A_HEADS):
        ig_col = g[:, h:h + 1]
        lf_col = lf_all[:, A_HEADS + h:A_HEADS + h + 1]
        m_prev = m_prev_all[:, h:h + 1]
        ig_row = to_row(ig_col)
        lf_row = to_row(lf_col)
        b_col = jnp.sum(jnp.where(causal, lf_row, 0.0), axis=1, keepdims=True)
        b_row = jnp.sum(jnp.where(causal_t, lf_col, 0.0), axis=0, keepdims=True)
        q = q_ref[:, h * A_DK:(h + 1) * A_DK]
        ks = k_ref[:, h * A_DK:(h + 1) * A_DK] * (A_DK ** -0.5)
        v = v_ref[:, h * A_DV:(h + 1) * A_DV]
        m_t, w_inter, inter, num, den = _mlstm_intra(q, ks, v, ig_col, b_col, b_row, ig_row, m_prev, causal)
        qb = q.astype(BF16)
        num_i = jnp.zeros((rs, A_DV), F32)
        qn = jnp.zeros((rs, 1), F32)
        for bb in range(nbs):
            mine = seq_of_row == bb
            num_i = jnp.where(mine, _dot(qb, ci_ref[bb, h].astype(BF16)), num_i)
            qn = jnp.where(mine, jnp.sum(q * ni_ref[bb, h], axis=1, keepdims=True), qn)
        num = w_inter * num_i + num
        den = w_inter * qn + den
        hh = num / jnp.maximum(jnp.abs(den), jnp.exp(-m_t))
        hg_ref[:, h * A_DV:(h + 1) * A_DV] = _sigmoid(o_ref[:, h * A_DV:(h + 1) * A_DV]) * hh

        m_new = at_last(to_row(m_t))
        w_state = jnp.exp(at_last(b_row) - b_col + ig_col - m_new)
        decay = jnp.exp(at_last(to_row(inter)) - m_new)
        kw = ks * w_state
        kw_t = _pad_rows(kw, LANES).T.astype(BF16)
        v_pad = _pad_rows(v, LANES)
        for bb in range(nbs):
            v_mine = jnp.where(seq_of_prow == bb, v_pad, 0.0).astype(BF16)
            dec = decay[bb:bb + 1, :]
            co_ref[bb, h] = dec * ci_ref[bb, h] + _dot(kw_t, v_mine)
            no_ref[bb, h] = dec * ni_ref[bb, h] + jnp.sum(jnp.where(seq_of_row == bb, kw, 0.0), axis=0, keepdims=True)
        m_out = jnp.where(col == h, m_new, m_out)
    mo_ref[...] = m_out


def _mlstm_sample(p, gates, bias_row, m_rows, c_in, n_in, *, seq_len):
    T = p.shape[0]
    nbs = SEQ_GROUP
    rs = nbs * seq_len
    nb = c_in.shape[0]
    hk = A_HEADS * A_DK
    hv = A_HEADS * A_DV
    return pl.pallas_call(
        functools.partial(_mlstm_sample_kernel, seq_len=seq_len),
        grid=(nb // nbs,),
        in_specs=[pl.BlockSpec((rs, hk), lambda i: (i, 0)),
                  pl.BlockSpec((rs, hk), lambda i: (i, 1)),
                  pl.BlockSpec((rs, hv), lambda i: (i, 1)),
                  pl.BlockSpec((rs, hv), lambda i: (i, 2)),
                  pl.BlockSpec((rs, LANES), lambda i: (i, 0)),
                  pl.BlockSpec((1, LANES), lambda i: (0, 0)),
                  pl.BlockSpec((rs, LANES), lambda i: (i, 0)),
                  pl.BlockSpec((nbs, A_HEADS, A_DK, A_DV), lambda i: (i, 0, 0, 0)),
                  pl.BlockSpec((nbs, A_HEADS, 1, A_DK), lambda i: (i, 0, 0, 0))],
        out_specs=[pl.BlockSpec((rs, hv), lambda i: (i, 0)),
                   pl.BlockSpec((nbs, A_HEADS, A_DK, A_DV), lambda i: (i, 0, 0, 0)),
                   pl.BlockSpec((nbs, A_HEADS, 1, A_DK), lambda i: (i, 0, 0, 0)),
                   pl.BlockSpec((rs, LANES), lambda i: (i, 0))],
        out_shape=[SDS((T, hv), F32),
                   SDS((nb, A_HEADS, A_DK, A_DV), F32),
                   SDS((nb, A_HEADS, 1, A_DK), F32),
                   SDS((T, LANES), F32)],
        compiler_params=_params(("arbitrary",)),
        name="mlstm_sample",
    )(p, p, p, p, gates, bias_row, m_rows, c_in, n_in)


def _pair_select(a, h0):
    lane = lax.broadcasted_iota(jnp.int32, (a.shape[0], LANES), 1)
    return jnp.where(lane < B_HEADDIM, a[:, h0:h0 + 1], a[:, h0 + 1:h0 + 2])


def _group_heads_first(a, g):
    return pltpu.roll(a, (LANES - HEADS_PER_GROUP * g) % LANES, 1)


def _group_norm_gate(y, z, nw):
    gz = y * _silu(z)
    return gz * lax.rsqrt(jnp.mean(gz * gz, axis=-1, keepdims=True) + EPS) * nw


def _ssd_prompt_kernel(x_ref, b_ref, c_ref, z_ref, dtc_ref,
                       cwx_ref, cwb_ref, cwc_ref, cbx_ref, cbb_ref, cbc_ref,
                       dbr_ref, alr_ref, d_ref, nw_ref,
                       y_ref, so_ref, s_s, xp_s, bp_s, cp_s, *, c):
    ci = pl.program_id(2)

    @pl.when(ci == 0)
    def _():
        s_s[...] = jnp.zeros_like(s_s)
        xp_s[...] = jnp.zeros_like(xp_s)
        bp_s[...] = jnp.zeros_like(bp_s)
        cp_s[...] = jnp.zeros_like(cp_s)

    def conv(raw_ref, prev_s, cw_ref, cb_ref):
        raw = raw_ref[...]
        prev = prev_s[...]
        out = _conv_taps(raw, lambda k: _shift_with_prev(raw, prev, k), cw_ref, cb_ref, B_DCONV)
        prev_s[...] = raw[c - SUBLANES:, :]
        return _silu(out)

    x = conv(x_ref, xp_s, cwx_ref, cbx_ref)
    bm = conv(b_ref, bp_s, cwb_ref, cbb_ref)
    cm = conv(c_ref, cp_s, cwc_ref, cbc_ref)

    dt_c = _softplus(_group_heads_first(dtc_ref[...], pl.program_id(1)) + dbr_ref[...])
    ac_c = _cumsum_rows(dt_c * (-jnp.exp(alr_ref[...])))
    ac_c2 = ac_c * LOG2E
    src_r2 = (ac_c2 - jnp.log(dt_c) * LOG2E).T[:HEADS_PER_GROUP, :]

    row = lax.broadcasted_iota(jnp.int32, (c, c), 0)
    col = lax.broadcasted_iota(jnp.int32, (c, c), 1)
    bmb = bm.astype(BF16)
    cmb = cm.astype(BF16)
    cb = jnp.where(col <= row, _dot_nt(cmb, bmb), 0.0)
    state = s_s[...]
    y_inter = _dot(cmb, state.astype(BF16))
    ac_last = ac_c[c - 1:c, :]
    to_end = jnp.exp(ac_last - ac_c) * dt_c
    e_ac = jnp.exp(ac_c)
    e_last = jnp.exp(ac_last)
    lane = lax.broadcasted_iota(jnp.int32, (c, LANES), 1)
    d_all = d_ref[...]

    ys = []
    for pr in range(HEADS_PER_GROUP // 2):
        h0 = 2 * pr
        sl = slice(pr * LANES, (pr + 1) * LANES)
        xp = x[:, sl]
        xpb = xp.astype(BF16)
        full = []
        for r in (h0, h0 + 1):
            blocks = []
            for rb in range(c // LANES):
                rsl = slice(rb * LANES, (rb + 1) * LANES)
                ncol = (rb + 1) * LANES
                fac = jnp.exp2(jnp.minimum(ac_c2[rsl, r:r + 1] - src_r2[r:r + 1, :ncol], MAX_EXP2))
                blocks.append(_dot((cb[rsl, :ncol] * fac).astype(BF16), xpb[:ncol]))
            full.append(jnp.concatenate(blocks, axis=0))
        yp = jnp.where(lane < B_HEADDIM, full[0], full[1])
        yp = yp + y_inter[:, sl] * _pair_select(e_ac, h0) + d_all[:, sl] * xp
        ys.append(yp)
        xw = (xp * _pair_select(to_end, h0)).astype(BF16)
        s_s[:, sl] = _pair_select(e_last, h0) * state[:, sl] + _dot_tn(bmb, xw)
    y = jnp.concatenate(ys, axis=1)
    y_ref[...] = _group_norm_gate(y, z_ref[...], nw_ref[...]).astype(BF16)

    @pl.when(ci == pl.num_programs(2) - 1)
    def _():
        so_ref[0, 0] = s_s[...]


def _ssd_prompt(p, dt_cols, w, *, nseq, seq_len, c):
    T = p.shape[0]
    nc = seq_len // c
    rows = lambda b, g, ci: b * nc + ci
    kx = B_DINNER // GROUP_CH
    kb = (2 * B_DINNER) // B_DSTATE
    kc = kb + B_GROUPS
    wb = B_DINNER // B_DSTATE
    wc = wb + B_GROUPS
    G = B_GROUPS
    per_lane = lambda width, start=0: pl.BlockSpec((1, width), lambda b, g, ci: (0, start + g))
    return pl.pallas_call(
        functools.partial(_ssd_prompt_kernel, c=c),
        grid=(nseq, G, nc),
        in_specs=[pl.BlockSpec((c, GROUP_CH), lambda b, g, ci: (rows(b, g, ci), kx + g)),
                  pl.BlockSpec((c, B_DSTATE), lambda b, g, ci: (rows(b, g, ci), kb + g)),
                  pl.BlockSpec((c, B_DSTATE), lambda b, g, ci: (rows(b, g, ci), kc + g)),
                  pl.BlockSpec((c, GROUP_CH), lambda b, g, ci: (rows(b, g, ci), g)),
                  pl.BlockSpec((c, LANES), lambda b, g, ci: (rows(b, g, ci), 0)),
                  pl.BlockSpec((None, B_DCONV, GROUP_CH), lambda b, g, ci: (0, 0, g)),
                  pl.BlockSpec((None, B_DCONV, B_DSTATE), lambda b, g, ci: (0, 0, wb + g)),
                  pl.BlockSpec((None, B_DCONV, B_DSTATE), lambda b, g, ci: (0, 0, wc + g)),
                  per_lane(GROUP_CH), per_lane(B_DSTATE, wb), per_lane(B_DSTATE, wc),
                  per_lane(LANES), per_lane(LANES),
                  per_lane(GROUP_CH), per_lane(GROUP_CH)],
        out_specs=[pl.BlockSpec((c, GROUP_CH), lambda b, g, ci: (rows(b, g, ci), g)),
                   pl.BlockSpec((1, 1, B_DSTATE, GROUP_CH), lambda b, g, ci: (b, g, 0, 0))],
        out_shape=[SDS((T, B_DINNER), BF16), SDS((nseq, G, B_DSTATE, GROUP_CH), F32)],
        scratch_shapes=[pltpu.VMEM((B_DSTATE, GROUP_CH), F32),
                        pltpu.VMEM((SUBLANES, GROUP_CH), F32),
                        pltpu.VMEM((SUBLANES, B_DSTATE), F32),
                        pltpu.VMEM((SUBLANES, B_DSTATE), F32)],
        compiler_params=_params(("arbitrary", "arbitrary", "arbitrary")),
        name="ssd_prompt",
    )(p, p, p, p, dt_cols, w["b_conv_w"], w["b_conv_w"], w["b_conv_w"],
      w["b_conv_b"], w["b_conv_b"], w["b_conv_b"],
      w["dtb_cols"], w["alog_cols"], w["d_lane"], w["b_norm_w"])


def _ssd_sample_kernel(x_ref, b_ref, c_ref, z_ref, dt_ref, sx_ref, sb_ref, sc_ref,
                       cwx_ref, cwb_ref, cwc_ref, cbx_ref, cbb_ref, cbc_ref,
                       dtb_ref, al_ref, d_ref, nw_ref, hs_ref,
                       y_ref, hso_ref, *, seq_len):
    nbs = SEQ_GROUP
    rs = nbs * seq_len
    sr = _SampleRows(rs, seq_len)
    prow = lax.broadcasted_iota(jnp.int32, (LANES, 1), 0)
    seq_of_prow = jnp.where(prow < rs, prow % nbs, -1)

    def conv(raw_ref, st_ref, cw_ref, cb_ref):
        raw = raw_ref[...]
        return _silu(_conv_taps(raw, sr.conv_shift(raw, st_ref[...], B_DCONV), cw_ref, cb_ref, B_DCONV))

    x = conv(x_ref, sx_ref, cwx_ref, cbx_ref)
    bm = conv(b_ref, sb_ref, cwb_ref, cbb_ref)
    cm = conv(c_ref, sc_ref, cwc_ref, cbc_ref)

    dt = _softplus(_group_heads_first(dt_ref[...], pl.program_id(1)) + dtb_ref[...])
    a = dt * (-jnp.exp(al_ref[...]))
    acum = a
    total = a
    for k in range(1, seq_len):
        acum = acum + sr.back(a, k)
        total = total + sr.back(a, k) + sr.ahead(a, k)

    coefs = []
    for d in range(seq_len):
        cb_d = jnp.sum(cm * sr.back(bm, d), axis=1, keepdims=True)
        coefs.append(cb_d * jnp.exp(acum - sr.back(acum, d)) * sr.back(dt, d))
    x_back = [sr.back(x, d) for d in range(seq_len)]

    e_ac = jnp.exp(acum)
    e_tot = jnp.exp(total)
    to_end = jnp.exp(total - acum) * dt
    d_all = d_ref[...]

    ys = []
    xws = []
    for pr in range(HEADS_PER_GROUP // 2):
        h0 = 2 * pr
        sl = slice(pr * LANES, (pr + 1) * LANES)
        yp = d_all[:, sl] * x[:, sl]
        for d in range(seq_len):
            yp = yp + _pair_select(coefs[d], h0) * x_back[d][:, sl]
        ys.append(yp)
        xws.append(x[:, sl] * _pair_select(to_end, h0))
    xw_t = _pad_rows(jnp.concatenate(xws, axis=1), LANES).T.astype(BF16)
    cmb = cm.astype(BF16)
    bm_pad = _pad_rows(bm, LANES)
    y_int = jnp.zeros((rs, GROUP_CH), F32)
    for bb in range(nbs):
        hs_g = hs_ref[bb].reshape(GROUP_CH, B_DSTATE)
        y_int = jnp.where(sr.seq == bb, _dot_nt(cmb, hs_g.astype(BF16)), y_int)
        b_mine = jnp.where(seq_of_prow == bb, bm_pad, 0.0).astype(BF16)
        upd = _dot(xw_t, b_mine)
        for r in range(HEADS_PER_GROUP):
            dec = e_tot[bb:bb + 1, r:r + 1]
            hso_ref[bb, r] = dec * hs_ref[bb, r] + upd[r * B_HEADDIM:(r + 1) * B_HEADDIM, :]
    e_pairs = jnp.concatenate([_pair_select(e_ac, 2 * pr) for pr in range(HEADS_PER_GROUP // 2)], axis=1)
    y = jnp.concatenate(ys, axis=1) + y_int * e_pairs
    y_ref[...] = _group_norm_gate(y, z_ref[...], nw_ref[...])


def _ssd_sample(p, dt_cols, conv_state, w, hs, *, seq_len):
    T = p.shape[0]
    nbs = SEQ_GROUP
    rs = nbs * seq_len
    nb = hs.shape[0]
    G = B_GROUPS
    kx = B_DINNER // GROUP_CH
    kb = (2 * B_DINNER) // B_DSTATE
    kc = kb + G
    wb = B_DINNER // B_DSTATE
    wc = wb + G
    per_lane = lambda width, start=0: pl.BlockSpec((1, width), lambda i, g: (0, start + g))
    hs_spec = pl.BlockSpec((nbs, HEADS_PER_GROUP, B_HEADDIM, B_DSTATE), lambda i, g: (i, g, 0, 0))
    return pl.pallas_call(
        functools.partial(_ssd_sample_kernel, seq_len=seq_len),
        grid=(nb // nbs, G),
        in_specs=[pl.BlockSpec((rs, GROUP_CH), lambda i, g: (i, kx + g)),
                  pl.BlockSpec((rs, B_DSTATE), lambda i, g: (i, kb + g)),
                  pl.BlockSpec((rs, B_DSTATE), lambda i, g: (i, kc + g)),
                  pl.BlockSpec((rs, GROUP_CH), lambda i, g: (i, g)),
                  pl.BlockSpec((rs, LANES), lambda i, g: (i, 0)),
                  pl.BlockSpec((rs, GROUP_CH), lambda i, g: (i, g)),
                  pl.BlockSpec((rs, B_DSTATE), lambda i, g: (i, wb + g)),
                  pl.BlockSpec((rs, B_DSTATE), lambda i, g: (i, wc + g)),
                  pl.BlockSpec((None, B_DCONV, GROUP_CH), lambda i, g: (0, 0, g)),
                  pl.BlockSpec((None, B_DCONV, B_DSTATE), lambda i, g: (0, 0, wb + g)),
                  pl.BlockSpec((None, B_DCONV, B_DSTATE), lambda i, g: (0, 0, wc + g)),
                  per_lane(GROUP_CH), per_lane(B_DSTATE, wb), per_lane(B_DSTATE, wc),
                  per_lane(LANES), per_lane(LANES), per_lane(GROUP_CH), per_lane(GROUP_CH),
                  hs_spec],
        out_specs=[pl.BlockSpec((rs, GROUP_CH), lambda i, g: (i, g)), hs_spec],
        out_shape=[SDS((T, B_DINNER), F32), SDS(hs.shape, F32)],
        compiler_params=_params(("arbitrary", "arbitrary")),
        name="ssd_sample",
    )(p, p, p, p, dt_cols, conv_state, conv_state, conv_state,
      w["b_conv_w"], w["b_conv_w"], w["b_conv_w"], w["b_conv_b"], w["b_conv_b"], w["b_conv_b"],
      w["dtb_cols"], w["alog_cols"], w["d_lane"], w["b_norm_w"], hs)


def _pad_lanes(a, width=LANES):
    return jnp.pad(a, ((0, 0), (0, width - a.shape[1])))


def _per_group_lanes(a):
    r = a.shape[0]
    a = a.reshape(r, B_GROUPS, HEADS_PER_GROUP)
    return jnp.pad(a, ((0, 0), (0, 0), (0, LANES - HEADS_PER_GROUP))).reshape(r, B_GROUPS * LANES)


def _to_sample_rows(a, seq_len):
    nseq, nt, ch = a.shape
    a = a.reshape(nseq // SEQ_GROUP, SEQ_GROUP, nt, ch).transpose(0, 2, 1, 3)
    a = jnp.pad(a, ((0, 0), (0, seq_len - nt), (0, 0), (0, 0)))
    return a.reshape(nseq * seq_len, ch)


def _from_sample_rows(a, nseq, seq_len, first_token=0):
    ch = a.shape[1]
    a = a.reshape(nseq // SEQ_GROUP, seq_len, SEQ_GROUP, ch)[:, first_token:]
    return a.transpose(0, 2, 1, 3).reshape(nseq, seq_len - first_token, ch)


def _forward(x, nseq, seq_len, state, w, *, prompt):
    T = x.shape[0]
    tm = 1024 if prompt else T
    a_main = 2 * A_HEADS * A_DK + 2 * A_HEADS * A_DV
    b_main = B_DINNER + B_CONV_DIM
    nstate = FFN_CONV - 1
    cast = {"ffn_bf16": [None, None]}

    outs = _norm_matmul(x, w["norm_mix"], 0, w["a_w_in"], a_main, w["a_w_gate"], tm=tm, tn=1024)
    p_a, gates = outs[:2]
    cast["a_w_in"] = outs[-1]
    if prompt:
        hg, m_c, m_n, m_m = _mlstm_prompt(p_a, gates, w["a_b_row"], nseq=nseq, seq_len=seq_len, c=256)
        m_m = m_m[:, :, 0, 0]
    else:
        c0, n0, m0 = state["mlstm"]
        m_rows = _pad_lanes(_to_sample_rows(jnp.broadcast_to(m0[:, None, :], (nseq, seq_len, A_HEADS)), seq_len))
        hg, m_c, m_n, m_rows_new = _mlstm_sample(p_a, gates, w["a_b_row"], m_rows, c0, n0[:, :, None, :],
                                                 seq_len=seq_len)
        m_m = _from_sample_rows(m_rows_new, nseq, seq_len, seq_len - 1)[:, 0, :A_HEADS]
    m_n = m_n[:, :, 0, :]
    outs = _matmul_res(hg, w["a_w_out"], x, tm=tm, tn=512)
    x, cast["a_w_out"] = outs[0], outs[-1]

    def ffn(x, layer, final_w=None):
        if prompt:
            ftm = 1024
            x_new, tg, tv = _ffn_prompt(x, layer, w, tm=ftm, tf=512, seq_len=seq_len, final_w=final_w)
            tps = seq_len // ftm
            pick = lambda t: t.reshape(nseq, tps, SUBLANES, D_FF)[:, tps - 1, SUBLANES - nstate:]
            tails = jnp.concatenate([pick(tg), pick(tv)], axis=-1)
        else:
            st = state["ffn_conv"][layer].reshape(nseq, nstate * 2 * D_FF)
            outs = _ffn_sample(x, layer, w, st, tf=256, seq_len=seq_len, final_w=final_w)
            x_new = outs[0]
            tails = jnp.stack([jnp.concatenate([outs[1 + t], outs[1 + nstate + t]], axis=-1) for t in range(nstate)],
                              axis=1)
            cast["ffn_bf16"][layer] = tuple(outs[1 + 2 * nstate:])
        return x_new, tails

    x, f_conv0 = ffn(x, 0)

    outs = _norm_matmul(x, w["norm_mix"], 1, w["b_w_in"], b_main, w["b_w_dt"], tm=tm, tn=1024)
    p_b, dt_raw = outs[:2]
    cast["b_w_in"] = outs[-1]
    if prompt:
        s_conv = p_b.reshape(nseq, seq_len, b_main)[:, seq_len - (B_DCONV - 1):, B_DINNER:]
        yn, s_t = _ssd_prompt(p_b, dt_raw, w, nseq=nseq, seq_len=seq_len, c=256)
        ssm = s_t.reshape(nseq, B_GROUPS, B_DSTATE, HEADS_PER_GROUP, B_HEADDIM)
        ssm = ssm.transpose(0, 1, 3, 4, 2).reshape(nseq, B_HEADS, B_HEADDIM, B_DSTATE)
    else:
        s_conv = _from_sample_rows(p_b, nseq, seq_len, seq_len - (B_DCONV - 1))[:, :, B_DINNER:]
        st = _to_sample_rows(state["ssm_conv"], seq_len)
        yn, ssm = _ssd_sample(p_b, dt_raw, st, w, state["ssm"], seq_len=seq_len)
    outs = _matmul_res(yn, w["b_w_out"], x, tm=tm, tn=512)
    x, cast["b_w_out"] = outs[0], outs[-1]

    y, f_conv1 = ffn(x, 1, final_w=w["norm_final"])
    return (y, m_c[None], m_n[None], m_m[None], ssm[None], s_conv[None], jnp.stack([f_conv0, f_conv1]), cast)


def kernel(x_prompt, x_sample, state_mlstm_C, state_mlstm_n, state_mlstm_m, state_ssm, state_ssm_conv,
           state_ffn_conv, norm_mix, norm_ffn, norm_final, a_w_in, a_b_if, a_w_out, b_w_in, b_conv_w,
           b_conv_b, b_dt_bias, b_A_log, b_D, b_norm_w, b_w_out, f_w_up, f_conv_w, f_conv_b, f_w_down):
    a_main = 2 * A_HEADS * A_DK + 2 * A_HEADS * A_DV
    b_main = B_DINNER + B_CONV_DIM
    w = {
        "norm_mix": norm_mix[:, None], "norm_ffn": norm_ffn[:, None], "norm_final": norm_final[None],
        "a_w_in": jnp.swapaxes(a_w_in, 1, 2),
        "a_w_gate": _pad_lanes(a_w_in[0, :, a_main:]).astype(BF16),
        "a_b_row": _pad_lanes(a_b_if),
        "a_w_out": a_w_out,
        "b_w_in": jnp.swapaxes(b_w_in, 1, 2),
        "b_w_dt": _pad_lanes(b_w_in[0, :, b_main:]).astype(BF16),
        "b_conv_w": b_conv_w, "b_conv_b": b_conv_b,
        "dtb_cols": _per_group_lanes(b_dt_bias), "alog_cols": _per_group_lanes(b_A_log),
        "d_lane": jnp.repeat(b_D, B_HEADDIM, axis=1), "b_norm_w": b_norm_w,
        "b_w_out": b_w_out,
        "f_w_up": f_w_up, "f_conv_w": f_conv_w, "f_conv_b": f_conv_b[:, None],
        "f_w_down": f_w_down,
    }
    bp, lp, _ = x_prompt.shape
    bs, ls, _ = x_sample.shape
    state = {"mlstm": (state_mlstm_C[0], state_mlstm_n[0], state_mlstm_m[0]),
             "ssm": state_ssm[0], "ssm_conv": state_ssm_conv[0], "ffn_conv": state_ffn_conv}
    ys, sc, sn, sm, sssm, ssconv, sfconv, w_bf16 = _forward(_to_sample_rows(x_sample, ls), bs, ls, state, w,
                                                            prompt=False)
    yp, pc, pn, pm, pssm, psconv, pfconv, _ = _forward(x_prompt.reshape(bp * lp, D_MODEL), bp, lp, None,
                                                       {**w, **w_bf16}, prompt=True)
    return (yp.reshape(bp, lp, D_MODEL), _from_sample_rows(ys, bs, ls), pc, sc, pn, sn, pm, sm,
            pssm, sssm, psconv, ssconv, pfconv, sfconv)
```

```python
import functools

import jax
import jax.numpy as jnp
from jax import lax
from jax.experimental import pallas as pl
from jax.experimental.pallas import tpu as pltpu

F32 = jnp.float32
BF16 = jnp.bfloat16
SDS = jax.ShapeDtypeStruct

D_MODEL = 2048
A_HEADS = 8
A_DK = 128
A_DV = 256
B_DINNER = 4096
B_HEADDIM = 64
B_HEADS = 64
B_GROUPS = 8
B_DSTATE = 128
B_DCONV = 4
B_CONV_DIM = 6144
D_FF = 5632
FFN_CONV = 3
EPS = 1e-6
NEG_INF = float("-inf")
LOG2E = 1.4426950408889634
MAX_EXP2 = 126.0

LANES = 128
SUBLANES = 8
VMEM_LIMIT_BYTES = 56 * 1024 * 1024

HEADS_PER_GROUP = B_HEADS // B_GROUPS
GROUP_CH = HEADS_PER_GROUP * B_HEADDIM
SEQ_GROUP = SUBLANES


def _params(sem, flags=None):
    return pltpu.CompilerParams(dimension_semantics=sem, vmem_limit_bytes=VMEM_LIMIT_BYTES, flags=flags)


def _sigmoid(x):
    return 1.0 / (1.0 + jnp.exp(-x))


def _silu(x):
    h = 0.5 * x
    return h + h * jnp.tanh(h)


def _softplus(x):
    return jnp.maximum(x, 0.0) + jnp.log1p(jnp.exp(-jnp.abs(x)))


def _log_sigmoid(x):
    return -_softplus(-x)


def _rms(x, w):
    ms = jnp.mean(x * x, axis=-1, keepdims=True)
    return x * lax.rsqrt(ms + EPS) * w


def _cumsum_rows(x):
    n = x.shape[0]
    row = lax.broadcasted_iota(jnp.int32, x.shape, 0)
    k = 1
    while k < n:
        x = x + jnp.where(row >= k, pltpu.roll(x, k, 0), 0.0)
        k *= 2
    return x


def _dot(a, b):
    return jnp.dot(a, b, preferred_element_type=F32)


def _dot_nt(a, b):
    return lax.dot_general(a, b, (((1,), (1,)), ((), ())), preferred_element_type=F32)


def _dot_tn(a, b):
    return lax.dot_general(a, b, (((0,), (0,)), ((), ())), preferred_element_type=F32)


def _conv_taps(u, shifted, w_ref, b_ref, ntaps):
    y = u * w_ref[ntaps - 1:ntaps, :]
    for k in range(1, ntaps):
        y = y + shifted(k) * w_ref[ntaps - 1 - k:ntaps - k, :]
    return y + b_ref[...]


def _shift_with_prev(u, prev8, k):
    ext = jnp.concatenate([prev8, u], axis=0)
    return pltpu.roll(ext, k, 0)[SUBLANES:]


def _pad_rows(a, rows):
    if a.shape[0] == rows:
        return a
    return jnp.concatenate([a, jnp.zeros((rows - a.shape[0], a.shape[1]), a.dtype)], axis=0)


class _SampleRows:
    def __init__(self, rows, seq_len):
        self.rows = rows
        self.seq_len = seq_len
        r = lax.broadcasted_iota(jnp.int32, (rows, 1), 0)
        self.token = (r // SEQ_GROUP) % seq_len
        self.seq = r % SEQ_GROUP

    def back(self, a, k):
        if k == 0:
            return a
        return jnp.where(self.token >= k, pltpu.roll(a, SEQ_GROUP * k, 0), 0.0)

    def ahead(self, a, k):
        if k == 0:
            return a
        return jnp.where(self.token + k < self.seq_len, pltpu.roll(a, self.rows - SEQ_GROUP * k, 0), 0.0)

    def conv_shift(self, u, state, ntaps):
        return lambda k: self.back(u, k) + self.ahead(state, ntaps - 1 - k)


def _tokens_to_rows(slots, seq_len):
    nseq, ch = slots[0].shape
    pad = jnp.zeros(((seq_len - len(slots)) * SEQ_GROUP, ch), slots[0].dtype)
    parts = []
    for g in range(nseq // SEQ_GROUP):
        parts += [s[g * SEQ_GROUP:(g + 1) * SEQ_GROUP] for s in slots] + [pad]
    return jnp.concatenate(parts, axis=0)


def _rows_of_token(a, seq_len, t):
    step = seq_len * SEQ_GROUP
    return jnp.concatenate([a[g * step + t * SEQ_GROUP:g * step + (t + 1) * SEQ_GROUP]
                            for g in range(a.shape[0] // step)], axis=0)


def _norm_matmul_kernel(x_ref, nw_ref, w_ref, ws_ref, o_ref, os_ref, *rest):
    hn_ref = rest[-1]

    @pl.when(pl.program_id(1) == 0)
    def _():
        hn = _rms(x_ref[...], nw_ref[...]).astype(BF16)
        hn_ref[...] = hn
        os_ref[...] = _dot(hn, ws_ref[...])

    wb = w_ref[...].astype(BF16)
    if len(rest) == 2:
        rest[0][...] = wb
    o_ref[...] = _dot_nt(hn_ref[...], wb)


def _norm_matmul(x, nw, layer, w_t, n_main, ws, *, tm, tn):
    T, D = x.shape
    emit = w_t.dtype != BF16
    w_spec = pl.BlockSpec((None, tn, D), lambda i, j: (0, j, 0))
    out_specs = [pl.BlockSpec((tm, tn), lambda i, j: (i, j)), pl.BlockSpec((tm, LANES), lambda i, j: (i, 0))]
    out_shape = [SDS((T, n_main), F32), SDS((T, LANES), F32)]
    if emit:
        assert T == tm, "the bfloat16 copy is written once, by a single row tile"
        out_specs.append(w_spec)
        out_shape.append(SDS((1, n_main, D), BF16))
    return pl.pallas_call(
        _norm_matmul_kernel,
        grid=(T // tm, n_main // tn),
        in_specs=[pl.BlockSpec((tm, D), lambda i, j: (i, 0)),
                  pl.BlockSpec((None, 1, D), lambda i, j: (layer, 0, 0)),
                  w_spec,
                  pl.BlockSpec((D, LANES), lambda i, j: (0, 0))],
        out_specs=out_specs,
        out_shape=out_shape,
        scratch_shapes=[pltpu.VMEM((tm, D), BF16)],
        compiler_params=_params(("arbitrary", "arbitrary")),
        name="norm_matmul",
    )(x, nw, w_t, ws)


def _matmul_res_kernel(a_ref, w_ref, r_ref, o_ref, *rest):
    wb = w_ref[...].astype(BF16)
    if rest:
        rest[0][...] = wb
    o_ref[...] = r_ref[...] + _dot(a_ref[...].astype(BF16), wb)


def _matmul_res(a, w, res, *, tm, tn):
    T, K = a.shape
    N = w.shape[2]
    emit = w.dtype != BF16
    w_spec = pl.BlockSpec((None, K, tn), lambda i, j: (0, 0, j))
    out_specs = [pl.BlockSpec((tm, tn), lambda i, j: (i, j))]
    out_shape = [SDS((T, N), F32)]
    if emit:
        assert T == tm, "the bfloat16 copy is written once, by a single row tile"
        out_specs.append(w_spec)
        out_shape.append(SDS((1, K, N), BF16))
    return pl.pallas_call(
        _matmul_res_kernel,
        grid=(T // tm, N // tn),
        in_specs=[pl.BlockSpec((tm, K), lambda i, j: (i, 0)),
                  w_spec,
                  pl.BlockSpec((tm, tn), lambda i, j: (i, j))],
        out_specs=out_specs,
        out_shape=out_shape,
        compiler_params=_params(("arbitrary", "arbitrary")),
        name="matmul_res",
    )(a, w, res)


def _ffn_sample_kernel(*refs, tm, seq_len, final_norm):
    refs = list(refs)
    x_ref, nw_ref, wug_ref, wuv_ref, cwg_ref, cwv_ref, cbg_ref, cbv_ref, wd_ref = refs[:9]
    del refs[:9]
    nstate = FFN_CONV - 1
    sg_ref, sv_ref = refs[:2]
    del refs[:2]
    fw_ref = refs.pop(0) if final_norm else None
    out_ref = refs.pop(0)
    tg_ref, tv_ref, wbg_ref, wbv_ref, wbd_ref, hn_s, acc_s = refs
    j = pl.program_id(1)

    @pl.when(j == 0)
    def _():
        hn_s[...] = _rms(x_ref[...], nw_ref[...]).astype(BF16)
        acc_s[...] = jnp.zeros_like(acc_s)

    wug = wug_ref[...].astype(BF16)
    wuv = wuv_ref[...].astype(BF16)
    wd = wd_ref[...].astype(BF16)
    hn = hn_s[...]
    ug = _dot(hn, wug)
    uv = _dot(hn, wuv)

    wbg_ref[...] = wug
    wbv_ref[...] = wuv
    wbd_ref[...] = wd
    for t in range(nstate):
        tg_ref[:, t, :] = _rows_of_token(ug, seq_len, seq_len - nstate + t)
        tv_ref[:, t, :] = _rows_of_token(uv, seq_len, seq_len - nstate + t)
    sr = _SampleRows(tm, seq_len)
    sg = _tokens_to_rows([sg_ref[:, t, :] for t in range(nstate)], seq_len)
    sv = _tokens_to_rows([sv_ref[:, t, :] for t in range(nstate)], seq_len)
    yg = _conv_taps(ug, sr.conv_shift(ug, sg, FFN_CONV), cwg_ref, cbg_ref, FFN_CONV)
    yv = _conv_taps(uv, sr.conv_shift(uv, sv, FFN_CONV), cwv_ref, cbv_ref, FFN_CONV)

    act = (_silu(yg) * yv).astype(BF16)
    acc_s[...] += _dot(act, wd)

    @pl.when(j == pl.num_programs(1) - 1)
    def _():
        y = x_ref[...] + acc_s[...]
        out_ref[...] = _rms(y, fw_ref[...]) if final_norm else y


def _ffn_sample(x, layer, w, state, *, tf, seq_len, final_w=None):
    T, D = x.shape
    nj = D_FF // tf
    nstate = FFN_CONV - 1
    nseq = T // seq_len
    in_specs = [pl.BlockSpec((T, D), lambda i, j: (0, 0)),
                pl.BlockSpec((None, 1, D), lambda i, j: (layer, 0, 0)),
                pl.BlockSpec((None, D, tf), lambda i, j: (layer, 0, j)),
                pl.BlockSpec((None, D, tf), lambda i, j: (layer, 0, nj + j)),
                pl.BlockSpec((None, FFN_CONV, tf), lambda i, j: (layer, 0, j)),
                pl.BlockSpec((None, FFN_CONV, tf), lambda i, j: (layer, 0, nj + j)),
                pl.BlockSpec((None, 1, tf), lambda i, j: (layer, 0, j)),
                pl.BlockSpec((None, 1, tf), lambda i, j: (layer, 0, nj + j)),
                pl.BlockSpec((None, tf, D), lambda i, j: (layer, j, 0))]
    args = [x, w["norm_ffn"], w["f_w_up"], w["f_w_up"], w["f_conv_w"], w["f_conv_w"], w["f_conv_b"], w["f_conv_b"],
            w["f_w_down"]]
    in_specs += [pl.BlockSpec((None, nseq, nstate, tf), lambda i, j: (layer, 0, 0, j)),
                 pl.BlockSpec((None, nseq, nstate, tf), lambda i, j: (layer, 0, 0, nj + j))]
    args += [state, state]
    if final_w is not None:
        in_specs.append(pl.BlockSpec((1, D), lambda i, j: (0, 0)))
        args.append(final_w)
    out_specs = [pl.BlockSpec((T, D), lambda i, j: (0, 0))]
    out_specs += [pl.BlockSpec((nseq, nstate, tf), lambda i, j: (0, 0, j))] * 2
    out_specs += [pl.BlockSpec((D, tf), lambda i, j: (0, j))] * 2 + [pl.BlockSpec((tf, D), lambda i, j: (j, 0))]
    out_shape = [SDS((T, D), F32)] + [SDS((nseq, nstate, D_FF), F32)] * 2
    out_shape += [SDS((D, D_FF), BF16)] * 2 + [SDS((D_FF, D), BF16)]
    return pl.pallas_call(
        functools.partial(_ffn_sample_kernel, tm=T, seq_len=seq_len, final_norm=final_w is not None),
        grid=(1, nj),
        in_specs=in_specs,
        out_specs=out_specs,
        out_shape=out_shape,
        scratch_shapes=[pltpu.VMEM((T, D), BF16), pltpu.VMEM((T, D), F32)],
        compiler_params=_params(("arbitrary", "arbitrary")),
        name="ffn_sample",
    )(*args)


def _ffn_prompt_kernel(x_ref, nw_ref, wug_ref, wuv_ref, cwg_ref, cwv_ref, cbg_ref, cbv_ref, wd_ref, *rest,
                       tm, tiles_per_seq, nj, final_norm):
    rest = list(rest)
    fw_ref = rest.pop(0) if final_norm else None
    out_ref, tg_ref, tv_ref, hn_s, cg_s, cv_s = rest[:6]
    act_slots = rest[6:8]
    i = pl.program_id(0)
    j = pl.program_id(1)

    def up_act(slot):
        hn = hn_s[...]
        ug = _dot(hn, wug_ref[...])
        uv = _dot(hn, wuv_ref[...])
        tail_g = ug[tm - SUBLANES:, :]
        tail_v = uv[tm - SUBLANES:, :]
        tg_ref[0] = tail_g
        tv_ref[0] = tail_v
        prev_g = cg_s[j]
        prev_v = cv_s[j]
        yg = _conv_taps(ug, lambda k: _shift_with_prev(ug, prev_g, k), cwg_ref, cbg_ref, FFN_CONV)
        yv = _conv_taps(uv, lambda k: _shift_with_prev(uv, prev_v, k), cwv_ref, cbv_ref, FFN_CONV)
        cg_s[j] = tail_g
        cv_s[j] = tail_v
        act_slots[slot][...] = (_silu(yg) * yv).astype(BF16)

    def down_proj(slot):
        out_ref[...] += _dot(act_slots[slot][...], wd_ref[...])

    @pl.when(j == 0)
    def _():
        x = x_ref[...]
        hn_s[...] = _rms(x, nw_ref[...]).astype(BF16)
        out_ref[...] = x

        @pl.when(i % tiles_per_seq == 0)
        def _():
            cg_s[...] = jnp.zeros_like(cg_s)
            cv_s[...] = jnp.zeros_like(cv_s)

        up_act(0)

    steady = (j >= 1) & (j < nj)
    for parity in (0, 1):
        @pl.when(steady & (j % 2 == parity))
        def _():
            up_act(parity)
            down_proj(1 - parity)

    @pl.when(j == nj)
    def _():
        down_proj((nj - 1) % 2)
        if final_norm:
            out_ref[...] = _rms(out_ref[...], fw_ref[...])


def _ffn_prompt(x, layer, w, *, tm, tf, seq_len, final_w=None):
    T, D = x.shape
    nj = D_FF // tf
    ni = T // tm
    assert nj >= 2 and seq_len % tm == 0
    act_j = lambda j: jnp.minimum(j, nj - 1)
    down_j = lambda j: jnp.maximum(j - 1, 0)
    wug, wuv, wd = w["ffn_bf16"][layer]
    in_specs = [pl.BlockSpec((tm, D), lambda i, j: (i, 0), pipeline_mode=pl.Buffered(1)),
                pl.BlockSpec((None, 1, D), lambda i, j: (layer, 0, 0)),
                pl.BlockSpec((D, tf), lambda i, j: (0, act_j(j))),
                pl.BlockSpec((D, tf), lambda i, j: (0, act_j(j))),
                pl.BlockSpec((None, FFN_CONV, tf), lambda i, j: (layer, 0, act_j(j))),
                pl.BlockSpec((None, FFN_CONV, tf), lambda i, j: (layer, 0, nj + act_j(j))),
                pl.BlockSpec((None, 1, tf), lambda i, j: (layer, 0, act_j(j))),
                pl.BlockSpec((None, 1, tf), lambda i, j: (layer, 0, nj + act_j(j))),
                pl.BlockSpec((tf, D), lambda i, j: (down_j(j), 0))]
    args = [x, w["norm_ffn"], wug, wuv, w["f_conv_w"], w["f_conv_w"], w["f_conv_b"], w["f_conv_b"], wd]
    if final_w is not None:
        in_specs.append(pl.BlockSpec((1, D), lambda i, j: (0, 0)))
        args.append(final_w)
    tail_spec = pl.BlockSpec((1, SUBLANES, tf), lambda i, j: (i, 0, act_j(j)))
    return pl.pallas_call(
        functools.partial(_ffn_prompt_kernel, tm=tm, tiles_per_seq=seq_len // tm, nj=nj,
                          final_norm=final_w is not None),
        grid=(ni, nj + 1),
        in_specs=in_specs,
        out_specs=[pl.BlockSpec((tm, D), lambda i, j: (i, 0)), tail_spec, tail_spec],
        out_shape=[SDS((T, D), F32)] + [SDS((ni, SUBLANES, D_FF), F32)] * 2,
        scratch_shapes=[pltpu.VMEM((tm, D), BF16)]
        + [pltpu.VMEM((nj, SUBLANES, tf), F32)] * 2
        + [pltpu.VMEM((tm, tf), BF16)] * 2,
        compiler_params=_params(("arbitrary", "arbitrary")),
        name="ffn_prompt",
    )(*args)


def _mlstm_intra(q, ks, v, ig_col, lf_cum_col, lf_cum_row, ig_row, m_prev_col, causal):
    log_d = jnp.where(causal, lf_cum_col - lf_cum_row + ig_row, NEG_INF)
    inter = lf_cum_col + m_prev_col
    m_t = jnp.maximum(inter, jnp.max(log_d, axis=1, keepdims=True))
    d_m = jnp.exp(log_d - m_t)
    w_inter = jnp.exp(inter - m_t)
    kb = _pad_rows(ks.astype(BF16), causal.shape[1])
    vb = _pad_rows(v.astype(BF16), causal.shape[1])
    s = _dot_nt(q.astype(BF16), kb) * d_m
    num = _dot(s.astype(BF16), vb)
    den = jnp.sum(s, axis=1, keepdims=True)
    return m_t, w_inter, inter, num, den


def _mlstm_prompt_kernel(q_ref, k_ref, v_ref, o_ref, g_ref, br_ref,
                         hg_ref, co_ref, no_ref, mo_ref, c_s, n_s, m_s, *, c):
    ci = pl.program_id(1)

    @pl.when(ci == 0)
    def _():
        c_s[...] = jnp.zeros_like(c_s)
        n_s[...] = jnp.zeros_like(n_s)
        m_s[...] = jnp.zeros_like(m_s)

    g = g_ref[...] + br_ref[...]
    lf_cum = _cumsum_rows(_log_sigmoid(g))
    gt = g.T
    lf_cum_t = lf_cum.T
    row = lax.broadcasted_iota(jnp.int32, (c, c), 0)
    col = lax.broadcasted_iota(jnp.int32, (c, c), 1)
    causal = col <= row
    last = ci == pl.num_programs(1) - 1

    for h in range(A_HEADS):
        ig_col = g[:, h:h + 1]
        b_col = lf_cum[:, A_HEADS + h:A_HEADS + h + 1]
        ig_row = gt[h:h + 1, :]
        b_row = lf_cum_t[A_HEADS + h:A_HEADS + h + 1, :]
        q = q_ref[:, h * A_DK:(h + 1) * A_DK]
        ks = k_ref[:, h * A_DK:(h + 1) * A_DK] * (A_DK ** -0.5)
        v = v_ref[:, h * A_DV:(h + 1) * A_DV]
        m_prev = m_s[h][:, 0:1]
        m_t, w_inter, inter, num, den = _mlstm_intra(q, ks, v, ig_col, b_col, b_row, ig_row, m_prev, causal)
        cmat = c_s[h]
        nvec = n_s[h]
        num = w_inter * _dot(q.astype(BF16), cmat.astype(BF16)) + num
        den = w_inter * jnp.sum(q * nvec, axis=1, keepdims=True) + den
        hh = num / jnp.maximum(jnp.abs(den), jnp.exp(-m_t))
        hg_ref[:, h * A_DV:(h + 1) * A_DV] = (_sigmoid(o_ref[:, h * A_DV:(h + 1) * A_DV]) * hh).astype(BF16)

        m_new = m_t[c - 1:c, :]
        w_state = jnp.exp(b_col[c - 1:c, :] - b_col + ig_col - m_new)
        decay = jnp.exp(inter[c - 1:c, :] - m_new)
        kw = ks * w_state
        c_new = decay * cmat + _dot_tn(kw.astype(BF16), v.astype(BF16))
        n_new = decay * nvec + jnp.sum(kw, axis=0, keepdims=True)
        m_row = jnp.broadcast_to(m_new, (1, LANES))
        c_s[h] = c_new
        n_s[h] = n_new
        m_s[h] = m_row

        @pl.when(last)
        def _():
            co_ref[0, h] = c_new
            no_ref[0, h] = n_new
            mo_ref[0, h] = m_row


def _mlstm_prompt(p, gates, bias_row, *, nseq, seq_len, c):
    T = p.shape[0]
    nc = seq_len // c
    hk = A_HEADS * A_DK
    hv = A_HEADS * A_DV
    rows = lambda b, ci: b * nc + ci
    return pl.pallas_call(
        functools.partial(_mlstm_prompt_kernel, c=c),
        grid=(nseq, nc),
        in_specs=[pl.BlockSpec((c, hk), lambda b, ci: (rows(b, ci), 0)),
                  pl.BlockSpec((c, hk), lambda b, ci: (rows(b, ci), 1)),
                  pl.BlockSpec((c, hv), lambda b, ci: (rows(b, ci), 1)),
                  pl.BlockSpec((c, hv), lambda b, ci: (rows(b, ci), 2)),
                  pl.BlockSpec((c, LANES), lambda b, ci: (rows(b, ci), 0)),
                  pl.BlockSpec((1, LANES), lambda b, ci: (0, 0))],
        out_specs=[pl.BlockSpec((c, hv), lambda b, ci: (rows(b, ci), 0)),
                   pl.BlockSpec((1, A_HEADS, A_DK, A_DV), lambda b, ci: (b, 0, 0, 0)),
                   pl.BlockSpec((1, A_HEADS, 1, A_DK), lambda b, ci: (b, 0, 0, 0)),
                   pl.BlockSpec((1, A_HEADS, 1, LANES), lambda b, ci: (b, 0, 0, 0))],
        out_shape=[SDS((T, hv), BF16),
                   SDS((nseq, A_HEADS, A_DK, A_DV), F32),
                   SDS((nseq, A_HEADS, 1, A_DK), F32),
                   SDS((nseq, A_HEADS, 1, LANES), F32)],
        scratch_shapes=[pltpu.VMEM((A_HEADS, A_DK, A_DV), F32), pltpu.VMEM((A_HEADS, 1, A_DK), F32),
                        pltpu.VMEM((A_HEADS, 1, LANES), F32)],
        compiler_params=_params(("arbitrary", "arbitrary")),
        name="mlstm_prompt",
    )(p, p, p, p, gates, bias_row)


def _mlstm_sample_kernel(q_ref, k_ref, v_ref, o_ref, g_ref, br_ref, mp_ref, ci_ref, ni_ref,
                         hg_ref, co_ref, no_ref, mo_ref, *, seq_len):
    nbs = SEQ_GROUP
    rs = nbs * seq_len
    g = g_ref[...] + br_ref[...]
    lf_all = _log_sigmoid(g)
    m_prev_all = mp_ref[...]
    row = lax.broadcasted_iota(jnp.int32, (rs, LANES), 0)
    col = lax.broadcasted_iota(jnp.int32, (rs, LANES), 1)
    same = ((row % nbs) == (col % nbs)) & (col < rs)
    causal = same & (col <= row)
    causal_t = same & (row <= col)
    eye = row == col
    last = same & (col // nbs == seq_len - 1)
    seq_of_row = lax.broadcasted_iota(jnp.int32, (rs, 1), 0) % nbs
    prow = lax.broadcasted_iota(jnp.int32, (LANES, 1), 0)
    seq_of_prow = jnp.where(prow < rs, prow % nbs, -1)

    def to_row(colvec):
        return jnp.sum(jnp.where(eye, colvec, 0.0), axis=0, keepdims=True)

    def at_last(rowvec):
        return jnp.sum(jnp.where(last, rowvec, 0.0), axis=1, keepdims=True)

    m_out = jnp.zeros((rs, LANES), F32)
    for h in range(A_HEADS):
        ig_col = g[:, h:h + 1]
        lf_col = lf_all[:, A_HEADS + h:A_HEADS + h + 1]
        m_prev = m_prev_all[:, h:h + 1]
        ig_row = to_row(ig_col)
        lf_row = to_row(lf_col)
        b_col = jnp.sum(jnp.where(causal, lf_row, 0.0), axis=1, keepdims=True)
        b_row = jnp.sum(jnp.where(causal_t, lf_col, 0.0), axis=0, keepdims=True)
        q = q_ref[:, h * A_DK:(h + 1) * A_DK]
        ks = k_ref[:, h * A_DK:(h + 1) * A_DK] * (A_DK ** -0.5)
        v = v_ref[:, h * A_DV:(h + 1) * A_DV]
        m_t, w_inter, inter, num, den = _mlstm_intra(q, ks, v, ig_col, b_col, b_row, ig_row, m_prev, causal)
        qb = q.astype(BF16)
        num_i = jnp.zeros((rs, A_DV), F32)
        qn = jnp.zeros((rs, 1), F32)
        for bb in range(nbs):
            mine = seq_of_row == bb
            num_i = jnp.where(mine, _dot(qb, ci_ref[bb, h].astype(BF16)), num_i)
            qn = jnp.where(mine, jnp.sum(q * ni_ref[bb, h], axis=1, keepdims=True), qn)
        num = w_inter * num_i + num
        den = w_inter * qn + den
        hh = num / jnp.maximum(jnp.abs(den), jnp.exp(-m_t))
        hg_ref[:, h * A_DV:(h + 1) * A_DV] = _sigmoid(o_ref[:, h * A_DV:(h + 1) * A_DV]) * hh

        m_new = at_last(to_row(m_t))
        w_state = jnp.exp(at_last(b_row) - b_col + ig_col - m_new)
        decay = jnp.exp(at_last(to_row(inter)) - m_new)
        kw = ks * w_state
        kw_t = _pad_rows(kw, LANES).T.astype(BF16)
        v_pad = _pad_rows(v, LANES)
        for bb in range(nbs):
            v_mine = jnp.where(seq_of_prow == bb, v_pad, 0.0).astype(BF16)
            dec = decay[bb:bb + 1, :]
            co_ref[bb, h] = dec * ci_ref[bb, h] + _dot(kw_t, v_mine)
            no_ref[bb, h] = dec * ni_ref[bb, h] + jnp.sum(jnp.where(seq_of_row == bb, kw, 0.0), axis=0, keepdims=True)
        m_out = jnp.where(col == h, m_new, m_out)
    mo_ref[...] = m_out


def _mlstm_sample(p, gates, bias_row, m_rows, c_in, n_in, *, seq_len):
    T = p.shape[0]
    nbs = SEQ_GROUP
    rs = nbs * seq_len
    nb = c_in.shape[0]
    hk = A_HEADS * A_DK
    hv = A_HEADS * A_DV
    return pl.pallas_call(
        functools.partial(_mlstm_sample_kernel, seq_len=seq_len),
        grid=(nb // nbs,),
        in_specs=[pl.BlockSpec((rs, hk), lambda i: (i, 0)),
                  pl.BlockSpec((rs, hk), lambda i: (i, 1)),
                  pl.BlockSpec((rs, hv), lambda i: (i, 1)),
                  pl.BlockSpec((rs, hv), lambda i: (i, 2)),
                  pl.BlockSpec((rs, LANES), lambda i: (i, 0)),
                  pl.BlockSpec((1, LANES), lambda i: (0, 0)),
                  pl.BlockSpec((rs, LANES), lambda i: (i, 0)),
                  pl.BlockSpec((nbs, A_HEADS, A_DK, A_DV), lambda i: (i, 0, 0, 0)),
                  pl.BlockSpec((nbs, A_HEADS, 1, A_DK), lambda i: (i, 0, 0, 0))],
        out_specs=[pl.BlockSpec((rs, hv), lambda i: (i, 0)),
                   pl.BlockSpec((nbs, A_HEADS, A_DK, A_DV), lambda i: (i, 0, 0, 0)),
                   pl.BlockSpec((nbs, A_HEADS, 1, A_DK), lambda i: (i, 0, 0, 0)),
                   pl.BlockSpec((rs, LANES), lambda i: (i, 0))],
        out_shape=[SDS((T, hv), F32),
                   SDS((nb, A_HEADS, A_DK, A_DV), F32),
                   SDS((nb, A_HEADS, 1, A_DK), F32),
                   SDS((T, LANES), F32)],
        compiler_params=_params(("arbitrary",)),
        name="mlstm_sample",
    )(p, p, p, p, gates, bias_row, m_rows, c_in, n_in)


def _pair_select(a, h0):
    lane = lax.broadcasted_iota(jnp.int32, (a.shape[0], LANES), 1)
    return jnp.where(lane < B_HEADDIM, a[:, h0:h0 + 1], a[:, h0 + 1:h0 + 2])


def _group_heads_first(a, g):
    return pltpu.roll(a, (LANES - HEADS_PER_GROUP * g) % LANES, 1)


def _group_norm_gate(y, z, nw):
    gz = y * _silu(z)
    return gz * lax.rsqrt(jnp.mean(gz * gz, axis=-1, keepdims=True) + EPS) * nw


def _ssd_prompt_kernel(x_ref, b_ref, c_ref, z_ref, dtc_ref,
                       cwx_ref, cwb_ref, cwc_ref, cbx_ref, cbb_ref, cbc_ref,
                       dbr_ref, alr_ref, d_ref, nw_ref,
                       y_ref, so_ref, s_s, xp_s, bp_s, cp_s, *, c, groups_per_step):
    ci = pl.program_id(2)

    @pl.when(ci == 0)
    def _():
        s_s[...] = jnp.zeros_like(s_s)
        xp_s[...] = jnp.zeros_like(xp_s)
        bp_s[...] = jnp.zeros_like(bp_s)
        cp_s[...] = jnp.zeros_like(cp_s)

    for gi in range(groups_per_step):
        _ssd_prompt_group(x_ref, b_ref, c_ref, z_ref, dtc_ref, cwx_ref, cwb_ref, cwc_ref, cbx_ref, cbb_ref, cbc_ref,
                          dbr_ref, alr_ref, d_ref, nw_ref, y_ref, s_s, xp_s, bp_s, cp_s,
                          c=c, gi=gi, group=pl.program_id(1) * groups_per_step + gi)

    @pl.when(ci == pl.num_programs(2) - 1)
    def _():
        so_ref[0] = s_s[...]


def _ssd_prompt_group(x_ref, b_ref, c_ref, z_ref, dtc_ref, cwx_ref, cwb_ref, cwc_ref, cbx_ref, cbb_ref, cbc_ref,
                      dbr_ref, alr_ref, d_ref, nw_ref, y_ref, s_s, xp_s, bp_s, cp_s, *, c, gi, group):
    ch = slice(gi * GROUP_CH, (gi + 1) * GROUP_CH)
    ns = slice(gi * B_DSTATE, (gi + 1) * B_DSTATE)
    ls = slice(gi * LANES, (gi + 1) * LANES)

    def conv(raw_ref, prev_s, cw_ref, cb_ref, sl):
        raw = raw_ref[:, sl]
        prev = prev_s[:, sl]
        out = _conv_taps(raw, lambda k: _shift_with_prev(raw, prev, k), cw_ref.at[:, sl], cb_ref.at[:, sl], B_DCONV)
        prev_s[:, sl] = raw[c - SUBLANES:, :]
        return _silu(out)

    x = conv(x_ref, xp_s, cwx_ref, cbx_ref, ch)
    bm = conv(b_ref, bp_s, cwb_ref, cbb_ref, ns)
    cm = conv(c_ref, cp_s, cwc_ref, cbc_ref, ns)

    dt_c = _softplus(_group_heads_first(dtc_ref[...], group) + dbr_ref[:, ls])
    ac_c = _cumsum_rows(dt_c * (-jnp.exp(alr_ref[:, ls])))
    ac_c2 = ac_c * LOG2E
    src_r2 = (ac_c2 - jnp.log(dt_c) * LOG2E).T[:HEADS_PER_GROUP, :]

    row = lax.broadcasted_iota(jnp.int32, (c, c), 0)
    col = lax.broadcasted_iota(jnp.int32, (c, c), 1)
    bmb = bm.astype(BF16)
    cmb = cm.astype(BF16)
    cb = jnp.where(col <= row, _dot_nt(cmb, bmb), 0.0)
    state = s_s[gi]
    y_inter = _dot(cmb, state.astype(BF16))
    ac_last = ac_c[c - 1:c, :]
    to_end = jnp.exp(ac_last - ac_c) * dt_c
    e_ac = jnp.exp(ac_c)
    e_last = jnp.exp(ac_last)
    lane = lax.broadcasted_iota(jnp.int32, (c, LANES), 1)
    d_all = d_ref[:, ch]

    ys = []
    for pr in range(HEADS_PER_GROUP // 2):
        h0 = 2 * pr
        sl = slice(pr * LANES, (pr + 1) * LANES)
        xp = x[:, sl]
        xpb = xp.astype(BF16)
        full = []
        for r in (h0, h0 + 1):
            blocks = []
            for rb in range(c // LANES):
                rsl = slice(rb * LANES, (rb + 1) * LANES)
                ncol = (rb + 1) * LANES
                fac = jnp.exp2(jnp.minimum(ac_c2[rsl, r:r + 1] - src_r2[r:r + 1, :ncol], MAX_EXP2))
                blocks.append(_dot((cb[rsl, :ncol] * fac).astype(BF16), xpb[:ncol]))
            full.append(jnp.concatenate(blocks, axis=0))
        yp = jnp.where(lane < B_HEADDIM, full[0], full[1])
        yp = yp + y_inter[:, sl] * _pair_select(e_ac, h0) + d_all[:, sl] * xp
        ys.append(yp)
        xw = (xp * _pair_select(to_end, h0)).astype(BF16)
        s_s[gi, :, sl] = _pair_select(e_last, h0) * state[:, sl] + _dot_tn(bmb, xw)
    y = jnp.concatenate(ys, axis=1)
    y_ref[:, ch] = _group_norm_gate(y, z_ref[:, ch], nw_ref[:, ch]).astype(BF16)


def _ssd_prompt(p, dt_cols, w, *, nseq, seq_len, c, groups_per_step):
    T = p.shape[0]
    nc = seq_len // c
    gps = groups_per_step
    steps = B_GROUPS // gps
    xw_, nw_, lw_ = gps * GROUP_CH, gps * B_DSTATE, gps * LANES
    rows = lambda b, g, ci: b * nc + ci
    kx = B_DINNER // xw_
    kb = (2 * B_DINNER) // nw_
    kc = kb + steps
    wb = B_DINNER // nw_
    wc = wb + steps
    per_lane = lambda width, start=0: pl.BlockSpec((1, width), lambda b, g, ci: (0, start + g))
    return pl.pallas_call(
        functools.partial(_ssd_prompt_kernel, c=c, groups_per_step=gps),
        grid=(nseq, steps, nc),
        in_specs=[pl.BlockSpec((c, xw_), lambda b, g, ci: (rows(b, g, ci), kx + g)),
                  pl.BlockSpec((c, nw_), lambda b, g, ci: (rows(b, g, ci), kb + g)),
                  pl.BlockSpec((c, nw_), lambda b, g, ci: (rows(b, g, ci), kc + g)),
                  pl.BlockSpec((c, xw_), lambda b, g, ci: (rows(b, g, ci), g)),
                  pl.BlockSpec((c, LANES), lambda b, g, ci: (rows(b, g, ci), 0)),
                  pl.BlockSpec((None, B_DCONV, xw_), lambda b, g, ci: (0, 0, g)),
                  pl.BlockSpec((None, B_DCONV, nw_), lambda b, g, ci: (0, 0, wb + g)),
                  pl.BlockSpec((None, B_DCONV, nw_), lambda b, g, ci: (0, 0, wc + g)),
                  per_lane(xw_), per_lane(nw_, wb), per_lane(nw_, wc),
                  per_lane(lw_), per_lane(lw_),
                  per_lane(xw_), per_lane(xw_)],
        out_specs=[pl.BlockSpec((c, xw_), lambda b, g, ci: (rows(b, g, ci), g)),
                   pl.BlockSpec((1, gps, B_DSTATE, GROUP_CH), lambda b, g, ci: (b, g, 0, 0))],
        out_shape=[SDS((T, B_DINNER), BF16), SDS((nseq, B_GROUPS, B_DSTATE, GROUP_CH), F32)],
        scratch_shapes=[pltpu.VMEM((gps, B_DSTATE, GROUP_CH), F32),
                        pltpu.VMEM((SUBLANES, xw_), F32),
                        pltpu.VMEM((SUBLANES, nw_), F32),
                        pltpu.VMEM((SUBLANES, nw_), F32)],
        compiler_params=_params(("arbitrary", "arbitrary", "arbitrary")),
        name="ssd_prompt",
    )(p, p, p, p, dt_cols, w["b_conv_w"], w["b_conv_w"], w["b_conv_w"],
      w["b_conv_b"], w["b_conv_b"], w["b_conv_b"],
      w["dtb_cols"], w["alog_cols"], w["d_lane"], w["b_norm_w"])


def _ssd_sample_kernel(x_ref, b_ref, c_ref, z_ref, dt_ref, sx_ref, sb_ref, sc_ref,
                       cwx_ref, cwb_ref, cwc_ref, cbx_ref, cbb_ref, cbc_ref,
                       dtb_ref, al_ref, d_ref, nw_ref, hs_ref,
                       y_ref, hso_ref, *, seq_len, groups_per_step):
    for gi in range(groups_per_step):
        _ssd_sample_group(x_ref, b_ref, c_ref, z_ref, dt_ref, sx_ref, sb_ref, sc_ref,
                          cwx_ref, cwb_ref, cwc_ref, cbx_ref, cbb_ref, cbc_ref,
                          dtb_ref, al_ref, d_ref, nw_ref, hs_ref, y_ref, hso_ref,
                          seq_len=seq_len, gi=gi, group=pl.program_id(1) * groups_per_step + gi)


def _ssd_sample_group(x_ref, b_ref, c_ref, z_ref, dt_ref, sx_ref, sb_ref, sc_ref,
                      cwx_ref, cwb_ref, cwc_ref, cbx_ref, cbb_ref, cbc_ref,
                      dtb_ref, al_ref, d_ref, nw_ref, hs_ref, y_ref, hso_ref, *, seq_len, gi, group):
    nbs = SEQ_GROUP
    rs = nbs * seq_len
    sr = _SampleRows(rs, seq_len)
    prow = lax.broadcasted_iota(jnp.int32, (LANES, 1), 0)
    seq_of_prow = jnp.where(prow < rs, prow % nbs, -1)
    ch = slice(gi * GROUP_CH, (gi + 1) * GROUP_CH)
    ns = slice(gi * B_DSTATE, (gi + 1) * B_DSTATE)
    ls = slice(gi * LANES, (gi + 1) * LANES)
    h_first = gi * HEADS_PER_GROUP

    def conv(raw_ref, st_ref, cw_ref, cb_ref, sl):
        raw = raw_ref[:, sl]
        return _silu(_conv_taps(raw, sr.conv_shift(raw, st_ref[:, sl], B_DCONV), cw_ref.at[:, sl], cb_ref.at[:, sl],
                                B_DCONV))

    x = conv(x_ref, sx_ref, cwx_ref, cbx_ref, ch)
    bm = conv(b_ref, sb_ref, cwb_ref, cbb_ref, ns)
    cm = conv(c_ref, sc_ref, cwc_ref, cbc_ref, ns)

    dt = _softplus(_group_heads_first(dt_ref[...], group) + dtb_ref[:, ls])
    a = dt * (-jnp.exp(al_ref[:, ls]))
    acum = a
    total = a
    for k in range(1, seq_len):
        acum = acum + sr.back(a, k)
        total = total + sr.back(a, k) + sr.ahead(a, k)

    coefs = []
    for d in range(seq_len):
        cb_d = jnp.sum(cm * sr.back(bm, d), axis=1, keepdims=True)
        coefs.append(cb_d * jnp.exp(acum - sr.back(acum, d)) * sr.back(dt, d))
    x_back = [sr.back(x, d) for d in range(seq_len)]

    e_ac = jnp.exp(acum)
    e_tot = jnp.exp(total)
    to_end = jnp.exp(total - acum) * dt
    d_all = d_ref[:, ch]

    ys = []
    xws = []
    for pr in range(HEADS_PER_GROUP // 2):
        h0 = 2 * pr
        sl = slice(pr * LANES, (pr + 1) * LANES)
        yp = d_all[:, sl] * x[:, sl]
        for d in range(seq_len):
            yp = yp + _pair_select(coefs[d], h0) * x_back[d][:, sl]
        ys.append(yp)
        xws.append(x[:, sl] * _pair_select(to_end, h0))
    xw_t = _pad_rows(jnp.concatenate(xws, axis=1), LANES).T.astype(BF16)
    cmb = cm.astype(BF16)
    bm_pad = _pad_rows(bm, LANES)
    y_int = jnp.zeros((rs, GROUP_CH), F32)
    for bb in range(nbs):
        hs_g = hs_ref[bb, h_first:h_first + HEADS_PER_GROUP].reshape(GROUP_CH, B_DSTATE)
        y_int = jnp.where(sr.seq == bb, _dot_nt(cmb, hs_g.astype(BF16)), y_int)
        b_mine = jnp.where(seq_of_prow == bb, bm_pad, 0.0).astype(BF16)
        upd = _dot(xw_t, b_mine)
        for r in range(HEADS_PER_GROUP):
            dec = e_tot[bb:bb + 1, r:r + 1]
            hso_ref[bb, h_first + r] = dec * hs_ref[bb, h_first + r] + upd[r * B_HEADDIM:(r + 1) * B_HEADDIM, :]
    e_pairs = jnp.concatenate([_pair_select(e_ac, 2 * pr) for pr in range(HEADS_PER_GROUP // 2)], axis=1)
    y = jnp.concatenate(ys, axis=1) + y_int * e_pairs
    y_ref[:, ch] = _group_norm_gate(y, z_ref[:, ch], nw_ref[:, ch])


def _ssd_sample(p, dt_cols, conv_state, w, hs, *, seq_len, groups_per_step):
    T = p.shape[0]
    nbs = SEQ_GROUP
    rs = nbs * seq_len
    nb = hs.shape[0]
    gps = groups_per_step
    steps = B_GROUPS // gps
    xw_, nw_, lw_ = gps * GROUP_CH, gps * B_DSTATE, gps * LANES
    kx = B_DINNER // xw_
    kb = (2 * B_DINNER) // nw_
    kc = kb + steps
    wb = B_DINNER // nw_
    wc = wb + steps
    per_lane = lambda width, start=0: pl.BlockSpec((1, width), lambda i, g: (0, start + g))
    hs_spec = pl.BlockSpec((nbs, gps * HEADS_PER_GROUP, B_HEADDIM, B_DSTATE), lambda i, g: (i, g, 0, 0))
    return pl.pallas_call(
        functools.partial(_ssd_sample_kernel, seq_len=seq_len, groups_per_step=gps),
        grid=(nb // nbs, steps),
        in_specs=[pl.BlockSpec((rs, xw_), lambda i, g: (i, kx + g)),
                  pl.BlockSpec((rs, nw_), lambda i, g: (i, kb + g)),
                  pl.BlockSpec((rs, nw_), lambda i, g: (i, kc + g)),
                  pl.BlockSpec((rs, xw_), lambda i, g: (i, g)),
                  pl.BlockSpec((rs, LANES), lambda i, g: (i, 0)),
                  pl.BlockSpec((rs, xw_), lambda i, g: (i, g)),
                  pl.BlockSpec((rs, nw_), lambda i, g: (i, wb + g)),
                  pl.BlockSpec((rs, nw_), lambda i, g: (i, wc + g)),
                  pl.BlockSpec((None, B_DCONV, xw_), lambda i, g: (0, 0, g)),
                  pl.BlockSpec((None, B_DCONV, nw_), lambda i, g: (0, 0, wb + g)),
                  pl.BlockSpec((None, B_DCONV, nw_), lambda i, g: (0, 0, wc + g)),
                  per_lane(xw_), per_lane(nw_, wb), per_lane(nw_, wc),
                  per_lane(lw_), per_lane(lw_), per_lane(xw_), per_lane(xw_),
                  hs_spec],
        out_specs=[pl.BlockSpec((rs, xw_), lambda i, g: (i, g)), hs_spec],
        out_shape=[SDS((T, B_DINNER), F32), SDS(hs.shape, F32)],
        compiler_params=_params(("arbitrary", "arbitrary")),
        name="ssd_sample",
    )(p, p, p, p, dt_cols, conv_state, conv_state, conv_state,
      w["b_conv_w"], w["b_conv_w"], w["b_conv_w"], w["b_conv_b"], w["b_conv_b"], w["b_conv_b"],
      w["dtb_cols"], w["alog_cols"], w["d_lane"], w["b_norm_w"], hs)


def _pad_lanes(a, width=LANES):
    return jnp.pad(a, ((0, 0), (0, width - a.shape[1])))


def _per_group_lanes(a):
    r = a.shape[0]
    a = a.reshape(r, B_GROUPS, HEADS_PER_GROUP)
    return jnp.pad(a, ((0, 0), (0, 0), (0, LANES - HEADS_PER_GROUP))).reshape(r, B_GROUPS * LANES)


def _to_sample_rows(a, seq_len):
    nseq, nt, ch = a.shape
    a = a.reshape(nseq // SEQ_GROUP, SEQ_GROUP, nt, ch).transpose(0, 2, 1, 3)
    a = jnp.pad(a, ((0, 0), (0, seq_len - nt), (0, 0), (0, 0)))
    return a.reshape(nseq * seq_len, ch)


def _from_sample_rows(a, nseq, seq_len, first_token=0):
    ch = a.shape[1]
    a = a.reshape(nseq // SEQ_GROUP, seq_len, SEQ_GROUP, ch)[:, first_token:]
    return a.transpose(0, 2, 1, 3).reshape(nseq, seq_len - first_token, ch)


def _forward(x, nseq, seq_len, state, w, *, prompt):
    T = x.shape[0]
    tm = 1024 if prompt else T
    a_main = 2 * A_HEADS * A_DK + 2 * A_HEADS * A_DV
    b_main = B_DINNER + B_CONV_DIM
    nstate = FFN_CONV - 1
    cast = {"ffn_bf16": [None, None]}

    outs = _norm_matmul(x, w["norm_mix"], 0, w["a_w_in"], a_main, w["a_w_gate"], tm=tm, tn=1024)
    p_a, gates = outs[:2]
    cast["a_w_in"] = outs[-1]
    if prompt:
        hg, m_c, m_n, m_m = _mlstm_prompt(p_a, gates, w["a_b_row"], nseq=nseq, seq_len=seq_len, c=256)
        m_m = m_m[:, :, 0, 0]
    else:
        c0, n0, m0 = state["mlstm"]
        m_rows = _pad_lanes(_to_sample_rows(jnp.broadcast_to(m0[:, None, :], (nseq, seq_len, A_HEADS)), seq_len))
        hg, m_c, m_n, m_rows_new = _mlstm_sample(p_a, gates, w["a_b_row"], m_rows, c0, n0[:, :, None, :],
                                                 seq_len=seq_len)
        m_m = _from_sample_rows(m_rows_new, nseq, seq_len, seq_len - 1)[:, 0, :A_HEADS]
    m_n = m_n[:, :, 0, :]
    outs = _matmul_res(hg, w["a_w_out"], x, tm=tm, tn=512)
    x, cast["a_w_out"] = outs[0], outs[-1]

    def ffn(x, layer, final_w=None):
        if prompt:
            ftm = 1024
            x_new, tg, tv = _ffn_prompt(x, layer, w, tm=ftm, tf=512, seq_len=seq_len, final_w=final_w)
            tps = seq_len // ftm
            pick = lambda t: t.reshape(nseq, tps, SUBLANES, D_FF)[:, tps - 1, SUBLANES - nstate:]
            tails = jnp.concatenate([pick(tg), pick(tv)], axis=-1)
        else:
            outs = _ffn_sample(x, layer, w, state["ffn_conv"], tf=256, seq_len=seq_len, final_w=final_w)
            x_new = outs[0]
            tails = jnp.concatenate(outs[1:3], axis=-1)
            cast["ffn_bf16"][layer] = tuple(outs[3:])
        return x_new, tails

    x, f_conv0 = ffn(x, 0)

    outs = _norm_matmul(x, w["norm_mix"], 1, w["b_w_in"], b_main, w["b_w_dt"], tm=tm, tn=1024)
    p_b, dt_raw = outs[:2]
    cast["b_w_in"] = outs[-1]
    if prompt:
        s_conv = p_b.reshape(nseq, seq_len, b_main)[:, seq_len - (B_DCONV - 1):, B_DINNER:]
        yn, s_t = _ssd_prompt(p_b, dt_raw, w, nseq=nseq, seq_len=seq_len, c=256, groups_per_step=2)
        ssm = s_t.reshape(nseq, B_GROUPS, B_DSTATE, HEADS_PER_GROUP, B_HEADDIM)
        ssm = ssm.transpose(0, 1, 3, 4, 2).reshape(nseq, B_HEADS, B_HEADDIM, B_DSTATE)
    else:
        s_conv = _from_sample_rows(p_b, nseq, seq_len, seq_len - (B_DCONV - 1))[:, :, B_DINNER:]
        st = _to_sample_rows(state["ssm_conv"], seq_len)
        yn, ssm = _ssd_sample(p_b, dt_raw, st, w, state["ssm"], seq_len=seq_len, groups_per_step=4)
    outs = _matmul_res(yn, w["b_w_out"], x, tm=tm, tn=512)
    x, cast["b_w_out"] = outs[0], outs[-1]

    y, f_conv1 = ffn(x, 1, final_w=w["norm_final"])
    return (y, m_c[None], m_n[None], m_m[None], ssm[None], s_conv[None], jnp.stack([f_conv0, f_conv1]), cast)


def kernel(x_prompt, x_sample, state_mlstm_C, state_mlstm_n, state_mlstm_m, state_ssm, state_ssm_conv,
           state_ffn_conv, norm_mix, norm_ffn, norm_final, a_w_in, a_b_if, a_w_out, b_w_in, b_conv_w,
           b_conv_b, b_dt_bias, b_A_log, b_D, b_norm_w, b_w_out, f_w_up, f_conv_w, f_conv_b, f_w_down):
    a_main = 2 * A_HEADS * A_DK + 2 * A_HEADS * A_DV
    b_main = B_DINNER + B_CONV_DIM
    w = {
        "norm_mix": norm_mix[:, None], "norm_ffn": norm_ffn[:, None], "norm_final": norm_final[None],
        "a_w_in": jnp.swapaxes(a_w_in, 1, 2),
        "a_w_gate": _pad_lanes(a_w_in[0, :, a_main:]).astype(BF16),
        "a_b_row": _pad_lanes(a_b_if),
        "a_w_out": a_w_out,
        "b_w_in": jnp.swapaxes(b_w_in, 1, 2),
        "b_w_dt": _pad_lanes(b_w_in[0, :, b_main:]).astype(BF16),
        "b_conv_w": b_conv_w, "b_conv_b": b_conv_b,
        "dtb_cols": _per_group_lanes(b_dt_bias), "alog_cols": _per_group_lanes(b_A_log),
        "d_lane": jnp.repeat(b_D, B_HEADDIM, axis=1), "b_norm_w": b_norm_w,
        "b_w_out": b_w_out,
        "f_w_up": f_w_up, "f_conv_w": f_conv_w, "f_conv_b": f_conv_b[:, None],
        "f_w_down": f_w_down,
    }
    bp, lp, _ = x_prompt.shape
    bs, ls, _ = x_sample.shape
    state = {"mlstm": (state_mlstm_C[0], state_mlstm_n[0], state_mlstm_m[0]),
             "ssm": state_ssm[0], "ssm_conv": state_ssm_conv[0], "ffn_conv": state_ffn_conv}
    ys, sc, sn, sm, sssm, ssconv, sfconv, w_bf16 = _forward(_to_sample_rows(x_sample, ls), bs, ls, state, w,
                                                            prompt=False)
    yp, pc, pn, pm, pssm, psconv, pfconv, _ = _forward(x_prompt.reshape(bp * lp, D_MODEL), bp, lp, None,
                                                       {**w, **w_bf16}, prompt=True)
    return (yp.reshape(bp, lp, D_MODEL), _from_sample_rows(ys, bs, ls), pc, sc, pn, sn, pm, sm,
            pssm, sssm, psconv, ssconv, pfconv, sfconv)
```

```python
import functools

import jax
import jax.numpy as jnp
from jax import lax
from jax.experimental import pallas as pl
from jax.experimental.pallas import tpu as pltpu

F32 = jnp.float32
BF16 = jnp.bfloat16
SDS = jax.ShapeDtypeStruct

D_MODEL = 2048
A_HEADS = 8
A_DK = 128
A_DV = 256
B_DINNER = 4096
B_HEADDIM = 64
B_HEADS = 64
B_GROUPS = 8
B_DSTATE = 128
B_DCONV = 4
B_CONV_DIM = 6144
D_FF = 5632
FFN_CONV = 3
EPS = 1e-6
NEG_INF = float("-inf")
LOG2E = 1.4426950408889634
MAX_EXP2 = 126.0

LANES = 128
SUBLANES = 8
VMEM_LIMIT_BYTES = 56 * 1024 * 1024

PROJ_ROWS = 1024
PROJ_COLS = 1024
OUT_COLS = 512
FFN_ROWS = 1024
FFN_COLS = 512
FFN_SAMPLE_COLS = 256
SCAN_CHUNK = 256
SSD_PROMPT_GROUPS = 2
SSD_SAMPLE_GROUPS = 4

HEADS_PER_GROUP = B_HEADS // B_GROUPS
GROUP_CH = HEADS_PER_GROUP * B_HEADDIM
SEQ_GROUP = SUBLANES


def _params(sem, flags=None):
    return pltpu.CompilerParams(dimension_semantics=sem, vmem_limit_bytes=VMEM_LIMIT_BYTES, flags=flags)


def _sigmoid(x):
    return 1.0 / (1.0 + jnp.exp(-x))


def _silu(x):
    h = 0.5 * x
    return h + h * jnp.tanh(h)


def _softplus(x):
    return jnp.maximum(x, 0.0) + jnp.log1p(jnp.exp(-jnp.abs(x)))


def _log_sigmoid(x):
    return -_softplus(-x)


def _rms(x, w):
    ms = jnp.mean(x * x, axis=-1, keepdims=True)
    return x * lax.rsqrt(ms + EPS) * w


def _cumsum_rows(x):
    n = x.shape[0]
    row = lax.broadcasted_iota(jnp.int32, x.shape, 0)
    k = 1
    while k < n:
        x = x + jnp.where(row >= k, pltpu.roll(x, k, 0), 0.0)
        k *= 2
    return x


def _dot(a, b):
    return jnp.dot(a, b, preferred_element_type=F32)


def _dot_nt(a, b):
    return lax.dot_general(a, b, (((1,), (1,)), ((), ())), preferred_element_type=F32)


def _dot_tn(a, b):
    return lax.dot_general(a, b, (((0,), (0,)), ((), ())), preferred_element_type=F32)


def _conv_taps(u, shifted, w_ref, b_ref, ntaps):
    y = u * w_ref[ntaps - 1:ntaps, :]
    for k in range(1, ntaps):
        y = y + shifted(k) * w_ref[ntaps - 1 - k:ntaps - k, :]
    return y + b_ref[...]


def _shift_with_prev(u, prev8, k):
    ext = jnp.concatenate([prev8, u], axis=0)
    return pltpu.roll(ext, k, 0)[SUBLANES:]


def _pad_rows(a, rows):
    if a.shape[0] == rows:
        return a
    return jnp.concatenate([a, jnp.zeros((rows - a.shape[0], a.shape[1]), a.dtype)], axis=0)


class _SampleRows:
    def __init__(self, rows, seq_len):
        self.rows = rows
        self.seq_len = seq_len
        r = lax.broadcasted_iota(jnp.int32, (rows, 1), 0)
        self.token = (r // SEQ_GROUP) % seq_len
        self.seq = r % SEQ_GROUP

    def back(self, a, k):
        if k == 0:
            return a
        return jnp.where(self.token >= k, pltpu.roll(a, SEQ_GROUP * k, 0), 0.0)

    def ahead(self, a, k):
        if k == 0:
            return a
        return jnp.where(self.token + k < self.seq_len, pltpu.roll(a, self.rows - SEQ_GROUP * k, 0), 0.0)

    def conv_shift(self, u, state, ntaps):
        return lambda k: self.back(u, k) + self.ahead(state, ntaps - 1 - k)


def _tokens_to_rows(slots, seq_len):
    nseq, ch = slots[0].shape
    pad = jnp.zeros(((seq_len - len(slots)) * SEQ_GROUP, ch), slots[0].dtype)
    parts = []
    for g in range(nseq // SEQ_GROUP):
        parts += [s[g * SEQ_GROUP:(g + 1) * SEQ_GROUP] for s in slots] + [pad]
    return jnp.concatenate(parts, axis=0)


def _rows_of_token(a, seq_len, t):
    step = seq_len * SEQ_GROUP
    return jnp.concatenate([a[g * step + t * SEQ_GROUP:g * step + (t + 1) * SEQ_GROUP]
                            for g in range(a.shape[0] // step)], axis=0)


def _norm_matmul_kernel(x_ref, nw_ref, w_ref, ws_ref, o_ref, os_ref, *rest):
    hn_ref = rest[-1]

    @pl.when(pl.program_id(1) == 0)
    def _():
        hn = _rms(x_ref[...], nw_ref[...]).astype(BF16)
        hn_ref[...] = hn
        os_ref[...] = _dot(hn, ws_ref[...])

    wb = w_ref[...].astype(BF16)
    if len(rest) == 2:
        rest[0][...] = wb
    o_ref[...] = _dot_nt(hn_ref[...], wb)


def _norm_matmul(x, nw, layer, w_t, n_main, ws, *, tm, tn):
    T, D = x.shape
    emit = w_t.dtype != BF16
    w_spec = pl.BlockSpec((None, tn, D), lambda i, j: (0, j, 0))
    out_specs = [pl.BlockSpec((tm, tn), lambda i, j: (i, j)), pl.BlockSpec((tm, LANES), lambda i, j: (i, 0))]
    out_shape = [SDS((T, n_main), F32), SDS((T, LANES), F32)]
    if emit:
        assert T == tm, "the bfloat16 copy is written once, by a single row tile"
        out_specs.append(w_spec)
        out_shape.append(SDS((1, n_main, D), BF16))
    return pl.pallas_call(
        _norm_matmul_kernel,
        grid=(T // tm, n_main // tn),
        in_specs=[pl.BlockSpec((tm, D), lambda i, j: (i, 0)),
                  pl.BlockSpec((None, 1, D), lambda i, j: (layer, 0, 0)),
                  w_spec,
                  pl.BlockSpec((D, LANES), lambda i, j: (0, 0))],
        out_specs=out_specs,
        out_shape=out_shape,
        scratch_shapes=[pltpu.VMEM((tm, D), BF16)],
        compiler_params=_params(("arbitrary", "arbitrary")),
        name="norm_matmul",
    )(x, nw, w_t, ws)


def _matmul_res_kernel(a_ref, w_ref, r_ref, o_ref, *rest):
    wb = w_ref[...].astype(BF16)
    if rest:
        rest[0][...] = wb
    o_ref[...] = r_ref[...] + _dot(a_ref[...].astype(BF16), wb)


def _matmul_res(a, w, res, *, tm, tn):
    T, K = a.shape
    N = w.shape[2]
    emit = w.dtype != BF16
    w_spec = pl.BlockSpec((None, K, tn), lambda i, j: (0, 0, j))
    out_specs = [pl.BlockSpec((tm, tn), lambda i, j: (i, j))]
    out_shape = [SDS((T, N), F32)]
    if emit:
        assert T == tm, "the bfloat16 copy is written once, by a single row tile"
        out_specs.append(w_spec)
        out_shape.append(SDS((1, K, N), BF16))
    return pl.pallas_call(
        _matmul_res_kernel,
        grid=(T // tm, N // tn),
        in_specs=[pl.BlockSpec((tm, K), lambda i, j: (i, 0)),
                  w_spec,
                  pl.BlockSpec((tm, tn), lambda i, j: (i, j))],
        out_specs=out_specs,
        out_shape=out_shape,
        compiler_params=_params(("arbitrary", "arbitrary")),
        name="matmul_res",
    )(a, w, res)


def _ffn_sample_kernel(*refs, tm, seq_len, final_norm):
    refs = list(refs)
    x_ref, nw_ref, wug_ref, wuv_ref, cwg_ref, cwv_ref, cbg_ref, cbv_ref, wd_ref = refs[:9]
    del refs[:9]
    nstate = FFN_CONV - 1
    sg_ref, sv_ref = refs[:2]
    del refs[:2]
    fw_ref = refs.pop(0) if final_norm else None
    out_ref = refs.pop(0)
    tg_ref, tv_ref, wbg_ref, wbv_ref, wbd_ref, hn_s, acc_s = refs
    j = pl.program_id(1)

    @pl.when(j == 0)
    def _():
        hn_s[...] = _rms(x_ref[...], nw_ref[...]).astype(BF16)
        acc_s[...] = jnp.zeros_like(acc_s)

    wug = wug_ref[...].astype(BF16)
    wuv = wuv_ref[...].astype(BF16)
    wd = wd_ref[...].astype(BF16)
    hn = hn_s[...]
    ug = _dot(hn, wug)
    uv = _dot(hn, wuv)

    wbg_ref[...] = wug
    wbv_ref[...] = wuv
    wbd_ref[...] = wd
    for t in range(nstate):
        tg_ref[:, t, :] = _rows_of_token(ug, seq_len, seq_len - nstate + t)
        tv_ref[:, t, :] = _rows_of_token(uv, seq_len, seq_len - nstate + t)
    sr = _SampleRows(tm, seq_len)
    sg = _tokens_to_rows([sg_ref[:, t, :] for t in range(nstate)], seq_len)
    sv = _tokens_to_rows([sv_ref[:, t, :] for t in range(nstate)], seq_len)
    yg = _conv_taps(ug, sr.conv_shift(ug, sg, FFN_CONV), cwg_ref, cbg_ref, FFN_CONV)
    yv = _conv_taps(uv, sr.conv_shift(uv, sv, FFN_CONV), cwv_ref, cbv_ref, FFN_CONV)

    act = (_silu(yg) * yv).astype(BF16)
    acc_s[...] += _dot(act, wd)

    @pl.when(j == pl.num_programs(1) - 1)
    def _():
        y = x_ref[...] + acc_s[...]
        out_ref[...] = _rms(y, fw_ref[...]) if final_norm else y


def _ffn_sample(x, layer, w, state, *, tf, seq_len, final_w=None):
    T, D = x.shape
    nj = D_FF // tf
    nstate = FFN_CONV - 1
    nseq = T // seq_len
    in_specs = [pl.BlockSpec((T, D), lambda i, j: (0, 0)),
                pl.BlockSpec((None, 1, D), lambda i, j: (layer, 0, 0)),
                pl.BlockSpec((None, D, tf), lambda i, j: (layer, 0, j)),
                pl.BlockSpec((None, D, tf), lambda i, j: (layer, 0, nj + j)),
                pl.BlockSpec((None, FFN_CONV, tf), lambda i, j: (layer, 0, j)),
                pl.BlockSpec((None, FFN_CONV, tf), lambda i, j: (layer, 0, nj + j)),
                pl.BlockSpec((None, 1, tf), lambda i, j: (layer, 0, j)),
                pl.BlockSpec((None, 1, tf), lambda i, j: (layer, 0, nj + j)),
                pl.BlockSpec((None, tf, D), lambda i, j: (layer, j, 0))]
    args = [x, w["norm_ffn"], w["f_w_up"], w["f_w_up"], w["f_conv_w"], w["f_conv_w"], w["f_conv_b"], w["f_conv_b"],
            w["f_w_down"]]
    in_specs += [pl.BlockSpec((None, nseq, nstate, tf), lambda i, j: (layer, 0, 0, j)),
                 pl.BlockSpec((None, nseq, nstate, tf), lambda i, j: (layer, 0, 0, nj + j))]
    args += [state, state]
    if final_w is not None:
        in_specs.append(pl.BlockSpec((1, D), lambda i, j: (0, 0)))
        args.append(final_w)
    out_specs = [pl.BlockSpec((T, D), lambda i, j: (0, 0))]
    out_specs += [pl.BlockSpec((nseq, nstate, tf), lambda i, j: (0, 0, j))] * 2
    out_specs += [pl.BlockSpec((D, tf), lambda i, j: (0, j))] * 2 + [pl.BlockSpec((tf, D), lambda i, j: (j, 0))]
    out_shape = [SDS((T, D), F32)] + [SDS((nseq, nstate, D_FF), F32)] * 2
    out_shape += [SDS((D, D_FF), BF16)] * 2 + [SDS((D_FF, D), BF16)]
    return pl.pallas_call(
        functools.partial(_ffn_sample_kernel, tm=T, seq_len=seq_len, final_norm=final_w is not None),
        grid=(1, nj),
        in_specs=in_specs,
        out_specs=out_specs,
        out_shape=out_shape,
        scratch_shapes=[pltpu.VMEM((T, D), BF16), pltpu.VMEM((T, D), F32)],
        compiler_params=_params(("arbitrary", "arbitrary")),
        name="ffn_sample",
    )(*args)


def _ffn_prompt_kernel(x_ref, nw_ref, wug_ref, wuv_ref, cwg_ref, cwv_ref, cbg_ref, cbv_ref, wd_ref, *rest,
                       tm, tiles_per_seq, nj, final_norm):
    rest = list(rest)
    fw_ref = rest.pop(0) if final_norm else None
    out_ref, tg_ref, tv_ref, hn_s, cg_s, cv_s = rest[:6]
    act_slots = rest[6:8]
    i = pl.program_id(0)
    j = pl.program_id(1)

    def up_act(slot):
        hn = hn_s[...]
        ug = _dot(hn, wug_ref[...])
        uv = _dot(hn, wuv_ref[...])
        tail_g = ug[tm - SUBLANES:, :]
        tail_v = uv[tm - SUBLANES:, :]
        tg_ref[0] = tail_g
        tv_ref[0] = tail_v
        prev_g = cg_s[j]
        prev_v = cv_s[j]
        yg = _conv_taps(ug, lambda k: _shift_with_prev(ug, prev_g, k), cwg_ref, cbg_ref, FFN_CONV)
        yv = _conv_taps(uv, lambda k: _shift_with_prev(uv, prev_v, k), cwv_ref, cbv_ref, FFN_CONV)
        cg_s[j] = tail_g
        cv_s[j] = tail_v
        act_slots[slot][...] = (_silu(yg) * yv).astype(BF16)

    def down_proj(slot):
        out_ref[...] += _dot(act_slots[slot][...], wd_ref[...])

    @pl.when(j == 0)
    def _():
        x = x_ref[...]
        hn_s[...] = _rms(x, nw_ref[...]).astype(BF16)
        out_ref[...] = x

        @pl.when(i % tiles_per_seq == 0)
        def _():
            cg_s[...] = jnp.zeros_like(cg_s)
            cv_s[...] = jnp.zeros_like(cv_s)

        up_act(0)

    steady = (j >= 1) & (j < nj)
    for parity in (0, 1):
        @pl.when(steady & (j % 2 == parity))
        def _():
            up_act(parity)
            down_proj(1 - parity)

    @pl.when(j == nj)
    def _():
        down_proj((nj - 1) % 2)
        if final_norm:
            out_ref[...] = _rms(out_ref[...], fw_ref[...])


def _ffn_prompt(x, layer, w, *, tm, tf, seq_len, final_w=None):
    T, D = x.shape
    nj = D_FF // tf
    ni = T // tm
    assert nj >= 2 and seq_len % tm == 0
    act_j = lambda j: jnp.minimum(j, nj - 1)
    down_j = lambda j: jnp.maximum(j - 1, 0)
    wug, wuv, wd = w["ffn_bf16"][layer]
    in_specs = [pl.BlockSpec((tm, D), lambda i, j: (i, 0), pipeline_mode=pl.Buffered(1)),
                pl.BlockSpec((None, 1, D), lambda i, j: (layer, 0, 0)),
                pl.BlockSpec((D, tf), lambda i, j: (0, act_j(j))),
                pl.BlockSpec((D, tf), lambda i, j: (0, act_j(j))),
                pl.BlockSpec((None, FFN_CONV, tf), lambda i, j: (layer, 0, act_j(j))),
                pl.BlockSpec((None, FFN_CONV, tf), lambda i, j: (layer, 0, nj + act_j(j))),
                pl.BlockSpec((None, 1, tf), lambda i, j: (layer, 0, act_j(j))),
                pl.BlockSpec((None, 1, tf), lambda i, j: (layer, 0, nj + act_j(j))),
                pl.BlockSpec((tf, D), lambda i, j: (down_j(j), 0))]
    args = [x, w["norm_ffn"], wug, wuv, w["f_conv_w"], w["f_conv_w"], w["f_conv_b"], w["f_conv_b"], wd]
    if final_w is not None:
        in_specs.append(pl.BlockSpec((1, D), lambda i, j: (0, 0)))
        args.append(final_w)
    tail_spec = pl.BlockSpec((1, SUBLANES, tf), lambda i, j: (i, 0, act_j(j)))
    return pl.pallas_call(
        functools.partial(_ffn_prompt_kernel, tm=tm, tiles_per_seq=seq_len // tm, nj=nj,
                          final_norm=final_w is not None),
        grid=(ni, nj + 1),
        in_specs=in_specs,
        out_specs=[pl.BlockSpec((tm, D), lambda i, j: (i, 0)), tail_spec, tail_spec],
        out_shape=[SDS((T, D), F32)] + [SDS((ni, SUBLANES, D_FF), F32)] * 2,
        scratch_shapes=[pltpu.VMEM((tm, D), BF16)]
        + [pltpu.VMEM((nj, SUBLANES, tf), F32)] * 2
        + [pltpu.VMEM((tm, tf), BF16)] * 2,
        compiler_params=_params(("arbitrary", "arbitrary")),
        name="ffn_prompt",
    )(*args)


def _mlstm_intra(q, ks, v, ig_col, lf_cum_col, lf_cum_row, ig_row, m_prev_col, causal):
    log_d = jnp.where(causal, lf_cum_col - lf_cum_row + ig_row, NEG_INF)
    inter = lf_cum_col + m_prev_col
    m_t = jnp.maximum(inter, jnp.max(log_d, axis=1, keepdims=True))
    d_m = jnp.exp(log_d - m_t)
    w_inter = jnp.exp(inter - m_t)
    kb = _pad_rows(ks.astype(BF16), causal.shape[1])
    vb = _pad_rows(v.astype(BF16), causal.shape[1])
    s = _dot_nt(q.astype(BF16), kb) * d_m
    num = _dot(s.astype(BF16), vb)
    den = jnp.sum(s, axis=1, keepdims=True)
    return m_t, w_inter, inter, num, den


def _mlstm_prompt_kernel(q_ref, k_ref, v_ref, o_ref, g_ref, br_ref,
                         hg_ref, co_ref, no_ref, mo_ref, c_s, n_s, m_s, *, c):
    ci = pl.program_id(1)

    @pl.when(ci == 0)
    def _():
        c_s[...] = jnp.zeros_like(c_s)
        n_s[...] = jnp.zeros_like(n_s)
        m_s[...] = jnp.zeros_like(m_s)

    g = g_ref[...] + br_ref[...]
    lf_cum = _cumsum_rows(_log_sigmoid(g))
    gt = g.T
    lf_cum_t = lf_cum.T
    row = lax.broadcasted_iota(jnp.int32, (c, c), 0)
    col = lax.broadcasted_iota(jnp.int32, (c, c), 1)
    causal = col <= row
    last = ci == pl.num_programs(1) - 1

    for h in range(A_HEADS):
        ig_col = g[:, h:h + 1]
        b_col = lf_cum[:, A_HEADS + h:A_HEADS + h + 1]
        ig_row = gt[h:h + 1, :]
        b_row = lf_cum_t[A_HEADS + h:A_HEADS + h + 1, :]
        q = q_ref[:, h * A_DK:(h + 1) * A_DK]
        ks = k_ref[:, h * A_DK:(h + 1) * A_DK] * (A_DK ** -0.5)
        v = v_ref[:, h * A_DV:(h + 1) * A_DV]
        m_prev = m_s[h][:, 0:1]
        m_t, w_inter, inter, num, den = _mlstm_intra(q, ks, v, ig_col, b_col, b_row, ig_row, m_prev, causal)
        cmat = c_s[h]
        nvec = n_s[h]
        num = w_inter * _dot(q.astype(BF16), cmat.astype(BF16)) + num
        den = w_inter * jnp.sum(q * nvec, axis=1, keepdims=True) + den
        hh = num / jnp.maximum(jnp.abs(den), jnp.exp(-m_t))
        hg_ref[:, h * A_DV:(h + 1) * A_DV] = (_sigmoid(o_ref[:, h * A_DV:(h + 1) * A_DV]) * hh).astype(BF16)

        m_new = m_t[c - 1:c, :]
        w_state = jnp.exp(b_col[c - 1:c, :] - b_col + ig_col - m_new)
        decay = jnp.exp(inter[c - 1:c, :] - m_new)
        kw = ks * w_state
        c_new = decay * cmat + _dot_tn(kw.astype(BF16), v.astype(BF16))
        n_new = decay * nvec + jnp.sum(kw, axis=0, keepdims=True)
        m_row = jnp.broadcast_to(m_new, (1, LANES))
        c_s[h] = c_new
        n_s[h] = n_new
        m_s[h] = m_row

        @pl.when(last)
        def _():
            co_ref[0, h] = c_new
            no_ref[0, h] = n_new
            mo_ref[0, h] = m_row


def _mlstm_prompt(p, gates, bias_row, *, nseq, seq_len, c):
    T = p.shape[0]
    nc = seq_len // c
    hk = A_HEADS * A_DK
    hv = A_HEADS * A_DV
    rows = lambda b, ci: b * nc + ci
    return pl.pallas_call(
        functools.partial(_mlstm_prompt_kernel, c=c),
        grid=(nseq, nc),
        in_specs=[pl.BlockSpec((c, hk), lambda b, ci: (rows(b, ci), 0)),
                  pl.BlockSpec((c, hk), lambda b, ci: (rows(b, ci), 1)),
                  pl.BlockSpec((c, hv), lambda b, ci: (rows(b, ci), 1)),
                  pl.BlockSpec((c, hv), lambda b, ci: (rows(b, ci), 2)),
                  pl.BlockSpec((c, LANES), lambda b, ci: (rows(b, ci), 0)),
                  pl.BlockSpec((1, LANES), lambda b, ci: (0, 0))],
        out_specs=[pl.BlockSpec((c, hv), lambda b, ci: (rows(b, ci), 0)),
                   pl.BlockSpec((1, A_HEADS, A_DK, A_DV), lambda b, ci: (b, 0, 0, 0)),
                   pl.BlockSpec((1, A_HEADS, 1, A_DK), lambda b, ci: (b, 0, 0, 0)),
                   pl.BlockSpec((1, A_HEADS, 1, LANES), lambda b, ci: (b, 0, 0, 0))],
        out_shape=[SDS((T, hv), BF16),
                   SDS((nseq, A_HEADS, A_DK, A_DV), F32),
                   SDS((nseq, A_HEADS, 1, A_DK), F32),
                   SDS((nseq, A_HEADS, 1, LANES), F32)],
        scratch_shapes=[pltpu.VMEM((A_HEADS, A_DK, A_DV), F32), pltpu.VMEM((A_HEADS, 1, A_DK), F32),
                        pltpu.VMEM((A_HEADS, 1, LANES), F32)],
        compiler_params=_params(("arbitrary", "arbitrary")),
        name="mlstm_prompt",
    )(p, p, p, p, gates, bias_row)


def _mlstm_sample_kernel(q_ref, k_ref, v_ref, o_ref, g_ref, br_ref, mp_ref, ci_ref, ni_ref,
                         hg_ref, co_ref, no_ref, mo_ref, *, seq_len):
    nbs = SEQ_GROUP
    rs = nbs * seq_len
    g = g_ref[...] + br_ref[...]
    lf_all = _log_sigmoid(g)
    m_prev_all = mp_ref[...]
    row = lax.broadcasted_iota(jnp.int32, (rs, LANES), 0)
    col = lax.broadcasted_iota(jnp.int32, (rs, LANES), 1)
    same = ((row % nbs) == (col % nbs)) & (col < rs)
    causal = same & (col <= row)
    causal_t = same & (row <= col)
    eye = row == col
    last = same & (col // nbs == seq_len - 1)
    seq_of_row = lax.broadcasted_iota(jnp.int32, (rs, 1), 0) % nbs
    prow = lax.broadcasted_iota(jnp.int32, (LANES, 1), 0)
    seq_of_prow = jnp.where(prow < rs, prow % nbs, -1)

    def to_row(colvec):
        return jnp.sum(jnp.where(eye, colvec, 0.0), axis=0, keepdims=True)

    def at_last(rowvec):
        return jnp.sum(jnp.where(last, rowvec, 0.0), axis=1, keepdims=True)

    m_out = jnp.zeros((rs, LANES), F32)
    for h in range(A_HEADS):
        ig_col = g[:, h:h + 1]
        lf_col = lf_all[:, A_HEADS + h:A_HEADS + h + 1]
        m_prev = m_prev_all[:, h:h + 1]
        ig_row = to_row(ig_col)
        lf_row = to_row(lf_col)
        b_col = jnp.sum(jnp.where(causal, lf_row, 0.0), axis=1, keepdims=True)
        b_row = jnp.sum(jnp.where(causal_t, lf_col, 0.0), axis=0, keepdims=True)
        q = q_ref[:, h * A_DK:(h + 1) * A_DK]
        ks = k_ref[:, h * A_DK:(h + 1) * A_DK] * (A_DK ** -0.5)
        v = v_ref[:, h * A_DV:(h + 1) * A_DV]
        m_t, w_inter, inter, num, den = _mlstm_intra(q, ks, v, ig_col, b_col, b_row, ig_row, m_prev, causal)
        qb = q.astype(BF16)
        num_i = jnp.zeros((rs, A_DV), F32)
        qn = jnp.zeros((rs, 1), F32)
        for bb in range(nbs):
            mine = seq_of_row == bb
            num_i = jnp.where(mine, _dot(qb, ci_ref[bb, h].astype(BF16)), num_i)
            qn = jnp.where(mine, jnp.sum(q * ni_ref[bb, h], axis=1, keepdims=True), qn)
        num = w_inter * num_i + num
        den = w_inter * qn + den
        hh = num / jnp.maximum(jnp.abs(den), jnp.exp(-m_t))
        hg_ref[:, h * A_DV:(h + 1) * A_DV] = _sigmoid(o_ref[:, h * A_DV:(h + 1) * A_DV]) * hh

        m_new = at_last(to_row(m_t))
        w_state = jnp.exp(at_last(b_row) - b_col + ig_col - m_new)
        decay = jnp.exp(at_last(to_row(inter)) - m_new)
        kw = ks * w_state
        kw_t = _pad_rows(kw, LANES).T.astype(BF16)
        v_pad = _pad_rows(v, LANES)
        for bb in range(nbs):
            v_mine = jnp.where(seq_of_prow == bb, v_pad, 0.0).astype(BF16)
            dec = decay[bb:bb + 1, :]
            co_ref[bb, h] = dec * ci_ref[bb, h] + _dot(kw_t, v_mine)
            no_ref[bb, h] = dec * ni_ref[bb, h] + jnp.sum(jnp.where(seq_of_row == bb, kw, 0.0), axis=0, keepdims=True)
        m_out = jnp.where(col == h, m_new, m_out)
    mo_ref[...] = m_out


def _mlstm_sample(p, gates, bias_row, m_rows, c_in, n_in, *, seq_len):
    T = p.shape[0]
    nbs = SEQ_GROUP
    rs = nbs * seq_len
    nb = c_in.shape[0]
    hk = A_HEADS * A_DK
    hv = A_HEADS * A_DV
    return pl.pallas_call(
        functools.partial(_mlstm_sample_kernel, seq_len=seq_len),
        grid=(nb // nbs,),
        in_specs=[pl.BlockSpec((rs, hk), lambda i: (i, 0)),
                  pl.BlockSpec((rs, hk), lambda i: (i, 1)),
                  pl.BlockSpec((rs, hv), lambda i: (i, 1)),
                  pl.BlockSpec((rs, hv), lambda i: (i, 2)),
                  pl.BlockSpec((rs, LANES), lambda i: (i, 0)),
                  pl.BlockSpec((1, LANES), lambda i: (0, 0)),
                  pl.BlockSpec((rs, LANES), lambda i: (i, 0)),
                  pl.BlockSpec((nbs, A_HEADS, A_DK, A_DV), lambda i: (i, 0, 0, 0)),
                  pl.BlockSpec((nbs, A_HEADS, 1, A_DK), lambda i: (i, 0, 0, 0))],
        out_specs=[pl.BlockSpec((rs, hv), lambda i: (i, 0)),
                   pl.BlockSpec((nbs, A_HEADS, A_DK, A_DV), lambda i: (i, 0, 0, 0)),
                   pl.BlockSpec((nbs, A_HEADS, 1, A_DK), lambda i: (i, 0, 0, 0)),
                   pl.BlockSpec((rs, LANES), lambda i: (i, 0))],
        out_shape=[SDS((T, hv), F32),
                   SDS((nb, A_HEADS, A_DK, A_DV), F32),
                   SDS((nb, A_HEADS, 1, A_DK), F32),
                   SDS((T, LANES), F32)],
        compiler_params=_params(("arbitrary",)),
        name="mlstm_sample",
    )(p, p, p, p, gates, bias_row, m_rows, c_in, n_in)


def _pair_select(a, h0):
    lane = lax.broadcasted_iota(jnp.int32, (a.shape[0], LANES), 1)
    return jnp.where(lane < B_HEADDIM, a[:, h0:h0 + 1], a[:, h0 + 1:h0 + 2])


def _group_heads_first(a, g):
    return pltpu.roll(a, (LANES - HEADS_PER_GROUP * g) % LANES, 1)


def _group_norm_gate(y, z, nw):
    gz = y * _silu(z)
    return gz * lax.rsqrt(jnp.mean(gz * gz, axis=-1, keepdims=True) + EPS) * nw


def _ssd_prompt_kernel(x_ref, b_ref, c_ref, z_ref, dtc_ref,
                       cwx_ref, cwb_ref, cwc_ref, cbx_ref, cbb_ref, cbc_ref,
                       dbr_ref, alr_ref, d_ref, nw_ref,
                       y_ref, so_ref, s_s, xp_s, bp_s, cp_s, *, c, groups_per_step):
    ci = pl.program_id(2)

    @pl.when(ci == 0)
    def _():
        s_s[...] = jnp.zeros_like(s_s)
        xp_s[...] = jnp.zeros_like(xp_s)
        bp_s[...] = jnp.zeros_like(bp_s)
        cp_s[...] = jnp.zeros_like(cp_s)

    for gi in range(groups_per_step):
        _ssd_prompt_group(x_ref, b_ref, c_ref, z_ref, dtc_ref, cwx_ref, cwb_ref, cwc_ref, cbx_ref, cbb_ref, cbc_ref,
                          dbr_ref, alr_ref, d_ref, nw_ref, y_ref, s_s, xp_s, bp_s, cp_s,
                          c=c, gi=gi, group=pl.program_id(1) * groups_per_step + gi)

    @pl.when(ci == pl.num_programs(2) - 1)
    def _():
        so_ref[0] = s_s[...]


def _ssd_prompt_group(x_ref, b_ref, c_ref, z_ref, dtc_ref, cwx_ref, cwb_ref, cwc_ref, cbx_ref, cbb_ref, cbc_ref,
                      dbr_ref, alr_ref, d_ref, nw_ref, y_ref, s_s, xp_s, bp_s, cp_s, *, c, gi, group):
    ch = slice(gi * GROUP_CH, (gi + 1) * GROUP_CH)
    ns = slice(gi * B_DSTATE, (gi + 1) * B_DSTATE)
    ls = slice(gi * LANES, (gi + 1) * LANES)

    def conv(raw_ref, prev_s, cw_ref, cb_ref, sl):
        raw = raw_ref[:, sl]
        prev = prev_s[:, sl]
        out = _conv_taps(raw, lambda k: _shift_with_prev(raw, prev, k), cw_ref.at[:, sl], cb_ref.at[:, sl], B_DCONV)
        prev_s[:, sl] = raw[c - SUBLANES:, :]
        return _silu(out)

    x = conv(x_ref, xp_s, cwx_ref, cbx_ref, ch)
    bm = conv(b_ref, bp_s, cwb_ref, cbb_ref, ns)
    cm = conv(c_ref, cp_s, cwc_ref, cbc_ref, ns)

    dt_c = _softplus(_group_heads_first(dtc_ref[...], group) + dbr_ref[:, ls])
    ac_c = _cumsum_rows(dt_c * (-jnp.exp(alr_ref[:, ls])))
    ac_c2 = ac_c * LOG2E
    src_r2 = (ac_c2 - jnp.log(dt_c) * LOG2E).T[:HEADS_PER_GROUP, :]

    row = lax.broadcasted_iota(jnp.int32, (c, c), 0)
    col = lax.broadcasted_iota(jnp.int32, (c, c), 1)
    bmb = bm.astype(BF16)
    cmb = cm.astype(BF16)
    cb = jnp.where(col <= row, _dot_nt(cmb, bmb), 0.0)
    state = s_s[gi]
    y_inter = _dot(cmb, state.astype(BF16))
    ac_last = ac_c[c - 1:c, :]
    to_end = jnp.exp(ac_last - ac_c) * dt_c
    e_ac = jnp.exp(ac_c)
    e_last = jnp.exp(ac_last)
    lane = lax.broadcasted_iota(jnp.int32, (c, LANES), 1)
    d_all = d_ref[:, ch]

    ys = []
    for pr in range(HEADS_PER_GROUP // 2):
        h0 = 2 * pr
        sl = slice(pr * LANES, (pr + 1) * LANES)
        xp = x[:, sl]
        xpb = xp.astype(BF16)
        full = []
        for r in (h0, h0 + 1):
            blocks = []
            for rb in range(c // LANES):
                rsl = slice(rb * LANES, (rb + 1) * LANES)
                ncol = (rb + 1) * LANES
                fac = jnp.exp2(jnp.minimum(ac_c2[rsl, r:r + 1] - src_r2[r:r + 1, :ncol], MAX_EXP2))
                blocks.append(_dot((cb[rsl, :ncol] * fac).astype(BF16), xpb[:ncol]))
            full.append(jnp.concatenate(blocks, axis=0))
        yp = jnp.where(lane < B_HEADDIM, full[0], full[1])
        yp = yp + y_inter[:, sl] * _pair_select(e_ac, h0) + d_all[:, sl] * xp
        ys.append(yp)
        xw = (xp * _pair_select(to_end, h0)).astype(BF16)
        s_s[gi, :, sl] = _pair_select(e_last, h0) * state[:, sl] + _dot_tn(bmb, xw)
    y = jnp.concatenate(ys, axis=1)
    y_ref[:, ch] = _group_norm_gate(y, z_ref[:, ch], nw_ref[:, ch]).astype(BF16)


def _ssd_prompt(p, dt_cols, w, *, nseq, seq_len, c, groups_per_step):
    T = p.shape[0]
    nc = seq_len // c
    gps = groups_per_step
    steps = B_GROUPS // gps
    xw_, nw_, lw_ = gps * GROUP_CH, gps * B_DSTATE, gps * LANES
    rows = lambda b, g, ci: b * nc + ci
    kx = B_DINNER // xw_
    kb = (2 * B_DINNER) // nw_
    kc = kb + steps
    wb = B_DINNER // nw_
    wc = wb + steps
    per_lane = lambda width, start=0: pl.BlockSpec((1, width), lambda b, g, ci: (0, start + g))
    return pl.pallas_call(
        functools.partial(_ssd_prompt_kernel, c=c, groups_per_step=gps),
        grid=(nseq, steps, nc),
        in_specs=[pl.BlockSpec((c, xw_), lambda b, g, ci: (rows(b, g, ci), kx + g)),
                  pl.BlockSpec((c, nw_), lambda b, g, ci: (rows(b, g, ci), kb + g)),
                  pl.BlockSpec((c, nw_), lambda b, g, ci: (rows(b, g, ci), kc + g)),
                  pl.BlockSpec((c, xw_), lambda b, g, ci: (rows(b, g, ci), g)),
                  pl.BlockSpec((c, LANES), lambda b, g, ci: (rows(b, g, ci), 0)),
                  pl.BlockSpec((None, B_DCONV, xw_), lambda b, g, ci: (0, 0, g)),
                  pl.BlockSpec((None, B_DCONV, nw_), lambda b, g, ci: (0, 0, wb + g)),
                  pl.BlockSpec((None, B_DCONV, nw_), lambda b, g, ci: (0, 0, wc + g)),
                  per_lane(xw_), per_lane(nw_, wb), per_lane(nw_, wc),
                  per_lane(lw_), per_lane(lw_),
                  per_lane(xw_), per_lane(xw_)],
        out_specs=[pl.BlockSpec((c, xw_), lambda b, g, ci: (rows(b, g, ci), g)),
                   pl.BlockSpec((1, gps, B_DSTATE, GROUP_CH), lambda b, g, ci: (b, g, 0, 0))],
        out_shape=[SDS((T, B_DINNER), BF16), SDS((nseq, B_GROUPS, B_DSTATE, GROUP_CH), F32)],
        scratch_shapes=[pltpu.VMEM((gps, B_DSTATE, GROUP_CH), F32),
                        pltpu.VMEM((SUBLANES, xw_), F32),
                        pltpu.VMEM((SUBLANES, nw_), F32),
                        pltpu.VMEM((SUBLANES, nw_), F32)],
        compiler_params=_params(("arbitrary", "arbitrary", "arbitrary")),
        name="ssd_prompt",
    )(p, p, p, p, dt_cols, w["b_conv_w"], w["b_conv_w"], w["b_conv_w"],
      w["b_conv_b"], w["b_conv_b"], w["b_conv_b"],
      w["dtb_cols"], w["alog_cols"], w["d_lane"], w["b_norm_w"])


def _ssd_sample_kernel(x_ref, b_ref, c_ref, z_ref, dt_ref, sx_ref, sb_ref, sc_ref,
                       cwx_ref, cwb_ref, cwc_ref, cbx_ref, cbb_ref, cbc_ref,
                       dtb_ref, al_ref, d_ref, nw_ref, hs_ref,
                       y_ref, hso_ref, *, seq_len, groups_per_step):
    for gi in range(groups_per_step):
        _ssd_sample_group(x_ref, b_ref, c_ref, z_ref, dt_ref, sx_ref, sb_ref, sc_ref,
                          cwx_ref, cwb_ref, cwc_ref, cbx_ref, cbb_ref, cbc_ref,
                          dtb_ref, al_ref, d_ref, nw_ref, hs_ref, y_ref, hso_ref,
                          seq_len=seq_len, gi=gi, group=pl.program_id(1) * groups_per_step + gi)


def _ssd_sample_group(x_ref, b_ref, c_ref, z_ref, dt_ref, sx_ref, sb_ref, sc_ref,
                      cwx_ref, cwb_ref, cwc_ref, cbx_ref, cbb_ref, cbc_ref,
                      dtb_ref, al_ref, d_ref, nw_ref, hs_ref, y_ref, hso_ref, *, seq_len, gi, group):
    nbs = SEQ_GROUP
    rs = nbs * seq_len
    sr = _SampleRows(rs, seq_len)
    prow = lax.broadcasted_iota(jnp.int32, (LANES, 1), 0)
    seq_of_prow = jnp.where(prow < rs, prow % nbs, -1)
    ch = slice(gi * GROUP_CH, (gi + 1) * GROUP_CH)
    ns = slice(gi * B_DSTATE, (gi + 1) * B_DSTATE)
    ls = slice(gi * LANES, (gi + 1) * LANES)
    h_first = gi * HEADS_PER_GROUP

    def conv(raw_ref, st_ref, cw_ref, cb_ref, sl):
        raw = raw_ref[:, sl]
        return _silu(_conv_taps(raw, sr.conv_shift(raw, st_ref[:, sl], B_DCONV), cw_ref.at[:, sl], cb_ref.at[:, sl],
                                B_DCONV))

    x = conv(x_ref, sx_ref, cwx_ref, cbx_ref, ch)
    bm = conv(b_ref, sb_ref, cwb_ref, cbb_ref, ns)
    cm = conv(c_ref, sc_ref, cwc_ref, cbc_ref, ns)

    dt = _softplus(_group_heads_first(dt_ref[...], group) + dtb_ref[:, ls])
    a = dt * (-jnp.exp(al_ref[:, ls]))
    acum = a
    total = a
    for k in range(1, seq_len):
        acum = acum + sr.back(a, k)
        total = total + sr.back(a, k) + sr.ahead(a, k)

    coefs = []
    for d in range(seq_len):
        cb_d = jnp.sum(cm * sr.back(bm, d), axis=1, keepdims=True)
        coefs.append(cb_d * jnp.exp(acum - sr.back(acum, d)) * sr.back(dt, d))
    x_back = [sr.back(x, d) for d in range(seq_len)]

    e_ac = jnp.exp(acum)
    e_tot = jnp.exp(total)
    to_end = jnp.exp(total - acum) * dt
    d_all = d_ref[:, ch]

    ys = []
    xws = []
    for pr in range(HEADS_PER_GROUP // 2):
        h0 = 2 * pr
        sl = slice(pr * LANES, (pr + 1) * LANES)
        yp = d_all[:, sl] * x[:, sl]
        for d in range(seq_len):
            yp = yp + _pair_select(coefs[d], h0) * x_back[d][:, sl]
        ys.append(yp)
        xws.append(x[:, sl] * _pair_select(to_end, h0))
    xw_t = _pad_rows(jnp.concatenate(xws, axis=1), LANES).T.astype(BF16)
    cmb = cm.astype(BF16)
    bm_pad = _pad_rows(bm, LANES)
    y_int = jnp.zeros((rs, GROUP_CH), F32)
    for bb in range(nbs):
        hs_g = hs_ref[bb, h_first:h_first + HEADS_PER_GROUP].reshape(GROUP_CH, B_DSTATE)
        y_int = jnp.where(sr.seq == bb, _dot_nt(cmb, hs_g.astype(BF16)), y_int)
        b_mine = jnp.where(seq_of_prow == bb, bm_pad, 0.0).astype(BF16)
        upd = _dot(xw_t, b_mine)
        for r in range(HEADS_PER_GROUP):
            dec = e_tot[bb:bb + 1, r:r + 1]
            hso_ref[bb, h_first + r] = dec * hs_ref[bb, h_first + r] + upd[r * B_HEADDIM:(r + 1) * B_HEADDIM, :]
    e_pairs = jnp.concatenate([_pair_select(e_ac, 2 * pr) for pr in range(HEADS_PER_GROUP // 2)], axis=1)
    y = jnp.concatenate(ys, axis=1) + y_int * e_pairs
    y_ref[:, ch] = _group_norm_gate(y, z_ref[:, ch], nw_ref[:, ch])


def _ssd_sample(p, dt_cols, conv_state, w, hs, *, seq_len, groups_per_step):
    T = p.shape[0]
    nbs = SEQ_GROUP
    rs = nbs * seq_len
    nb = hs.shape[0]
    gps = groups_per_step
    steps = B_GROUPS // gps
    xw_, nw_, lw_ = gps * GROUP_CH, gps * B_DSTATE, gps * LANES
    kx = B_DINNER // xw_
    kb = (2 * B_DINNER) // nw_
    kc = kb + steps
    wb = B_DINNER // nw_
    wc = wb + steps
    per_lane = lambda width, start=0: pl.BlockSpec((1, width), lambda i, g: (0, start + g))
    hs_spec = pl.BlockSpec((nbs, gps * HEADS_PER_GROUP, B_HEADDIM, B_DSTATE), lambda i, g: (i, g, 0, 0))
    return pl.pallas_call(
        functools.partial(_ssd_sample_kernel, seq_len=seq_len, groups_per_step=gps),
        grid=(nb // nbs, steps),
        in_specs=[pl.BlockSpec((rs, xw_), lambda i, g: (i, kx + g)),
                  pl.BlockSpec((rs, nw_), lambda i, g: (i, kb + g)),
                  pl.BlockSpec((rs, nw_), lambda i, g: (i, kc + g)),
                  pl.BlockSpec((rs, xw_), lambda i, g: (i, g)),
                  pl.BlockSpec((rs, LANES), lambda i, g: (i, 0)),
                  pl.BlockSpec((rs, xw_), lambda i, g: (i, g)),
                  pl.BlockSpec((rs, nw_), lambda i, g: (i, wb + g)),
                  pl.BlockSpec((rs, nw_), lambda i, g: (i, wc + g)),
                  pl.BlockSpec((None, B_DCONV, xw_), lambda i, g: (0, 0, g)),
                  pl.BlockSpec((None, B_DCONV, nw_), lambda i, g: (0, 0, wb + g)),
                  pl.BlockSpec((None, B_DCONV, nw_), lambda i, g: (0, 0, wc + g)),
                  per_lane(xw_), per_lane(nw_, wb), per_lane(nw_, wc),
                  per_lane(lw_), per_lane(lw_), per_lane(xw_), per_lane(xw_),
                  hs_spec],
        out_specs=[pl.BlockSpec((rs, xw_), lambda i, g: (i, g)), hs_spec],
        out_shape=[SDS((T, B_DINNER), F32), SDS(hs.shape, F32)],
        compiler_params=_params(("arbitrary", "arbitrary")),
        name="ssd_sample",
    )(p, p, p, p, dt_cols, conv_state, conv_state, conv_state,
      w["b_conv_w"], w["b_conv_w"], w["b_conv_w"], w["b_conv_b"], w["b_conv_b"], w["b_conv_b"],
      w["dtb_cols"], w["alog_cols"], w["d_lane"], w["b_norm_w"], hs)


def _pad_lanes(a, width=LANES):
    return jnp.pad(a, ((0, 0), (0, width - a.shape[1])))


def _per_group_lanes(a):
    r = a.shape[0]
    a = a.reshape(r, B_GROUPS, HEADS_PER_GROUP)
    return jnp.pad(a, ((0, 0), (0, 0), (0, LANES - HEADS_PER_GROUP))).reshape(r, B_GROUPS * LANES)


def _to_sample_rows(a, seq_len):
    nseq, nt, ch = a.shape
    a = a.reshape(nseq // SEQ_GROUP, SEQ_GROUP, nt, ch).transpose(0, 2, 1, 3)
    a = jnp.pad(a, ((0, 0), (0, seq_len - nt), (0, 0), (0, 0)))
    return a.reshape(nseq * seq_len, ch)


def _from_sample_rows(a, nseq, seq_len, first_token=0):
    ch = a.shape[1]
    a = a.reshape(nseq // SEQ_GROUP, seq_len, SEQ_GROUP, ch)[:, first_token:]
    return a.transpose(0, 2, 1, 3).reshape(nseq, seq_len - first_token, ch)


def _forward(x, nseq, seq_len, state, w, *, prompt):
    T = x.shape[0]
    tm = PROJ_ROWS if prompt else T
    a_main = 2 * A_HEADS * A_DK + 2 * A_HEADS * A_DV
    b_main = B_DINNER + B_CONV_DIM
    nstate = FFN_CONV - 1
    cast = {"ffn_bf16": [None, None]}

    def keep_cast(name, outs):
        if not prompt:
            cast[name] = outs[-1]

    outs = _norm_matmul(x, w["norm_mix"], 0, w["a_w_in"], a_main, w["a_w_gate"], tm=tm, tn=PROJ_COLS)
    p_a, gates = outs[:2]
    keep_cast("a_w_in", outs)
    if prompt:
        hg, m_c, m_n, m_m = _mlstm_prompt(p_a, gates, w["a_b_row"], nseq=nseq, seq_len=seq_len, c=SCAN_CHUNK)
        m_m = m_m[:, :, 0, 0]
    else:
        c0, n0, m0 = state["mlstm"]
        m_rows = _pad_lanes(_to_sample_rows(jnp.broadcast_to(m0[:, None, :], (nseq, seq_len, A_HEADS)), seq_len))
        hg, m_c, m_n, m_rows_new = _mlstm_sample(p_a, gates, w["a_b_row"], m_rows, c0, n0[:, :, None, :],
                                                 seq_len=seq_len)
        m_m = _from_sample_rows(m_rows_new, nseq, seq_len, seq_len - 1)[:, 0, :A_HEADS]
    m_n = m_n[:, :, 0, :]
    outs = _matmul_res(hg, w["a_w_out"], x, tm=tm, tn=OUT_COLS)
    x = outs[0]
    keep_cast("a_w_out", outs)

    def ffn(x, layer, final_w=None):
        if prompt:
            x_new, tg, tv = _ffn_prompt(x, layer, w, tm=FFN_ROWS, tf=FFN_COLS, seq_len=seq_len, final_w=final_w)
            tps = seq_len // FFN_ROWS
            pick = lambda t: t.reshape(nseq, tps, SUBLANES, D_FF)[:, tps - 1, SUBLANES - nstate:]
            tails = jnp.concatenate([pick(tg), pick(tv)], axis=-1)
        else:
            outs = _ffn_sample(x, layer, w, state["ffn_conv"], tf=FFN_SAMPLE_COLS, seq_len=seq_len, final_w=final_w)
            x_new = outs[0]
            tails = jnp.concatenate(outs[1:3], axis=-1)
            cast["ffn_bf16"][layer] = tuple(outs[3:])
        return x_new, tails

    x, f_conv0 = ffn(x, 0)

    outs = _norm_matmul(x, w["norm_mix"], 1, w["b_w_in"], b_main, w["b_w_dt"], tm=tm, tn=PROJ_COLS)
    p_b, dt_raw = outs[:2]
    keep_cast("b_w_in", outs)
    if prompt:
        s_conv = p_b.reshape(nseq, seq_len, b_main)[:, seq_len - (B_DCONV - 1):, B_DINNER:]
        yn, s_t = _ssd_prompt(p_b, dt_raw, w, nseq=nseq, seq_len=seq_len, c=SCAN_CHUNK,
                              groups_per_step=SSD_PROMPT_GROUPS)
        ssm = s_t.reshape(nseq, B_GROUPS, B_DSTATE, HEADS_PER_GROUP, B_HEADDIM)
        ssm = ssm.transpose(0, 1, 3, 4, 2).reshape(nseq, B_HEADS, B_HEADDIM, B_DSTATE)
    else:
        s_conv = _from_sample_rows(p_b, nseq, seq_len, seq_len - (B_DCONV - 1))[:, :, B_DINNER:]
        st = _to_sample_rows(state["ssm_conv"], seq_len)
        yn, ssm = _ssd_sample(p_b, dt_raw, st, w, state["ssm"], seq_len=seq_len, groups_per_step=SSD_SAMPLE_GROUPS)
    outs = _matmul_res(yn, w["b_w_out"], x, tm=tm, tn=OUT_COLS)
    x = outs[0]
    keep_cast("b_w_out", outs)

    y, f_conv1 = ffn(x, 1, final_w=w["norm_final"])
    return (y, m_c[None], m_n[None], m_m[None], ssm[None], s_conv[None], jnp.stack([f_conv0, f_conv1]), cast)


def kernel(x_prompt, x_sample, state_mlstm_C, state_mlstm_n, state_mlstm_m, state_ssm, state_ssm_conv,
           state_ffn_conv, norm_mix, norm_ffn, norm_final, a_w_in, a_b_if, a_w_out, b_w_in, b_conv_w,
           b_conv_b, b_dt_bias, b_A_log, b_D, b_norm_w, b_w_out, f_w_up, f_conv_w, f_conv_b, f_w_down):
    a_main = 2 * A_HEADS * A_DK + 2 * A_HEADS * A_DV
    b_main = B_DINNER + B_CONV_DIM
    w = {
        "norm_mix": norm_mix[:, None], "norm_ffn": norm_ffn[:, None], "norm_final": norm_final[None],
        "a_w_in": jnp.swapaxes(a_w_in, 1, 2),
        "a_w_gate": _pad_lanes(a_w_in[0, :, a_main:]).astype(BF16),
        "a_b_row": _pad_lanes(a_b_if),
        "a_w_out": a_w_out,
        "b_w_in": jnp.swapaxes(b_w_in, 1, 2),
        "b_w_dt": _pad_lanes(b_w_in[0, :, b_main:]).astype(BF16),
        "b_conv_w": b_conv_w, "b_conv_b": b_conv_b,
        "dtb_cols": _per_group_lanes(b_dt_bias), "alog_cols": _per_group_lanes(b_A_log),
        "d_lane": jnp.repeat(b_D, B_HEADDIM, axis=1), "b_norm_w": b_norm_w,
        "b_w_out": b_w_out,
        "f_w_up": f_w_up, "f_conv_w": f_conv_w, "f_conv_b": f_conv_b[:, None],
        "f_w_down": f_w_down,
    }
    bp, lp, _ = x_prompt.shape
    bs, ls, _ = x_sample.shape
    state = {"mlstm": (state_mlstm_C[0], state_mlstm_n[0], state_mlstm_m[0]),
             "ssm": state_ssm[0], "ssm_conv": state_ssm_conv[0], "ffn_conv": state_ffn_conv}
    ys, sc, sn, sm, sssm, ssconv, sfconv, w_bf16 = _forward(_to_sample_rows(x_sample, ls), bs, ls, state, w,
                                                            prompt=False)
    yp, pc, pn, pm, pssm, psconv, pfconv, _ = _forward(x_prompt.reshape(bp * lp, D_MODEL), bp, lp, None,
                                                       {**w, **w_bf16}, prompt=True)
    return (yp.reshape(bp, lp, D_MODEL), _from_sample_rows(ys, bs, ls), pc, sc, pn, sn, pm, sm,
            pssm, sssm, psconv, ssconv, pfconv, sfconv)
```

```python
import functools

import jax
import jax.numpy as jnp
from jax import lax
from jax.experimental import pallas as pl
from jax.experimental.pallas import tpu as pltpu

F32 = jnp.float32
BF16 = jnp.bfloat16
SDS = jax.ShapeDtypeStruct

D_MODEL = 2048
A_HEADS = 8
A_DK = 128
A_DV = 256
B_DINNER = 4096
B_HEADDIM = 64
B_HEADS = 64
B_GROUPS = 8
B_DSTATE = 128
B_DCONV = 4
B_CONV_DIM = 6144
D_FF = 5632
FFN_CONV = 3
EPS = 1e-6
NEG_INF = float("-inf")
LOG2E = 1.4426950408889634
MAX_EXP2 = 126.0

LANES = 128
SUBLANES = 8
VMEM_LIMIT_BYTES = 56 * 1024 * 1024

PROJ_ROWS = 1024
PROJ_COLS = 1024
OUT_COLS = 512
FFN_ROWS = 1024
FFN_COLS = 512
FFN_SAMPLE_COLS = 256
MLSTM_CHUNK = 512
SCAN_CHUNK = 256
SSD_PROMPT_GROUPS = 2
SSD_SAMPLE_GROUPS = 4

HEADS_PER_GROUP = B_HEADS // B_GROUPS
GROUP_CH = HEADS_PER_GROUP * B_HEADDIM
SEQ_GROUP = SUBLANES


def _params(sem, flags=None):
    return pltpu.CompilerParams(dimension_semantics=sem, vmem_limit_bytes=VMEM_LIMIT_BYTES, flags=flags)


def _sigmoid(x):
    return 1.0 / (1.0 + jnp.exp(-x))


def _silu(x):
    h = 0.5 * x
    return h + h * jnp.tanh(h)


def _softplus(x):
    return jnp.maximum(x, 0.0) + jnp.log1p(jnp.exp(-jnp.abs(x)))


def _log_sigmoid(x):
    return -_softplus(-x)


def _rms(x, w):
    ms = jnp.mean(x * x, axis=-1, keepdims=True)
    return x * lax.rsqrt(ms + EPS) * w


def _cumsum_rows(x):
    n = x.shape[0]
    row = lax.broadcasted_iota(jnp.int32, x.shape, 0)
    k = 1
    while k < n:
        x = x + jnp.where(row >= k, pltpu.roll(x, k, 0), 0.0)
        k *= 2
    return x


def _dot(a, b):
    return jnp.dot(a, b, preferred_element_type=F32)


def _dot_nt(a, b):
    return lax.dot_general(a, b, (((1,), (1,)), ((), ())), preferred_element_type=F32)


def _dot_tn(a, b):
    return lax.dot_general(a, b, (((0,), (0,)), ((), ())), preferred_element_type=F32)


def _conv_taps(u, shifted, w_ref, b_ref, ntaps):
    y = u * w_ref[ntaps - 1:ntaps, :]
    for k in range(1, ntaps):
        y = y + shifted(k) * w_ref[ntaps - 1 - k:ntaps - k, :]
    return y + b_ref[...]


def _shift_with_prev(u, prev8, k):
    ext = jnp.concatenate([prev8, u], axis=0)
    return pltpu.roll(ext, k, 0)[SUBLANES:]


def _pad_rows(a, rows):
    if a.shape[0] == rows:
        return a
    return jnp.concatenate([a, jnp.zeros((rows - a.shape[0], a.shape[1]), a.dtype)], axis=0)


class _SampleRows:
    def __init__(self, rows, seq_len):
        self.rows = rows
        self.seq_len = seq_len
        r = lax.broadcasted_iota(jnp.int32, (rows, 1), 0)
        self.token = (r // SEQ_GROUP) % seq_len
        self.seq = r % SEQ_GROUP

    def back(self, a, k):
        if k == 0:
            return a
        return jnp.where(self.token >= k, pltpu.roll(a, SEQ_GROUP * k, 0), 0.0)

    def ahead(self, a, k):
        if k == 0:
            return a
        return jnp.where(self.token + k < self.seq_len, pltpu.roll(a, self.rows - SEQ_GROUP * k, 0), 0.0)

    def conv_shift(self, u, state, ntaps):
        return lambda k: self.back(u, k) + self.ahead(state, ntaps - 1 - k)


def _tokens_to_rows(slots, seq_len):
    nseq, ch = slots[0].shape
    pad = jnp.zeros(((seq_len - len(slots)) * SEQ_GROUP, ch), slots[0].dtype)
    parts = []
    for g in range(nseq // SEQ_GROUP):
        parts += [s[g * SEQ_GROUP:(g + 1) * SEQ_GROUP] for s in slots] + [pad]
    return jnp.concatenate(parts, axis=0)


def _rows_of_token(a, seq_len, t):
    step = seq_len * SEQ_GROUP
    return jnp.concatenate([a[g * step + t * SEQ_GROUP:g * step + (t + 1) * SEQ_GROUP]
                            for g in range(a.shape[0] // step)], axis=0)


def _norm_matmul_kernel(x_ref, nw_ref, w_ref, ws_ref, o_ref, os_ref, *rest):
    hn_ref = rest[-1]

    @pl.when(pl.program_id(1) == 0)
    def _():
        hn = _rms(x_ref[...], nw_ref[...]).astype(BF16)
        hn_ref[...] = hn
        os_ref[...] = _dot(hn, ws_ref[...])

    wb = w_ref[...].astype(BF16)
    if len(rest) == 2:
        rest[0][...] = wb
    o_ref[...] = _dot_nt(hn_ref[...], wb)


def _norm_matmul(x, nw, layer, w_t, n_main, ws, *, tm, tn):
    T, D = x.shape
    emit = w_t.dtype != BF16
    w_spec = pl.BlockSpec((None, tn, D), lambda i, j: (0, j, 0))
    out_specs = [pl.BlockSpec((tm, tn), lambda i, j: (i, j)), pl.BlockSpec((tm, LANES), lambda i, j: (i, 0))]
    out_shape = [SDS((T, n_main), F32), SDS((T, LANES), F32)]
    if emit:
        assert T == tm, "the bfloat16 copy is written once, by a single row tile"
        out_specs.append(w_spec)
        out_shape.append(SDS((1, n_main, D), BF16))
    return pl.pallas_call(
        _norm_matmul_kernel,
        grid=(T // tm, n_main // tn),
        in_specs=[pl.BlockSpec((tm, D), lambda i, j: (i, 0)),
                  pl.BlockSpec((None, 1, D), lambda i, j: (layer, 0, 0)),
                  w_spec,
                  pl.BlockSpec((D, LANES), lambda i, j: (0, 0))],
        out_specs=out_specs,
        out_shape=out_shape,
        scratch_shapes=[pltpu.VMEM((tm, D), BF16)],
        compiler_params=_params(("arbitrary", "arbitrary")),
        name="norm_matmul",
    )(x, nw, w_t, ws)


def _matmul_res_kernel(a_ref, w_ref, r_ref, o_ref, *rest):
    wb = w_ref[...].astype(BF16)
    if rest:
        rest[0][...] = wb
    o_ref[...] = r_ref[...] + _dot(a_ref[...].astype(BF16), wb)


def _matmul_res(a, w, res, *, tm, tn):
    T, K = a.shape
    N = w.shape[2]
    emit = w.dtype != BF16
    w_spec = pl.BlockSpec((None, K, tn), lambda i, j: (0, 0, j))
    out_specs = [pl.BlockSpec((tm, tn), lambda i, j: (i, j))]
    out_shape = [SDS((T, N), F32)]
    if emit:
        assert T == tm, "the bfloat16 copy is written once, by a single row tile"
        out_specs.append(w_spec)
        out_shape.append(SDS((1, K, N), BF16))
    return pl.pallas_call(
        _matmul_res_kernel,
        grid=(T // tm, N // tn),
        in_specs=[pl.BlockSpec((tm, K), lambda i, j: (i, 0)),
                  w_spec,
                  pl.BlockSpec((tm, tn), lambda i, j: (i, j))],
        out_specs=out_specs,
        out_shape=out_shape,
        compiler_params=_params(("arbitrary", "arbitrary")),
        name="matmul_res",
    )(a, w, res)


def _ffn_sample_kernel(*refs, tm, seq_len, final_norm):
    refs = list(refs)
    x_ref, nw_ref, wug_ref, wuv_ref, cwg_ref, cwv_ref, cbg_ref, cbv_ref, wd_ref = refs[:9]
    del refs[:9]
    nstate = FFN_CONV - 1
    sg_ref, sv_ref = refs[:2]
    del refs[:2]
    fw_ref = refs.pop(0) if final_norm else None
    out_ref = refs.pop(0)
    tg_ref, tv_ref, wbg_ref, wbv_ref, wbd_ref, hn_s, acc_s = refs
    j = pl.program_id(1)

    @pl.when(j == 0)
    def _():
        hn_s[...] = _rms(x_ref[...], nw_ref[...]).astype(BF16)
        acc_s[...] = jnp.zeros_like(acc_s)

    wug = wug_ref[...].astype(BF16)
    wuv = wuv_ref[...].astype(BF16)
    wd = wd_ref[...].astype(BF16)
    hn = hn_s[...]
    ug = _dot(hn, wug)
    uv = _dot(hn, wuv)

    wbg_ref[...] = wug
    wbv_ref[...] = wuv
    wbd_ref[...] = wd
    for t in range(nstate):
        tg_ref[:, t, :] = _rows_of_token(ug, seq_len, seq_len - nstate + t)
        tv_ref[:, t, :] = _rows_of_token(uv, seq_len, seq_len - nstate + t)
    sr = _SampleRows(tm, seq_len)
    sg = _tokens_to_rows([sg_ref[:, t, :] for t in range(nstate)], seq_len)
    sv = _tokens_to_rows([sv_ref[:, t, :] for t in range(nstate)], seq_len)
    yg = _conv_taps(ug, sr.conv_shift(ug, sg, FFN_CONV), cwg_ref, cbg_ref, FFN_CONV)
    yv = _conv_taps(uv, sr.conv_shift(uv, sv, FFN_CONV), cwv_ref, cbv_ref, FFN_CONV)

    act = (_silu(yg) * yv).astype(BF16)
    acc_s[...] += _dot(act, wd)

    @pl.when(j == pl.num_programs(1) - 1)
    def _():
        y = x_ref[...] + acc_s[...]
        out_ref[...] = _rms(y, fw_ref[...]) if final_norm else y


def _ffn_sample(x, layer, w, state, *, tf, seq_len, final_w=None):
    T, D = x.shape
    nj = D_FF // tf
    nstate = FFN_CONV - 1
    nseq = T // seq_len
    in_specs = [pl.BlockSpec((T, D), lambda i, j: (0, 0)),
                pl.BlockSpec((None, 1, D), lambda i, j: (layer, 0, 0)),
                pl.BlockSpec((None, D, tf), lambda i, j: (layer, 0, j)),
                pl.BlockSpec((None, D, tf), lambda i, j: (layer, 0, nj + j)),
                pl.BlockSpec((None, FFN_CONV, tf), lambda i, j: (layer, 0, j)),
                pl.BlockSpec((None, FFN_CONV, tf), lambda i, j: (layer, 0, nj + j)),
                pl.BlockSpec((None, 1, tf), lambda i, j: (layer, 0, j)),
                pl.BlockSpec((None, 1, tf), lambda i, j: (layer, 0, nj + j)),
                pl.BlockSpec((None, tf, D), lambda i, j: (layer, j, 0))]
    args = [x, w["norm_ffn"], w["f_w_up"], w["f_w_up"], w["f_conv_w"], w["f_conv_w"], w["f_conv_b"], w["f_conv_b"],
            w["f_w_down"]]
    in_specs += [pl.BlockSpec((None, nseq, nstate, tf), lambda i, j: (layer, 0, 0, j)),
                 pl.BlockSpec((None, nseq, nstate, tf), lambda i, j: (layer, 0, 0, nj + j))]
    args += [state, state]
    if final_w is not None:
        in_specs.append(pl.BlockSpec((1, D), lambda i, j: (0, 0)))
        args.append(final_w)
    out_specs = [pl.BlockSpec((T, D), lambda i, j: (0, 0))]
    out_specs += [pl.BlockSpec((nseq, nstate, tf), lambda i, j: (0, 0, j))] * 2
    out_specs += [pl.BlockSpec((D, tf), lambda i, j: (0, j))] * 2 + [pl.BlockSpec((tf, D), lambda i, j: (j, 0))]
    out_shape = [SDS((T, D), F32)] + [SDS((nseq, nstate, D_FF), F32)] * 2
    out_shape += [SDS((D, D_FF), BF16)] * 2 + [SDS((D_FF, D), BF16)]
    return pl.pallas_call(
        functools.partial(_ffn_sample_kernel, tm=T, seq_len=seq_len, final_norm=final_w is not None),
        grid=(1, nj),
        in_specs=in_specs,
        out_specs=out_specs,
        out_shape=out_shape,
        scratch_shapes=[pltpu.VMEM((T, D), BF16), pltpu.VMEM((T, D), F32)],
        compiler_params=_params(("arbitrary", "arbitrary")),
        name="ffn_sample",
    )(*args)


def _ffn_prompt_kernel(x_ref, nw_ref, wug_ref, wuv_ref, cwg_ref, cwv_ref, cbg_ref, cbv_ref, wd_ref, *rest,
                       tm, tiles_per_seq, nj, final_norm):
    rest = list(rest)
    fw_ref = rest.pop(0) if final_norm else None
    out_ref, tg_ref, tv_ref, hn_s, cg_s, cv_s = rest[:6]
    act_slots = rest[6:8]
    i = pl.program_id(0)
    j = pl.program_id(1)

    def up_act(slot):
        hn = hn_s[...]
        ug = _dot(hn, wug_ref[...])
        uv = _dot(hn, wuv_ref[...])
        tail_g = ug[tm - SUBLANES:, :]
        tail_v = uv[tm - SUBLANES:, :]
        tg_ref[0] = tail_g
        tv_ref[0] = tail_v
        prev_g = cg_s[j]
        prev_v = cv_s[j]
        yg = _conv_taps(ug, lambda k: _shift_with_prev(ug, prev_g, k), cwg_ref, cbg_ref, FFN_CONV)
        yv = _conv_taps(uv, lambda k: _shift_with_prev(uv, prev_v, k), cwv_ref, cbv_ref, FFN_CONV)
        cg_s[j] = tail_g
        cv_s[j] = tail_v
        act_slots[slot][...] = (_silu(yg) * yv).astype(BF16)

    def down_proj(slot):
        out_ref[...] += _dot(act_slots[slot][...], wd_ref[...])

    @pl.when(j == 0)
    def _():
        x = x_ref[...]
        hn_s[...] = _rms(x, nw_ref[...]).astype(BF16)
        out_ref[...] = x

        @pl.when(i % tiles_per_seq == 0)
        def _():
            cg_s[...] = jnp.zeros_like(cg_s)
            cv_s[...] = jnp.zeros_like(cv_s)

        up_act(0)

    steady = (j >= 1) & (j < nj)
    for parity in (0, 1):
        @pl.when(steady & (j % 2 == parity))
        def _():
            up_act(parity)
            down_proj(1 - parity)

    @pl.when(j == nj)
    def _():
        down_proj((nj - 1) % 2)
        if final_norm:
            out_ref[...] = _rms(out_ref[...], fw_ref[...])


def _ffn_prompt(x, layer, w, *, tm, tf, seq_len, final_w=None):
    T, D = x.shape
    nj = D_FF // tf
    ni = T // tm
    assert nj >= 2 and seq_len % tm == 0
    act_j = lambda j: jnp.minimum(j, nj - 1)
    down_j = lambda j: jnp.maximum(j - 1, 0)
    wug, wuv, wd = w["ffn_bf16"][layer]
    in_specs = [pl.BlockSpec((tm, D), lambda i, j: (i, 0), pipeline_mode=pl.Buffered(1)),
                pl.BlockSpec((None, 1, D), lambda i, j: (layer, 0, 0)),
                pl.BlockSpec((D, tf), lambda i, j: (0, act_j(j))),
                pl.BlockSpec((D, tf), lambda i, j: (0, act_j(j))),
                pl.BlockSpec((None, FFN_CONV, tf), lambda i, j: (layer, 0, act_j(j))),
                pl.BlockSpec((None, FFN_CONV, tf), lambda i, j: (layer, 0, nj + act_j(j))),
                pl.BlockSpec((None, 1, tf), lambda i, j: (layer, 0, act_j(j))),
                pl.BlockSpec((None, 1, tf), lambda i, j: (layer, 0, nj + act_j(j))),
                pl.BlockSpec((tf, D), lambda i, j: (down_j(j), 0))]
    args = [x, w["norm_ffn"], wug, wuv, w["f_conv_w"], w["f_conv_w"], w["f_conv_b"], w["f_conv_b"], wd]
    if final_w is not None:
        in_specs.append(pl.BlockSpec((1, D), lambda i, j: (0, 0)))
        args.append(final_w)
    tail_spec = pl.BlockSpec((1, SUBLANES, tf), lambda i, j: (i, 0, act_j(j)))
    return pl.pallas_call(
        functools.partial(_ffn_prompt_kernel, tm=tm, tiles_per_seq=seq_len // tm, nj=nj,
                          final_norm=final_w is not None),
        grid=(ni, nj + 1),
        in_specs=in_specs,
        out_specs=[pl.BlockSpec((tm, D), lambda i, j: (i, 0)), tail_spec, tail_spec],
        out_shape=[SDS((T, D), F32)] + [SDS((ni, SUBLANES, D_FF), F32)] * 2,
        scratch_shapes=[pltpu.VMEM((tm, D), BF16)]
        + [pltpu.VMEM((nj, SUBLANES, tf), F32)] * 2
        + [pltpu.VMEM((tm, tf), BF16)] * 2,
        compiler_params=_params(("arbitrary", "arbitrary")),
        name="ffn_prompt",
    )(*args)


def _mlstm_intra(q, ks, v, ig_col, lf_cum_col, lf_cum_row, ig_row, m_prev_col, causal):
    log_d = jnp.where(causal, lf_cum_col - lf_cum_row + ig_row, NEG_INF)
    inter = lf_cum_col + m_prev_col
    m_t = jnp.maximum(inter, jnp.max(log_d, axis=1, keepdims=True))
    d_m = jnp.exp(log_d - m_t)
    w_inter = jnp.exp(inter - m_t)
    kb = _pad_rows(ks.astype(BF16), causal.shape[1])
    vb = _pad_rows(v.astype(BF16), causal.shape[1])
    s = _dot_nt(q.astype(BF16), kb) * d_m
    num = _dot(s.astype(BF16), vb)
    den = jnp.sum(s, axis=1, keepdims=True)
    return m_t, w_inter, inter, num, den


def _mlstm_prompt_kernel(q_ref, k_ref, v_ref, o_ref, g_ref, br_ref,
                         hg_ref, co_ref, no_ref, mo_ref, c_s, n_s, m_s, *, c):
    ci = pl.program_id(1)

    @pl.when(ci == 0)
    def _():
        c_s[...] = jnp.zeros_like(c_s)
        n_s[...] = jnp.zeros_like(n_s)
        m_s[...] = jnp.zeros_like(m_s)

    g = g_ref[...] + br_ref[...]
    lf_cum = _cumsum_rows(_log_sigmoid(g))
    gt = g.T
    lf_cum_t = lf_cum.T
    row = lax.broadcasted_iota(jnp.int32, (c, c), 0)
    col = lax.broadcasted_iota(jnp.int32, (c, c), 1)
    causal = col <= row
    last = ci == pl.num_programs(1) - 1

    for h in range(A_HEADS):
        ig_col = g[:, h:h + 1]
        b_col = lf_cum[:, A_HEADS + h:A_HEADS + h + 1]
        ig_row = gt[h:h + 1, :]
        b_row = lf_cum_t[A_HEADS + h:A_HEADS + h + 1, :]
        q = q_ref[:, h * A_DK:(h + 1) * A_DK]
        ks = k_ref[:, h * A_DK:(h + 1) * A_DK] * (A_DK ** -0.5)
        v = v_ref[:, h * A_DV:(h + 1) * A_DV]
        m_prev = m_s[h][:, 0:1]
        m_t, w_inter, inter, num, den = _mlstm_intra(q, ks, v, ig_col, b_col, b_row, ig_row, m_prev, causal)
        cmat = c_s[h]
        nvec = n_s[h]
        num = w_inter * _dot(q.astype(BF16), cmat.astype(BF16)) + num
        den = w_inter * jnp.sum(q * nvec, axis=1, keepdims=True) + den
        hh = num / jnp.maximum(jnp.abs(den), jnp.exp(-m_t))
        hg_ref[:, h * A_DV:(h + 1) * A_DV] = (_sigmoid(o_ref[:, h * A_DV:(h + 1) * A_DV]) * hh).astype(BF16)

        m_new = m_t[c - 1:c, :]
        w_state = jnp.exp(b_col[c - 1:c, :] - b_col + ig_col - m_new)
        decay = jnp.exp(inter[c - 1:c, :] - m_new)
        kw = ks * w_state
        c_new = decay * cmat + _dot_tn(kw.astype(BF16), v.astype(BF16))
        n_new = decay * nvec + jnp.sum(kw, axis=0, keepdims=True)
        m_row = jnp.broadcast_to(m_new, (1, LANES))
        c_s[h] = c_new
        n_s[h] = n_new
        m_s[h] = m_row

        @pl.when(last)
        def _():
            co_ref[0, h] = c_new
            no_ref[0, h] = n_new
            mo_ref[0, h] = m_row


def _mlstm_prompt(p, gates, bias_row, *, nseq, seq_len, c):
    T = p.shape[0]
    nc = seq_len // c
    hk = A_HEADS * A_DK
    hv = A_HEADS * A_DV
    rows = lambda b, ci: b * nc + ci
    return pl.pallas_call(
        functools.partial(_mlstm_prompt_kernel, c=c),
        grid=(nseq, nc),
        in_specs=[pl.BlockSpec((c, hk), lambda b, ci: (rows(b, ci), 0)),
                  pl.BlockSpec((c, hk), lambda b, ci: (rows(b, ci), 1)),
                  pl.BlockSpec((c, hv), lambda b, ci: (rows(b, ci), 1)),
                  pl.BlockSpec((c, hv), lambda b, ci: (rows(b, ci), 2)),
                  pl.BlockSpec((c, LANES), lambda b, ci: (rows(b, ci), 0)),
                  pl.BlockSpec((1, LANES), lambda b, ci: (0, 0))],
        out_specs=[pl.BlockSpec((c, hv), lambda b, ci: (rows(b, ci), 0)),
                   pl.BlockSpec((1, A_HEADS, A_DK, A_DV), lambda b, ci: (b, 0, 0, 0)),
                   pl.BlockSpec((1, A_HEADS, 1, A_DK), lambda b, ci: (b, 0, 0, 0)),
                   pl.BlockSpec((1, A_HEADS, 1, LANES), lambda b, ci: (b, 0, 0, 0))],
        out_shape=[SDS((T, hv), BF16),
                   SDS((nseq, A_HEADS, A_DK, A_DV), F32),
                   SDS((nseq, A_HEADS, 1, A_DK), F32),
                   SDS((nseq, A_HEADS, 1, LANES), F32)],
        scratch_shapes=[pltpu.VMEM((A_HEADS, A_DK, A_DV), F32), pltpu.VMEM((A_HEADS, 1, A_DK), F32),
                        pltpu.VMEM((A_HEADS, 1, LANES), F32)],
        compiler_params=_params(("arbitrary", "arbitrary")),
        name="mlstm_prompt",
    )(p, p, p, p, gates, bias_row)


def _mlstm_sample_kernel(q_ref, k_ref, v_ref, o_ref, g_ref, br_ref, mp_ref, ci_ref, ni_ref,
                         hg_ref, co_ref, no_ref, mo_ref, *, seq_len):
    nbs = SEQ_GROUP
    rs = nbs * seq_len
    g = g_ref[...] + br_ref[...]
    lf_all = _log_sigmoid(g)
    m_prev_all = mp_ref[...]
    row = lax.broadcasted_iota(jnp.int32, (rs, LANES), 0)
    col = lax.broadcasted_iota(jnp.int32, (rs, LANES), 1)
    same = ((row % nbs) == (col % nbs)) & (col < rs)
    causal = same & (col <= row)
    causal_t = same & (row <= col)
    eye = row == col
    last = same & (col // nbs == seq_len - 1)
    seq_of_row = lax.broadcasted_iota(jnp.int32, (rs, 1), 0) % nbs
    prow = lax.broadcasted_iota(jnp.int32, (LANES, 1), 0)
    seq_of_prow = jnp.where(prow < rs, prow % nbs, -1)

    def to_row(colvec):
        return jnp.sum(jnp.where(eye, colvec, 0.0), axis=0, keepdims=True)

    def at_last(rowvec):
        return jnp.sum(jnp.where(last, rowvec, 0.0), axis=1, keepdims=True)

    m_out = jnp.zeros((rs, LANES), F32)
    for h in range(A_HEADS):
        ig_col = g[:, h:h + 1]
        lf_col = lf_all[:, A_HEADS + h:A_HEADS + h + 1]
        m_prev = m_prev_all[:, h:h + 1]
        ig_row = to_row(ig_col)
        lf_row = to_row(lf_col)
        b_col = jnp.sum(jnp.where(causal, lf_row, 0.0), axis=1, keepdims=True)
        b_row = jnp.sum(jnp.where(causal_t, lf_col, 0.0), axis=0, keepdims=True)
        q = q_ref[:, h * A_DK:(h + 1) * A_DK]
        ks = k_ref[:, h * A_DK:(h + 1) * A_DK] * (A_DK ** -0.5)
        v = v_ref[:, h * A_DV:(h + 1) * A_DV]
        m_t, w_inter, inter, num, den = _mlstm_intra(q, ks, v, ig_col, b_col, b_row, ig_row, m_prev, causal)
        qb = q.astype(BF16)
        num_i = jnp.zeros((rs, A_DV), F32)
        qn = jnp.zeros((rs, 1), F32)
        for bb in range(nbs):
            mine = seq_of_row == bb
            num_i = jnp.where(mine, _dot(qb, ci_ref[bb, h].astype(BF16)), num_i)
            qn = jnp.where(mine, jnp.sum(q * ni_ref[bb, h], axis=1, keepdims=True), qn)
        num = w_inter * num_i + num
        den = w_inter * qn + den
        hh = num / jnp.maximum(jnp.abs(den), jnp.exp(-m_t))
        hg_ref[:, h * A_DV:(h + 1) * A_DV] = _sigmoid(o_ref[:, h * A_DV:(h + 1) * A_DV]) * hh

        m_new = at_last(to_row(m_t))
        w_state = jnp.exp(at_last(b_row) - b_col + ig_col - m_new)
        decay = jnp.exp(at_last(to_row(inter)) - m_new)
        kw = ks * w_state
        kw_t = _pad_rows(kw, LANES).T.astype(BF16)
        v_pad = _pad_rows(v, LANES)
        for bb in range(nbs):
            v_mine = jnp.where(seq_of_prow == bb, v_pad, 0.0).astype(BF16)
            dec = decay[bb:bb + 1, :]
            co_ref[bb, h] = dec * ci_ref[bb, h] + _dot(kw_t, v_mine)
            no_ref[bb, h] = dec * ni_ref[bb, h] + jnp.sum(jnp.where(seq_of_row == bb, kw, 0.0), axis=0, keepdims=True)
        m_out = jnp.where(col == h, m_new, m_out)
    mo_ref[...] = m_out


def _mlstm_sample(p, gates, bias_row, m_rows, c_in, n_in, *, seq_len):
    T = p.shape[0]
    nbs = SEQ_GROUP
    rs = nbs * seq_len
    nb = c_in.shape[0]
    hk = A_HEADS * A_DK
    hv = A_HEADS * A_DV
    return pl.pallas_call(
        functools.partial(_mlstm_sample_kernel, seq_len=seq_len),
        grid=(nb // nbs,),
        in_specs=[pl.BlockSpec((rs, hk), lambda i: (i, 0)),
                  pl.BlockSpec((rs, hk), lambda i: (i, 1)),
                  pl.BlockSpec((rs, hv), lambda i: (i, 1)),
                  pl.BlockSpec((rs, hv), lambda i: (i, 2)),
                  pl.BlockSpec((rs, LANES), lambda i: (i, 0)),
                  pl.BlockSpec((1, LANES), lambda i: (0, 0)),
                  pl.BlockSpec((rs, LANES), lambda i: (i, 0)),
                  pl.BlockSpec((nbs, A_HEADS, A_DK, A_DV), lambda i: (i, 0, 0, 0)),
                  pl.BlockSpec((nbs, A_HEADS, 1, A_DK), lambda i: (i, 0, 0, 0))],
        out_specs=[pl.BlockSpec((rs, hv), lambda i: (i, 0)),
                   pl.BlockSpec((nbs, A_HEADS, A_DK, A_DV), lambda i: (i, 0, 0, 0)),
                   pl.BlockSpec((nbs, A_HEADS, 1, A_DK), lambda i: (i, 0, 0, 0)),
                   pl.BlockSpec((rs, LANES), lambda i: (i, 0))],
        out_shape=[SDS((T, hv), F32),
                   SDS((nb, A_HEADS, A_DK, A_DV), F32),
                   SDS((nb, A_HEADS, 1, A_DK), F32),
                   SDS((T, LANES), F32)],
        compiler_params=_params(("arbitrary",)),
        name="mlstm_sample",
    )(p, p, p, p, gates, bias_row, m_rows, c_in, n_in)


def _pair_select(a, h0):
    lane = lax.broadcasted_iota(jnp.int32, (a.shape[0], LANES), 1)
    return jnp.where(lane < B_HEADDIM, a[:, h0:h0 + 1], a[:, h0 + 1:h0 + 2])


def _group_heads_first(a, g):
    return pltpu.roll(a, (LANES - HEADS_PER_GROUP * g) % LANES, 1)


def _group_norm_gate(y, z, nw):
    gz = y * _silu(z)
    return gz * lax.rsqrt(jnp.mean(gz * gz, axis=-1, keepdims=True) + EPS) * nw


def _ssd_prompt_kernel(x_ref, b_ref, c_ref, z_ref, dtc_ref,
                       cwx_ref, cwb_ref, cwc_ref, cbx_ref, cbb_ref, cbc_ref,
                       dbr_ref, alr_ref, d_ref, nw_ref,
                       y_ref, so_ref, s_s, xp_s, bp_s, cp_s, *, c, groups_per_step):
    ci = pl.program_id(2)

    @pl.when(ci == 0)
    def _():
        s_s[...] = jnp.zeros_like(s_s)
        xp_s[...] = jnp.zeros_like(xp_s)
        bp_s[...] = jnp.zeros_like(bp_s)
        cp_s[...] = jnp.zeros_like(cp_s)

    for gi in range(groups_per_step):
        _ssd_prompt_group(x_ref, b_ref, c_ref, z_ref, dtc_ref, cwx_ref, cwb_ref, cwc_ref, cbx_ref, cbb_ref, cbc_ref,
                          dbr_ref, alr_ref, d_ref, nw_ref, y_ref, s_s, xp_s, bp_s, cp_s,
                          c=c, gi=gi, group=pl.program_id(1) * groups_per_step + gi)

    @pl.when(ci == pl.num_programs(2) - 1)
    def _():
        so_ref[0] = s_s[...]


def _ssd_prompt_group(x_ref, b_ref, c_ref, z_ref, dtc_ref, cwx_ref, cwb_ref, cwc_ref, cbx_ref, cbb_ref, cbc_ref,
                      dbr_ref, alr_ref, d_ref, nw_ref, y_ref, s_s, xp_s, bp_s, cp_s, *, c, gi, group):
    ch = slice(gi * GROUP_CH, (gi + 1) * GROUP_CH)
    ns = slice(gi * B_DSTATE, (gi + 1) * B_DSTATE)
    ls = slice(gi * LANES, (gi + 1) * LANES)

    def conv(raw_ref, prev_s, cw_ref, cb_ref, sl):
        raw = raw_ref[:, sl]
        prev = prev_s[:, sl]
        out = _conv_taps(raw, lambda k: _shift_with_prev(raw, prev, k), cw_ref.at[:, sl], cb_ref.at[:, sl], B_DCONV)
        prev_s[:, sl] = raw[c - SUBLANES:, :]
        return _silu(out)

    x = conv(x_ref, xp_s, cwx_ref, cbx_ref, ch)
    bm = conv(b_ref, bp_s, cwb_ref, cbb_ref, ns)
    cm = conv(c_ref, cp_s, cwc_ref, cbc_ref, ns)

    dt_c = _softplus(_group_heads_first(dtc_ref[...], group) + dbr_ref[:, ls])
    ac_c = _cumsum_rows(dt_c * (-jnp.exp(alr_ref[:, ls])))
    ac_c2 = ac_c * LOG2E
    src_r2 = (ac_c2 - jnp.log(dt_c) * LOG2E).T[:HEADS_PER_GROUP, :]

    row = lax.broadcasted_iota(jnp.int32, (c, c), 0)
    col = lax.broadcasted_iota(jnp.int32, (c, c), 1)
    bmb = bm.astype(BF16)
    cmb = cm.astype(BF16)
    cb = jnp.where(col <= row, _dot_nt(cmb, bmb), 0.0)
    state = s_s[gi]
    y_inter = _dot(cmb, state.astype(BF16))
    ac_last = ac_c[c - 1:c, :]
    to_end = jnp.exp(ac_last - ac_c) * dt_c
    e_ac = jnp.exp(ac_c)
    e_last = jnp.exp(ac_last)
    lane = lax.broadcasted_iota(jnp.int32, (c, LANES), 1)
    d_all = d_ref[:, ch]

    ys = []
    for pr in range(HEADS_PER_GROUP // 2):
        h0 = 2 * pr
        sl = slice(pr * LANES, (pr + 1) * LANES)
        xp = x[:, sl]
        xpb = xp.astype(BF16)
        full = []
        for r in (h0, h0 + 1):
            blocks = []
            for rb in range(c // LANES):
                rsl = slice(rb * LANES, (rb + 1) * LANES)
                ncol = (rb + 1) * LANES
                fac = jnp.exp2(jnp.minimum(ac_c2[rsl, r:r + 1] - src_r2[r:r + 1, :ncol], MAX_EXP2))
                blocks.append(_dot((cb[rsl, :ncol] * fac).astype(BF16), xpb[:ncol]))
            full.append(jnp.concatenate(blocks, axis=0))
        yp = jnp.where(lane < B_HEADDIM, full[0], full[1])
        yp = yp + y_inter[:, sl] * _pair_select(e_ac, h0) + d_all[:, sl] * xp
        ys.append(yp)
        xw = (xp * _pair_select(to_end, h0)).astype(BF16)
        s_s[gi, :, sl] = _pair_select(e_last, h0) * state[:, sl] + _dot_tn(bmb, xw)
    y = jnp.concatenate(ys, axis=1)
    y_ref[:, ch] = _group_norm_gate(y, z_ref[:, ch], nw_ref[:, ch]).astype(BF16)


def _ssd_prompt(p, dt_cols, w, *, nseq, seq_len, c, groups_per_step):
    T = p.shape[0]
    nc = seq_len // c
    gps = groups_per_step
    steps = B_GROUPS // gps
    xw_, nw_, lw_ = gps * GROUP_CH, gps * B_DSTATE, gps * LANES
    rows = lambda b, g, ci: b * nc + ci
    kx = B_DINNER // xw_
    kb = (2 * B_DINNER) // nw_
    kc = kb + steps
    wb = B_DINNER // nw_
    wc = wb + steps
    per_lane = lambda width, start=0: pl.BlockSpec((1, width), lambda b, g, ci: (0, start + g))
    return pl.pallas_call(
        functools.partial(_ssd_prompt_kernel, c=c, groups_per_step=gps),
        grid=(nseq, steps, nc),
        in_specs=[pl.BlockSpec((c, xw_), lambda b, g, ci: (rows(b, g, ci), kx + g)),
                  pl.BlockSpec((c, nw_), lambda b, g, ci: (rows(b, g, ci), kb + g)),
                  pl.BlockSpec((c, nw_), lambda b, g, ci: (rows(b, g, ci), kc + g)),
                  pl.BlockSpec((c, xw_), lambda b, g, ci: (rows(b, g, ci), g)),
                  pl.BlockSpec((c, LANES), lambda b, g, ci: (rows(b, g, ci), 0)),
                  pl.BlockSpec((None, B_DCONV, xw_), lambda b, g, ci: (0, 0, g)),
                  pl.BlockSpec((None, B_DCONV, nw_), lambda b, g, ci: (0, 0, wb + g)),
                  pl.BlockSpec((None, B_DCONV, nw_), lambda b, g, ci: (0, 0, wc + g)),
                  per_lane(xw_), per_lane(nw_, wb), per_lane(nw_, wc),
                  per_lane(lw_), per_lane(lw_),
                  per_lane(xw_), per_lane(xw_)],
        out_specs=[pl.BlockSpec((c, xw_), lambda b, g, ci: (rows(b, g, ci), g)),
                   pl.BlockSpec((1, gps, B_DSTATE, GROUP_CH), lambda b, g, ci: (b, g, 0, 0))],
        out_shape=[SDS((T, B_DINNER), BF16), SDS((nseq, B_GROUPS, B_DSTATE, GROUP_CH), F32)],
        scratch_shapes=[pltpu.VMEM((gps, B_DSTATE, GROUP_CH), F32),
                        pltpu.VMEM((SUBLANES, xw_), F32),
                        pltpu.VMEM((SUBLANES, nw_), F32),
                        pltpu.VMEM((SUBLANES, nw_), F32)],
        compiler_params=_params(("arbitrary", "arbitrary", "arbitrary")),
        name="ssd_prompt",
    )(p, p, p, p, dt_cols, w["b_conv_w"], w["b_conv_w"], w["b_conv_w"],
      w["b_conv_b"], w["b_conv_b"], w["b_conv_b"],
      w["dtb_cols"], w["alog_cols"], w["d_lane"], w["b_norm_w"])


def _ssd_sample_kernel(x_ref, b_ref, c_ref, z_ref, dt_ref, sx_ref, sb_ref, sc_ref,
                       cwx_ref, cwb_ref, cwc_ref, cbx_ref, cbb_ref, cbc_ref,
                       dtb_ref, al_ref, d_ref, nw_ref, hs_ref,
                       y_ref, hso_ref, *, seq_len, groups_per_step):
    for gi in range(groups_per_step):
        _ssd_sample_group(x_ref, b_ref, c_ref, z_ref, dt_ref, sx_ref, sb_ref, sc_ref,
                          cwx_ref, cwb_ref, cwc_ref, cbx_ref, cbb_ref, cbc_ref,
                          dtb_ref, al_ref, d_ref, nw_ref, hs_ref, y_ref, hso_ref,
                          seq_len=seq_len, gi=gi, group=pl.program_id(1) * groups_per_step + gi)


def _ssd_sample_group(x_ref, b_ref, c_ref, z_ref, dt_ref, sx_ref, sb_ref, sc_ref,
                      cwx_ref, cwb_ref, cwc_ref, cbx_ref, cbb_ref, cbc_ref,
                      dtb_ref, al_ref, d_ref, nw_ref, hs_ref, y_ref, hso_ref, *, seq_len, gi, group):
    nbs = SEQ_GROUP
    rs = nbs * seq_len
    sr = _SampleRows(rs, seq_len)
    prow = lax.broadcasted_iota(jnp.int32, (LANES, 1), 0)
    seq_of_prow = jnp.where(prow < rs, prow % nbs, -1)
    ch = slice(gi * GROUP_CH, (gi + 1) * GROUP_CH)
    ns = slice(gi * B_DSTATE, (gi + 1) * B_DSTATE)
    ls = slice(gi * LANES, (gi + 1) * LANES)
    h_first = gi * HEADS_PER_GROUP

    def conv(raw_ref, st_ref, cw_ref, cb_ref, sl):
        raw = raw_ref[:, sl]
        return _silu(_conv_taps(raw, sr.conv_shift(raw, st_ref[:, sl], B_DCONV), cw_ref.at[:, sl], cb_ref.at[:, sl],
                                B_DCONV))

    x = conv(x_ref, sx_ref, cwx_ref, cbx_ref, ch)
    bm = conv(b_ref, sb_ref, cwb_ref, cbb_ref, ns)
    cm = conv(c_ref, sc_ref, cwc_ref, cbc_ref, ns)

    dt = _softplus(_group_heads_first(dt_ref[...], group) + dtb_ref[:, ls])
    a = dt * (-jnp.exp(al_ref[:, ls]))
    acum = a
    total = a
    for k in range(1, seq_len):
        acum = acum + sr.back(a, k)
        total = total + sr.back(a, k) + sr.ahead(a, k)

    coefs = []
    for d in range(seq_len):
        cb_d = jnp.sum(cm * sr.back(bm, d), axis=1, keepdims=True)
        coefs.append(cb_d * jnp.exp(acum - sr.back(acum, d)) * sr.back(dt, d))
    x_back = [sr.back(x, d) for d in range(seq_len)]

    e_ac = jnp.exp(acum)
    e_tot = jnp.exp(total)
    to_end = jnp.exp(total - acum) * dt
    d_all = d_ref[:, ch]

    ys = []
    xws = []
    for pr in range(HEADS_PER_GROUP // 2):
        h0 = 2 * pr
        sl = slice(pr * LANES, (pr + 1) * LANES)
        yp = d_all[:, sl] * x[:, sl]
        for d in range(seq_len):
            yp = yp + _pair_select(coefs[d], h0) * x_back[d][:, sl]
        ys.append(yp)
        xws.append(x[:, sl] * _pair_select(to_end, h0))
    xw_t = _pad_rows(jnp.concatenate(xws, axis=1), LANES).T.astype(BF16)
    cmb = cm.astype(BF16)
    bm_pad = _pad_rows(bm, LANES)
    y_int = jnp.zeros((rs, GROUP_CH), F32)
    for bb in range(nbs):
        hs_g = hs_ref[bb, h_first:h_first + HEADS_PER_GROUP].reshape(GROUP_CH, B_DSTATE)
        y_int = jnp.where(sr.seq == bb, _dot_nt(cmb, hs_g.astype(BF16)), y_int)
        b_mine = jnp.where(seq_of_prow == bb, bm_pad, 0.0).astype(BF16)
        upd = _dot(xw_t, b_mine)
        for r in range(HEADS_PER_GROUP):
            dec = e_tot[bb:bb + 1, r:r + 1]
            hso_ref[bb, h_first + r] = dec * hs_ref[bb, h_first + r] + upd[r * B_HEADDIM:(r + 1) * B_HEADDIM, :]
    e_pairs = jnp.concatenate([_pair_select(e_ac, 2 * pr) for pr in range(HEADS_PER_GROUP // 2)], axis=1)
    y = jnp.concatenate(ys, axis=1) + y_int * e_pairs
    y_ref[:, ch] = _group_norm_gate(y, z_ref[:, ch], nw_ref[:, ch])


def _ssd_sample(p, dt_cols, conv_state, w, hs, *, seq_len, groups_per_step):
    T = p.shape[0]
    nbs = SEQ_GROUP
    rs = nbs * seq_len
    nb = hs.shape[0]
    gps = groups_per_step
    steps = B_GROUPS // gps
    xw_, nw_, lw_ = gps * GROUP_CH, gps * B_DSTATE, gps * LANES
    kx = B_DINNER // xw_
    kb = (2 * B_DINNER) // nw_
    kc = kb + steps
    wb = B_DINNER // nw_
    wc = wb + steps
    per_lane = lambda width, start=0: pl.BlockSpec((1, width), lambda i, g: (0, start + g))
    hs_spec = pl.BlockSpec((nbs, gps * HEADS_PER_GROUP, B_HEADDIM, B_DSTATE), lambda i, g: (i, g, 0, 0))
    return pl.pallas_call(
        functools.partial(_ssd_sample_kernel, seq_len=seq_len, groups_per_step=gps),
        grid=(nb // nbs, steps),
        in_specs=[pl.BlockSpec((rs, xw_), lambda i, g: (i, kx + g)),
                  pl.BlockSpec((rs, nw_), lambda i, g: (i, kb + g)),
                  pl.BlockSpec((rs, nw_), lambda i, g: (i, kc + g)),
                  pl.BlockSpec((rs, xw_), lambda i, g: (i, g)),
                  pl.BlockSpec((rs, LANES), lambda i, g: (i, 0)),
                  pl.BlockSpec((rs, xw_), lambda i, g: (i, g)),
                  pl.BlockSpec((rs, nw_), lambda i, g: (i, wb + g)),
                  pl.BlockSpec((rs, nw_), lambda i, g: (i, wc + g)),
                  pl.BlockSpec((None, B_DCONV, xw_), lambda i, g: (0, 0, g)),
                  pl.BlockSpec((None, B_DCONV, nw_), lambda i, g: (0, 0, wb + g)),
                  pl.BlockSpec((None, B_DCONV, nw_), lambda i, g: (0, 0, wc + g)),
                  per_lane(xw_), per_lane(nw_, wb), per_lane(nw_, wc),
                  per_lane(lw_), per_lane(lw_), per_lane(xw_), per_lane(xw_),
                  hs_spec],
        out_specs=[pl.BlockSpec((rs, xw_), lambda i, g: (i, g)), hs_spec],
        out_shape=[SDS((T, B_DINNER), F32), SDS(hs.shape, F32)],
        compiler_params=_params(("arbitrary", "arbitrary")),
        name="ssd_sample",
    )(p, p, p, p, dt_cols, conv_state, conv_state, conv_state,
      w["b_conv_w"], w["b_conv_w"], w["b_conv_w"], w["b_conv_b"], w["b_conv_b"], w["b_conv_b"],
      w["dtb_cols"], w["alog_cols"], w["d_lane"], w["b_norm_w"], hs)


def _pad_lanes(a, width=LANES):
    return jnp.pad(a, ((0, 0), (0, width - a.shape[1])))


def _per_group_lanes(a):
    r = a.shape[0]
    a = a.reshape(r, B_GROUPS, HEADS_PER_GROUP)
    return jnp.pad(a, ((0, 0), (0, 0), (0, LANES - HEADS_PER_GROUP))).reshape(r, B_GROUPS * LANES)


def _to_sample_rows(a, seq_len):
    nseq, nt, ch = a.shape
    a = a.reshape(nseq // SEQ_GROUP, SEQ_GROUP, nt, ch).transpose(0, 2, 1, 3)
    a = jnp.pad(a, ((0, 0), (0, seq_len - nt), (0, 0), (0, 0)))
    return a.reshape(nseq * seq_len, ch)


def _from_sample_rows(a, nseq, seq_len, first_token=0):
    ch = a.shape[1]
    a = a.reshape(nseq // SEQ_GROUP, seq_len, SEQ_GROUP, ch)[:, first_token:]
    return a.transpose(0, 2, 1, 3).reshape(nseq, seq_len - first_token, ch)


def _forward(x, nseq, seq_len, state, w, *, prompt):
    T = x.shape[0]
    tm = PROJ_ROWS if prompt else T
    a_main = 2 * A_HEADS * A_DK + 2 * A_HEADS * A_DV
    b_main = B_DINNER + B_CONV_DIM
    nstate = FFN_CONV - 1
    cast = {"ffn_bf16": [None, None]}

    def keep_cast(name, outs):
        if not prompt:
            cast[name] = outs[-1]

    outs = _norm_matmul(x, w["norm_mix"], 0, w["a_w_in"], a_main, w["a_w_gate"], tm=tm, tn=PROJ_COLS)
    p_a, gates = outs[:2]
    keep_cast("a_w_in", outs)
    if prompt:
        hg, m_c, m_n, m_m = _mlstm_prompt(p_a, gates, w["a_b_row"], nseq=nseq, seq_len=seq_len, c=MLSTM_CHUNK)
        m_m = m_m[:, :, 0, 0]
    else:
        c0, n0, m0 = state["mlstm"]
        m_rows = _pad_lanes(_to_sample_rows(jnp.broadcast_to(m0[:, None, :], (nseq, seq_len, A_HEADS)), seq_len))
        hg, m_c, m_n, m_rows_new = _mlstm_sample(p_a, gates, w["a_b_row"], m_rows, c0, n0[:, :, None, :],
                                                 seq_len=seq_len)
        m_m = _from_sample_rows(m_rows_new, nseq, seq_len, seq_len - 1)[:, 0, :A_HEADS]
    m_n = m_n[:, :, 0, :]
    outs = _matmul_res(hg, w["a_w_out"], x, tm=tm, tn=OUT_COLS)
    x = outs[0]
    keep_cast("a_w_out", outs)

    def ffn(x, layer, final_w=None):
        if prompt:
            x_new, tg, tv = _ffn_prompt(x, layer, w, tm=FFN_ROWS, tf=FFN_COLS, seq_len=seq_len, final_w=final_w)
            tps = seq_len // FFN_ROWS
            pick = lambda t: t.reshape(nseq, tps, SUBLANES, D_FF)[:, tps - 1, SUBLANES - nstate:]
            tails = jnp.concatenate([pick(tg), pick(tv)], axis=-1)
        else:
            outs = _ffn_sample(x, layer, w, state["ffn_conv"], tf=FFN_SAMPLE_COLS, seq_len=seq_len, final_w=final_w)
            x_new = outs[0]
            tails = jnp.concatenate(outs[1:3], axis=-1)
            cast["ffn_bf16"][layer] = tuple(outs[3:])
        return x_new, tails

    x, f_conv0 = ffn(x, 0)

    outs = _norm_matmul(x, w["norm_mix"], 1, w["b_w_in"], b_main, w["b_w_dt"], tm=tm, tn=PROJ_COLS)
    p_b, dt_raw = outs[:2]
    keep_cast("b_w_in", outs)
    if prompt:
        s_conv = p_b.reshape(nseq, seq_len, b_main)[:, seq_len - (B_DCONV - 1):, B_DINNER:]
        yn, s_t = _ssd_prompt(p_b, dt_raw, w, nseq=nseq, seq_len=seq_len, c=SCAN_CHUNK,
                              groups_per_step=SSD_PROMPT_GROUPS)
        ssm = s_t.reshape(nseq, B_GROUPS, B_DSTATE, HEADS_PER_GROUP, B_HEADDIM)
        ssm = ssm.transpose(0, 1, 3, 4, 2).reshape(nseq, B_HEADS, B_HEADDIM, B_DSTATE)
    else:
        s_conv = _from_sample_rows(p_b, nseq, seq_len, seq_len - (B_DCONV - 1))[:, :, B_DINNER:]
        st = _to_sample_rows(state["ssm_conv"], seq_len)
        yn, ssm = _ssd_sample(p_b, dt_raw, st, w, state["ssm"], seq_len=seq_len, groups_per_step=SSD_SAMPLE_GROUPS)
    outs = _matmul_res(yn, w["b_w_out"], x, tm=tm, tn=OUT_COLS)
    x = outs[0]
    keep_cast("b_w_out", outs)

    y, f_conv1 = ffn(x, 1, final_w=w["norm_final"])
    return (y, m_c[None], m_n[None], m_m[None], ssm[None], s_conv[None], jnp.stack([f_conv0, f_conv1]), cast)


def kernel(x_prompt, x_sample, state_mlstm_C, state_mlstm_n, state_mlstm_m, state_ssm, state_ssm_conv,
           state_ffn_conv, norm_mix, norm_ffn, norm_final, a_w_in, a_b_if, a_w_out, b_w_in, b_conv_w,
           b_conv_b, b_dt_bias, b_A_log, b_D, b_norm_w, b_w_out, f_w_up, f_conv_w, f_conv_b, f_w_down):
    a_main = 2 * A_HEADS * A_DK + 2 * A_HEADS * A_DV
    b_main = B_DINNER + B_CONV_DIM
    w = {
        "norm_mix": norm_mix[:, None], "norm_ffn": norm_ffn[:, None], "norm_final": norm_final[None],
        "a_w_in": jnp.swapaxes(a_w_in, 1, 2),
        "a_w_gate": _pad_lanes(a_w_in[0, :, a_main:]).astype(BF16),
        "a_b_row": _pad_lanes(a_b_if),
        "a_w_out": a_w_out,
        "b_w_in": jnp.swapaxes(b_w_in, 1, 2),
        "b_w_dt": _pad_lanes(b_w_in[0, :, b_main:]).astype(BF16),
        "b_conv_w": b_conv_w, "b_conv_b": b_conv_b,
        "dtb_cols": _per_group_lanes(b_dt_bias), "alog_cols": _per_group_lanes(b_A_log),
        "d_lane": jnp.repeat(b_D, B_HEADDIM, axis=1), "b_norm_w": b_norm_w,
        "b_w_out": b_w_out,
        "f_w_up": f_w_up, "f_conv_w": f_conv_w, "f_conv_b": f_conv_b[:, None],
        "f_w_down": f_w_down,
    }
    bp, lp, _ = x_prompt.shape
    bs, ls, _ = x_sample.shape
    state = {"mlstm": (state_mlstm_C[0], state_mlstm_n[0], state_mlstm_m[0]),
             "ssm": state_ssm[0], "ssm_conv": state_ssm_conv[0], "ffn_conv": state_ffn_conv}
    ys, sc, sn, sm, sssm, ssconv, sfconv, w_bf16 = _forward(_to_sample_rows(x_sample, ls), bs, ls, state, w,
                                                            prompt=False)
    yp, pc, pn, pm, pssm, psconv, pfconv, _ = _forward(x_prompt.reshape(bp * lp, D_MODEL), bp, lp, None,
                                                       {**w, **w_bf16}, prompt=True)
    return (yp.reshape(bp, lp, D_MODEL), _from_sample_rows(ys, bs, ls), pc, sc, pn, sn, pm, sm,
            pssm, sssm, psconv, ssconv, pfconv, sfconv)
```
